```python
import math
import jax, jax.numpy as jnp
from jax import lax
import numpy as np

D_MODEL = 1024
BATCH = 2
SEQ = 16384
DEPTH = 4
DEC_BATCH = 32
DEC_SEQ = 32
PAST_LEN = 1024

CHUNK = 64
Q_BLOCK = 128
N_HEADS = 8
QK_NOPE = 64
QK_ROPE = 32
V_HEAD = 64
Q_RANK = 384
KV_RANK = 256
ROPE_THETA = 10000.0
C_CONV = 512
CONV_W = 31
D_FF = 2816
N_EXPERTS = 8
TOP_K = 2
D_FF_EXPERT = 3584
MOE_BLOCK = 256
N_DENSE = (DEPTH + 1) // 2
N_MOE = DEPTH // 2
ALPHA = (2 * DEPTH) ** 0.25
BETA = (8 * DEPTH) ** -0.25
ATTN_SCALE = (QK_NOPE + QK_ROPE) ** -0.5
LN_EPS = 1e-5
NEG_INF = -1e30
Q_OUT = N_HEADS * (QK_NOPE + QK_ROPE)
P_IN = Q_RANK + KV_RANK + QK_ROPE + 2 * C_CONV + 2 * D_MODEL
SPLITS = list(np.cumsum([Q_RANK, KV_RANK, QK_ROPE, 2 * C_CONV, D_MODEL]))

kernel_name = 'mla_conformer_gated_hybrid_stream_step'


def layernorm(x, g, b):
    xf = x.astype(jnp.float32)
    mu = jnp.mean(xf, axis=-1, keepdims=True)
    var = jnp.mean(jnp.square(xf - mu), axis=-1, keepdims=True)
    return ((xf - mu) * lax.rsqrt(var + LN_EPS) * g + b).astype(x.dtype)


def rmsnorm(x, g):
    xf = x.astype(jnp.float32)
    return (xf * lax.rsqrt(jnp.mean(jnp.square(xf), axis=-1, keepdims=True) + LN_EPS) * g).astype(x.dtype)


def rope_cos_sin(pos):
    inv = ROPE_THETA ** (-jnp.arange(QK_ROPE // 2, dtype=jnp.float32) * (2.0 / QK_ROPE))
    ang = pos.astype(jnp.float32)[:, None] * inv[None, :]
    return jnp.cos(ang), jnp.sin(ang)


def apply_rope(x, cos, sin):
    x1, x2 = jnp.split(x.astype(jnp.float32), 2, axis=-1)
    return jnp.concatenate([x1 * cos - x2 * sin, x1 * sin + x2 * cos], axis=-1).astype(x.dtype)


def _attend(q_n, q_r, q_pos, k_n, k_r, v, k_pos):
    s = jnp.einsum('bqhd,bkhd->bhqk', q_n, k_n) + jnp.einsum('bqhd,bkd->bhqk', q_r, k_r)
    s = s.astype(jnp.float32) * ATTN_SCALE
    mask = (k_pos // CHUNK)[None, :] <= (q_pos // CHUNK)[:, None]
    p = jax.nn.softmax(jnp.where(mask[None, None], s, NEG_INF), axis=-1).astype(v.dtype)
    return jnp.einsum('bhqk,bkhd->bqhd', p, v)


def chunk_causal_attention(q_n, q_r, q_pos, k_n, k_r, v, k_pos):
    b, t = q_n.shape[0], q_n.shape[1]
    if t > Q_BLOCK and t % Q_BLOCK == 0:
        nb = t // Q_BLOCK
        blocks = lambda a: jnp.moveaxis(a.reshape(b, nb, Q_BLOCK, *a.shape[2:]), 1, 0)
        o = lax.map(lambda args: _attend(args[0], args[1], args[2], k_n, k_r, v, k_pos),
                    (blocks(q_n), blocks(q_r), q_pos.reshape(nb, Q_BLOCK)))
        return jnp.moveaxis(o, 0, 1).reshape(b, t, *o.shape[3:])
    return _attend(q_n, q_r, q_pos, k_n, k_r, v, k_pos)


def causal_depthwise_conv(u_ext, w, b):
    out = lax.conv_general_dilated(u_ext, w[:, None, :], (1,), 'VALID',
                                   dimension_numbers=('NWC', 'WIO', 'NWC'),
                                   feature_group_count=u_ext.shape[-1])
    return out + b


def token_mixer(h, pos, ckv_past, kr_past, conv_past, w_in, b_in, g_qnorm, w_uq, g_kvnorm, w_uk, w_uv,
                w_o_attn, conv_w, conv_b, conv_ln_g, conv_ln_b, w_conv_out, w_out):
    bsz, t, _ = h.shape
    p = h @ w_in + b_in
    c_q, c_kv, k_r, u_pair, gate_a, gate_c = jnp.split(p, SPLITS, axis=-1)
    cos, sin = rope_cos_sin(pos)
    q = (rmsnorm(c_q, g_qnorm) @ w_uq).reshape(bsz, t, N_HEADS, QK_NOPE + QK_ROPE)
    q_n, q_r = q[..., :QK_NOPE], apply_rope(q[..., QK_NOPE:], cos[:, None, :], sin[:, None, :])
    ckv_new = rmsnorm(c_kv, g_kvnorm)
    kr_new = apply_rope(k_r, cos, sin)
    if ckv_past is None:
        ckv_all, kr_all, k_pos = ckv_new, kr_new, pos
    else:
        ckv_all = jnp.concatenate([ckv_past, ckv_new], axis=1)
        kr_all = jnp.concatenate([kr_past, kr_new], axis=1)
        k_pos = jnp.concatenate([jnp.arange(ckv_past.shape[1], dtype=jnp.int32), pos])
    k_n = jnp.einsum('btr,rhd->bthd', ckv_all, w_uk)
    v = jnp.einsum('btr,rhd->bthd', ckv_all, w_uv)
    attn = chunk_causal_attention(q_n, q_r, pos, k_n, kr_all, v, k_pos)
    branch_a = attn.reshape(bsz, t, N_HEADS * V_HEAD) @ w_o_attn
    u = u_pair[..., :C_CONV] * jax.nn.sigmoid(u_pair[..., C_CONV:])
    u_ext = jnp.concatenate([conv_past, u], axis=1)
    c = causal_depthwise_conv(u_ext, conv_w, conv_b)
    branch_c = jax.nn.silu(layernorm(c, conv_ln_g, conv_ln_b)) @ w_conv_out
    merged = jax.nn.sigmoid(gate_a) * branch_a + jax.nn.sigmoid(gate_c) * branch_c
    return merged @ w_out, ckv_new, kr_new, u_ext[:, -(CONV_W - 1):]


def swiglu(x, w_gate, w_up, w_down):
    return (jax.nn.silu(x @ w_gate) * (x @ w_up)) @ w_down


def moe_swiglu(x, w_router, w_gate, w_up, w_down):
    n_tok, d = x.shape
    logits = jnp.einsum('nd,de->ne', x.astype(jnp.float32), w_router.astype(jnp.float32))
    top_val, top_idx = lax.top_k(logits, TOP_K)
    gates = jax.nn.softmax(top_val, axis=-1).astype(x.dtype)
    n_assign = n_tok * TOP_K
    flat_e = top_idx.reshape(-1)
    flat_tok = jnp.repeat(jnp.arange(n_tok, dtype=jnp.int32), TOP_K)
    flat_g = gates.reshape(-1)
    order = jnp.argsort(flat_e)
    se, st, sg = flat_e[order], flat_tok[order], flat_g[order]
    counts = jnp.bincount(flat_e, length=N_EXPERTS)
    start = jnp.cumsum(counts) - counts
    padded = (counts + MOE_BLOCK - 1) // MOE_BLOCK * MOE_BLOCK
    pad_end = jnp.cumsum(padded)
    pad_start = pad_end - padded
    dest = pad_start[se] + jnp.arange(n_assign, dtype=jnp.int32) - start[se]
    n_blocks = -(-n_assign // MOE_BLOCK) + N_EXPERTS
    buf_tok = jnp.full((n_blocks * MOE_BLOCK,), n_tok, jnp.int32).at[dest].set(st)
    buf_g = jnp.zeros((n_blocks * MOE_BLOCK,), x.dtype).at[dest].set(sg)
    blk_e = jnp.minimum(jnp.searchsorted(pad_end, jnp.arange(n_blocks) * MOE_BLOCK, side='right'),
                        N_EXPERTS - 1)
    x_pad = jnp.concatenate([x, jnp.zeros((1, d), x.dtype)], axis=0)

    def expert_block(args):
        tok, e = args
        xb = x_pad[tok]
        return swiglu(xb, w_gate[e], w_up[e], w_down[e])

    out = lax.map(expert_block, (buf_tok.reshape(n_blocks, MOE_BLOCK), blk_e))
    y = jax.ops.segment_sum(out.reshape(-1, d) * buf_g[:, None], buf_tok, num_segments=n_tok + 1)
    return y[:n_tok]


def setup_inputs(seed: int = 0) -> dict:
    key = jax.random.key(seed)
    ks = jax.random.split(key, 40)
    n = lambda i, shape, s: jax.random.normal(ks[i], shape, jnp.float32) * s
    return {
        'x_prompt': n(0, (BATCH, SEQ, D_MODEL), 1.0),
        'x_sample': n(1, (DEC_BATCH, DEC_SEQ, D_MODEL), 1.0),
        'cache_ckv': n(2, (DEPTH, DEC_BATCH, PAST_LEN, KV_RANK), 1.0),
        'cache_kr': n(3, (DEPTH, DEC_BATCH, PAST_LEN, QK_ROPE), 1.0),
        'state_conv': n(4, (DEPTH, DEC_BATCH, CONV_W - 1, C_CONV), 0.5),
        'ln_in_g': 1.0 + n(5, (D_MODEL,), 0.02),
        'ln_in_b': n(6, (D_MODEL,), 0.02),
        'w_in': n(7, (DEPTH, D_MODEL, P_IN), D_MODEL ** -0.5),
        'b_in': n(8, (DEPTH, P_IN), 0.02),
        'g_qnorm': 1.0 + n(9, (DEPTH, Q_RANK), 0.02),
        'w_uq': n(10, (DEPTH, Q_RANK, Q_OUT), Q_RANK ** -0.5),
        'g_kvnorm': 1.0 + n(11, (DEPTH, KV_RANK), 0.02),
        'w_uk': n(12, (DEPTH, KV_RANK, N_HEADS, QK_NOPE), KV_RANK ** -0.5),
        'w_uv': n(13, (DEPTH, KV_RANK, N_HEADS, V_HEAD), KV_RANK ** -0.5),
        'w_o_attn': n(14, (DEPTH, N_HEADS * V_HEAD, D_MODEL), BETA * (N_HEADS * V_HEAD) ** -0.5),
        'conv_w': n(15, (DEPTH, CONV_W, C_CONV), CONV_W ** -0.5),
        'conv_b': n(16, (DEPTH, C_CONV), 0.02),
        'conv_ln_g': 1.0 + n(17, (DEPTH, C_CONV), 0.02),
        'conv_ln_b': n(18, (DEPTH, C_CONV), 0.02),
        'w_conv_out': n(19, (DEPTH, C_CONV, D_MODEL), BETA * C_CONV ** -0.5),
        'w_out': n(20, (DEPTH, D_MODEL, D_MODEL), BETA * D_MODEL ** -0.5),
        'ln1_g': 1.0 + n(21, (DEPTH, D_MODEL), 0.02),
        'ln1_b': n(22, (DEPTH, D_MODEL), 0.02),
        'ln2_g': 1.0 + n(23, (DEPTH, D_MODEL), 0.02),
        'ln2_b': n(24, (DEPTH, D_MODEL), 0.02),
        'w_ffn_gate': n(25, (N_DENSE, D_MODEL, D_FF), D_MODEL ** -0.5),
        'w_ffn_up': n(26, (N_DENSE, D_MODEL, D_FF), D_MODEL ** -0.5),
        'w_ffn_down': n(27, (N_DENSE, D_FF, D_MODEL), BETA * D_FF ** -0.5),
        'w_router': n(28, (N_MOE, D_MODEL, N_EXPERTS), D_MODEL ** -0.5),
        'w_exp_gate': n(29, (N_MOE, N_EXPERTS, D_MODEL, D_FF_EXPERT), D_MODEL ** -0.5),
        'w_exp_up': n(30, (N_MOE, N_EXPERTS, D_MODEL, D_FF_EXPERT), D_MODEL ** -0.5),
        'w_exp_down': n(31, (N_MOE, N_EXPERTS, D_FF_EXPERT, D_MODEL), BETA * D_FF_EXPERT ** -0.5),
    }


def reference(x_prompt, x_sample, cache_ckv, cache_kr, state_conv, ln_in_g, ln_in_b, w_in, b_in,
              g_qnorm, w_uq, g_kvnorm, w_uk, w_uv, w_o_attn, conv_w, conv_b, conv_ln_g, conv_ln_b,
              w_conv_out, w_out, ln1_g, ln1_b, ln2_g, ln2_b, w_ffn_gate, w_ffn_up, w_ffn_down,
              w_router, w_exp_gate, w_exp_up, w_exp_down):
    pos_p = jnp.arange(x_prompt.shape[1], dtype=jnp.int32)
    pos_s = cache_ckv.shape[2] + jnp.arange(x_sample.shape[1], dtype=jnp.int32)
    conv_zero = jnp.zeros((x_prompt.shape[0], CONV_W - 1, C_CONV), x_prompt.dtype)
    xp = layernorm(x_prompt, ln_in_g, ln_in_b)
    xs = layernorm(x_sample, ln_in_g, ln_in_b)
    ckv_p_l, kr_p_l, conv_p_l, ckv_s_l, kr_s_l, conv_s_l = [], [], [], [], [], []
    for l in range(DEPTH):
        mix_w = (w_in[l], b_in[l], g_qnorm[l], w_uq[l], g_kvnorm[l], w_uk[l], w_uv[l], w_o_attn[l],
                 conv_w[l], conv_b[l], conv_ln_g[l], conv_ln_b[l], w_conv_out[l], w_out[l])
        mp, ckv_p, kr_p, cs_p = token_mixer(xp, pos_p, None, None, conv_zero, *mix_w)
        ms, ckv_s, kr_s, cs_s = token_mixer(xs, pos_s, cache_ckv[l], cache_kr[l], state_conv[l], *mix_w)
        xp = layernorm(ALPHA * xp + mp, ln1_g[l], ln1_b[l])
        xs = layernorm(ALPHA * xs + ms, ln1_g[l], ln1_b[l])
        i = l // 2
        if l % 2 == 0:
            fp = swiglu(xp, w_ffn_gate[i], w_ffn_up[i], w_ffn_down[i])
            fs = swiglu(xs, w_ffn_gate[i], w_ffn_up[i], w_ffn_down[i])
        else:
            fp = moe_swiglu(xp.reshape(-1, D_MODEL), w_router[i], w_exp_gate[i], w_exp_up[i],
                            w_exp_down[i]).reshape(xp.shape)
            fs = moe_swiglu(xs.reshape(-1, D_MODEL), w_router[i], w_exp_gate[i], w_exp_up[i],
                            w_exp_down[i]).reshape(xs.shape)
        xp = layernorm(ALPHA * xp + fp, ln2_g[l], ln2_b[l])
        xs = layernorm(ALPHA * xs + fs, ln2_g[l], ln2_b[l])
        ckv_p_l.append(ckv_p); kr_p_l.append(kr_p); conv_p_l.append(cs_p)
        ckv_s_l.append(ckv_s); kr_s_l.append(kr_s); conv_s_l.append(cs_s)
    return (xp, xs, jnp.stack(ckv_p_l), jnp.stack(kr_p_l), jnp.stack(conv_p_l),
            jnp.stack(ckv_s_l), jnp.stack(kr_s_l), jnp.stack(conv_s_l))
```

```python
import functools
import math

import numpy as np
import jax
import jax.numpy as jnp
from jax import lax
from jax.experimental import pallas as pl
from jax.experimental.pallas import tpu as pltpu

CHUNK = 64
N_HEADS = 8
QK_NOPE = 64
QK_ROPE = 32
V_HEAD = 64
ROPE_THETA = 10000.0
TOP_K = 2
LN_EPS = 1e-5
NEG_INF = -1e30
ATTN_SCALE = (QK_NOPE + QK_ROPE) ** -0.5

LANES = 128
HEAD_PAD = 128
HEADS_PER_STEP = 2
MOE_ROWS = 256
VMEM_LIMIT = 56 * 1024 * 1024

BF16 = jnp.bfloat16
F32 = jnp.float32


def _cparams(sem):
    return pltpu.CompilerParams(dimension_semantics=sem, vmem_limit_bytes=VMEM_LIMIT)


def _pick_tile(n, candidates):
    for c in candidates:
        if n % c == 0:
            return c
    raise ValueError(f"no tile in {candidates} divides {n}")


def _dot(a, b):
    return jnp.dot(a, b, preferred_element_type=F32)


def _dot_nt(a, b):
    return lax.dot_general(a, b, (((1,), (1,)), ((), ())), preferred_element_type=F32)


def _layernorm(x, g, b):
    mu = jnp.mean(x, axis=-1, keepdims=True)
    xc = x - mu
    var = jnp.mean(xc * xc, axis=-1, keepdims=True)
    return xc * lax.rsqrt(var + LN_EPS) * g + b


def _rmsnorm(x, g):
    return x * lax.rsqrt(jnp.mean(x * x, axis=-1, keepdims=True) + LN_EPS) * g


def _silu(x):
    return x * jax.nn.sigmoid(x)


def _ln_kernel(x_ref, g_ref, b_ref, o_ref):
    o_ref[...] = _layernorm(x_ref[...], g_ref[...], b_ref[...])


def _ln_call(x, g, b):
    n, d = x.shape
    tm = _pick_tile(n, (1024, 512, 256, 128, 64, 32))
    return pl.pallas_call(
        _ln_kernel,
        grid=(n // tm,),
        in_specs=[pl.BlockSpec((tm, d), lambda i: (i, 0)),
                  pl.BlockSpec((1, d), lambda i: (0, 0)),
                  pl.BlockSpec((1, d), lambda i: (0, 0))],
        out_specs=pl.BlockSpec((tm, d), lambda i: (i, 0)),
        out_shape=jax.ShapeDtypeStruct((n, d), F32),
        compiler_params=_cparams(("parallel",)),
        name="ln_in",
    )(x, g.reshape(1, d), b.reshape(1, d))


def _inproj_kernel(q_rank, kv_rank, c_conv,
                   x_ref, wa_ref, ba_ref, wb_ref, bb_ref, gq_ref, wqa_ref, wqb_ref, gkv_ref,
                   wuk_ref, wuv_ref, ctq_ref, stq_ref, ck_ref, sk_ref,
                   q_ref, k_ref, v_ref, ckv_ref, kr_ref, u_ref, gate_ref):
    xb = x_ref[...].astype(BF16)
    pa = _dot(xb, wa_ref[...]) + ba_ref[...]
    o1, o2, o3 = q_rank, q_rank + kv_rank, q_rank + kv_rank + HEAD_PAD
    cqn = _rmsnorm(pa[:, :o1], gq_ref[...]).astype(BF16)
    ct = jnp.concatenate([ctq_ref[...]] * N_HEADS, axis=1)
    st = jnp.concatenate([stq_ref[...]] * N_HEADS, axis=1)
    q = _dot(cqn, wqa_ref[...]) * ct + _dot(cqn, wqb_ref[...]) * st
    q_ref[...] = q.astype(BF16)
    ckv = _rmsnorm(pa[:, o1:o2], gkv_ref[...])
    ckv_ref[...] = ckv
    kr = pa[:, o2:o3] * ck_ref[...] + pa[:, o3:] * sk_ref[...]
    kr_ref[...] = kr
    ckvb = ckv.astype(BF16)
    k = _dot(ckvb, wuk_ref[...]) + jnp.concatenate([kr] * N_HEADS, axis=1)
    k_ref[...] = k.astype(BF16)
    v_ref[...] = _dot(ckvb, wuv_ref[...]).astype(BF16)
    pu = _dot(xb, wb_ref[:, :2 * c_conv]) + bb_ref[:, :2 * c_conv]
    u_ref[...] = pu[:, :c_conv] * jax.nn.sigmoid(pu[:, c_conv:])
    pg = _dot(xb, wb_ref[:, 2 * c_conv:]) + bb_ref[:, 2 * c_conv:]
    gate_ref[...] = jax.nn.sigmoid(pg).astype(BF16)


def _inproj_call(x, wl, tabs, dims):
    n, d = x.shape
    q_rank, kv_rank, c_conv = dims
    tm = _pick_tile(n, (256, 128, 64, 32))
    na = wl["wa"].shape[1]
    nb = wl["wb"].shape[1]
    hq = N_HEADS * HEAD_PAD
    hv = N_HEADS * V_HEAD
    const = lambda i: (0, 0)
    row = lambda i: (i, 0)
    in_specs = [
        pl.BlockSpec((tm, d), row),
        pl.BlockSpec((d, na), const), pl.BlockSpec((1, na), const),
        pl.BlockSpec((d, nb), const), pl.BlockSpec((1, nb), const),
        pl.BlockSpec((1, q_rank), const),
        pl.BlockSpec((q_rank, hq), const), pl.BlockSpec((q_rank, hq), const),
        pl.BlockSpec((1, kv_rank), const),
        pl.BlockSpec((kv_rank, hq), const), pl.BlockSpec((kv_rank, hv), const),
        pl.BlockSpec((tm, HEAD_PAD), row), pl.BlockSpec((tm, HEAD_PAD), row),
        pl.BlockSpec((tm, HEAD_PAD), row), pl.BlockSpec((tm, HEAD_PAD), row),
    ]
    out_specs = [
        pl.BlockSpec((tm, hq), row), pl.BlockSpec((tm, hq), row), pl.BlockSpec((tm, hv), row),
        pl.BlockSpec((tm, kv_rank), row), pl.BlockSpec((tm, HEAD_PAD), row),
        pl.BlockSpec((tm, c_conv), row), pl.BlockSpec((tm, 2 * d), row),
    ]
    out_shape = [
        jax.ShapeDtypeStruct((n, hq), BF16), jax.ShapeDtypeStruct((n, hq), BF16),
        jax.ShapeDtypeStruct((n, hv), BF16),
        jax.ShapeDtypeStruct((n, kv_rank), F32), jax.ShapeDtypeStruct((n, HEAD_PAD), F32),
        jax.ShapeDtypeStruct((n, c_conv), F32), jax.ShapeDtypeStruct((n, 2 * d), BF16),
    ]
    return pl.pallas_call(
        functools.partial(_inproj_kernel, q_rank, kv_rank, c_conv),
        grid=(n // tm,), in_specs=in_specs, out_specs=out_specs, out_shape=out_shape,
        compiler_params=_cparams(("parallel",)), name="inproj",
    )(x, wl["wa"], wl["ba"], wl["wb"], wl["bb"], wl["gq"], wl["wqa"], wl["wqb"], wl["gkv"],
      wl["wuk"], wl["wuv"], tabs["ctq"], tabs["stq"], tabs["ck"], tabs["sk"])


def _kvpast_kernel(ckv_ref, kr_ref, wuk_ref, wuv_ref, place_ref, k_ref, v_ref):
    ckvb = ckv_ref[...].astype(BF16)
    k = _dot(ckvb, wuk_ref[...]) + _dot(kr_ref[...].astype(BF16), place_ref[...])
    k_ref[...] = k.astype(BF16)
    v_ref[...] = _dot(ckvb, wuv_ref[...]).astype(BF16)


def _kvpast_call(ckv, kr, wuk, wuv, place):
    depth, rows, kv_rank = ckv.shape
    rope = kr.shape[-1]
    tm = _pick_tile(rows, (1024, 512, 256, 128, 64, 32))
    tpl = rows // tm
    hq = N_HEADS * HEAD_PAD
    hv = N_HEADS * V_HEAD
    return pl.pallas_call(
        _kvpast_kernel,
        grid=(depth, tpl),
        in_specs=[pl.BlockSpec((None, tm, kv_rank), lambda l, i: (l, i, 0)),
                  pl.BlockSpec((None, tm, rope), lambda l, i: (l, i, 0)),
                  pl.BlockSpec((None, kv_rank, hq), lambda l, i: (l, 0, 0)),
                  pl.BlockSpec((None, kv_rank, hv), lambda l, i: (l, 0, 0)),
                  pl.BlockSpec((rope, hq), lambda l, i: (0, 0))],
        out_specs=[pl.BlockSpec((None, tm, hq), lambda l, i: (l, i, 0)),
                   pl.BlockSpec((None, tm, hv), lambda l, i: (l, i, 0))],
        out_shape=[jax.ShapeDtypeStruct((depth, rows, hq), BF16),
                   jax.ShapeDtypeStruct((depth, rows, hv), BF16)],
        compiler_params=_cparams(("parallel", "parallel")), name="kvpast",
    )(ckv, kr, wuk, wuv, place)


def _attn_prompt_kernel(tq, q_ref, k_ref, v_ref, o_ref, m_ref, l_ref, acc_ref):
    qi = pl.program_id(2)
    m_ref[...] = jnp.full(m_ref.shape, NEG_INF, F32)
    l_ref[...] = jnp.zeros(l_ref.shape, F32)
    acc_ref[...] = jnp.zeros(acc_ref.shape, F32)

    def tile(j, masked):
        start = pl.multiple_of(j * tq, tq)
        kt = k_ref[pl.ds(start, tq), :]
        vt = v_ref[pl.ds(start, tq), :]
        for h in range(HEADS_PER_STEP):
            qh = q_ref[:, h * HEAD_PAD:(h + 1) * HEAD_PAD]
            s = _dot_nt(qh, kt[:, h * HEAD_PAD:(h + 1) * HEAD_PAD])
            if masked:
                qc = lax.broadcasted_iota(jnp.int32, s.shape, 0) // CHUNK
                kc = lax.broadcasted_iota(jnp.int32, s.shape, 1) // CHUNK
                s = jnp.where(kc <= qc, s, NEG_INF)
            m_old = m_ref[h]
            m_new = jnp.maximum(m_old, jnp.max(s, axis=1, keepdims=True))
            p = jnp.exp(s - m_new)
            alpha = jnp.exp(m_old - m_new)
            l_ref[h] = alpha * l_ref[h] + jnp.sum(p, axis=1, keepdims=True)
            acc_ref[h] = alpha * acc_ref[h] + _dot(p.astype(BF16), vt)
            m_ref[h] = m_new

    def body(j, carry):
        tile(j, False)
        return carry

    lax.fori_loop(0, qi, body, 0)
    tile(qi, True)
    lane = lax.broadcasted_iota(jnp.int32, (tq, HEADS_PER_STEP * V_HEAD), 1)
    o0 = acc_ref[0] / l_ref[0]
    o1 = acc_ref[1] / l_ref[1]
    o_ref[...] = jnp.where(lane < V_HEAD, o0, o1).astype(BF16)


def _attn_prompt_call(q, k, v, batch, seq):
    tq = _pick_tile(seq, (512, 256, 128, 64))
    nq = seq // tq
    hp = N_HEADS // HEADS_PER_STEP
    qw = HEADS_PER_STEP * HEAD_PAD
    vw = HEADS_PER_STEP * V_HEAD
    return pl.pallas_call(
        functools.partial(_attn_prompt_kernel, tq),
        grid=(batch, hp, nq),
        in_specs=[pl.BlockSpec((tq, qw), lambda b, h, i: (b * nq + i, h)),
                  pl.BlockSpec((seq, qw), lambda b, h, i: (b, h)),
                  pl.BlockSpec((seq, vw), lambda b, h, i: (b, h))],
        out_specs=pl.BlockSpec((tq, vw), lambda b, h, i: (b * nq + i, h)),
        out_shape=jax.ShapeDtypeStruct((batch * seq, N_HEADS * V_HEAD), BF16),
        scratch_shapes=[pltpu.VMEM((HEADS_PER_STEP, tq, 1), F32),
                        pltpu.VMEM((HEADS_PER_STEP, tq, 1), F32),
                        pltpu.VMEM((HEADS_PER_STEP, tq, vw), F32)],
        compiler_params=_cparams(("parallel", "parallel", "arbitrary")), name="attn_prompt",
    )(q, k, v)


def _attn_sample_kernel(mask_past, mask_new, past_len,
                        q_ref, kn_ref, vn_ref, kp_ref, vp_ref, o_ref):
    vp = vp_ref[...]
    vn = vn_ref[...]
    outs = []
    for h in range(HEADS_PER_STEP):
        sl = slice(h * HEAD_PAD, (h + 1) * HEAD_PAD)
        qh = q_ref[:, sl]
        sp = _dot_nt(qh, kp_ref[:, sl])
        sn = _dot_nt(qh, kn_ref[:, sl])
        if mask_past:
            qc = (lax.broadcasted_iota(jnp.int32, sp.shape, 0) + past_len) // CHUNK
            kc = lax.broadcasted_iota(jnp.int32, sp.shape, 1) // CHUNK
            sp = jnp.where(kc <= qc, sp, NEG_INF)
        if mask_new:
            qc = (lax.broadcasted_iota(jnp.int32, sn.shape, 0) + past_len) // CHUNK
            kc = (lax.broadcasted_iota(jnp.int32, sn.shape, 1) + past_len) // CHUNK
            sn = jnp.where(kc <= qc, sn, NEG_INF)
        m = jnp.maximum(jnp.max(sp, axis=1, keepdims=True), jnp.max(sn, axis=1, keepdims=True))
        pp = jnp.exp(sp - m)
        pn = jnp.exp(sn - m)
        l = jnp.sum(pp, axis=1, keepdims=True) + jnp.sum(pn, axis=1, keepdims=True)
        outs.append((_dot(pp.astype(BF16), vp) + _dot(pn.astype(BF16), vn)) / l)
    lane = lax.broadcasted_iota(jnp.int32, outs[0].shape, 1)
    o_ref[...] = jnp.where(lane < V_HEAD, outs[0], outs[1]).astype(BF16)


def _attn_sample_call(q, k, v, kpast, vpast, layer, row0, n_streams, t_new, past_len):
    hp = N_HEADS // HEADS_PER_STEP
    qw = HEADS_PER_STEP * HEAD_PAD
    vw = HEADS_PER_STEP * V_HEAD
    blk0 = row0 // t_new
    q_pos = past_len + np.arange(t_new)
    mask_past = not bool(np.all((np.arange(past_len) // CHUNK)[None, :] <= (q_pos // CHUNK)[:, None]))
    mask_new = not bool(np.all((q_pos // CHUNK)[None, :] <= (q_pos // CHUNK)[:, None]))
    new_spec = lambda w: pl.BlockSpec((t_new, w), lambda s, h: (blk0 + s, h))
    return pl.pallas_call(
        functools.partial(_attn_sample_kernel, mask_past, mask_new, past_len),
        grid=(n_streams, hp),
        in_specs=[new_spec(qw), new_spec(qw), new_spec(vw),
                  pl.BlockSpec((None, past_len, qw), lambda s, h: (layer * n_streams + s, 0, h)),
                  pl.BlockSpec((None, past_len, vw), lambda s, h: (layer * n_streams + s, 0, h))],
        out_specs=pl.BlockSpec((t_new, vw), lambda s, h: (s, h)),
        out_shape=jax.ShapeDtypeStruct((n_streams * t_new, N_HEADS * V_HEAD), BF16),
        compiler_params=_cparams(("parallel", "parallel")), name="attn_sample",
    )(q, k, v, kpast, vpast)


def _conv_kernel(tc, halo, conv_w, rc, u_ref, h_ref, w_ref, b_ref, g_ref, beta_ref, o_ref, ext_ref):
    ext_ref[0:halo, :] = h_ref[...]
    ext_ref[halo:halo + tc, :] = u_ref[...]
    lead = halo - (conv_w - 1)
    for c in range(tc // rc):
        acc = jnp.broadcast_to(b_ref[...], (rc, u_ref.shape[1]))
        for j in range(conv_w):
            r0 = c * rc + lead + j
            acc = acc + w_ref[j:j + 1, :] * ext_ref[r0:r0 + rc, :]
        y = _layernorm(acc, g_ref[...], beta_ref[...])
        o_ref[c * rc:(c + 1) * rc, :] = _silu(y).astype(BF16)


def _conv_call(u, halo_rows, tc, w, b, g, beta):
    n, c = u.shape
    s = n // tc
    halo = halo_rows.shape[0] // s
    conv_w = w.shape[0]
    rc = min(tc, 64)
    const = lambda i: (0, 0)
    return pl.pallas_call(
        functools.partial(_conv_kernel, tc, halo, conv_w, rc),
        grid=(s,),
        in_specs=[pl.BlockSpec((tc, c), lambda i: (i, 0)),
                  pl.BlockSpec((halo, c), lambda i: (i, 0)),
                  pl.BlockSpec((conv_w, c), const), pl.BlockSpec((1, c), const),
                  pl.BlockSpec((1, c), const), pl.BlockSpec((1, c), const)],
        out_specs=pl.BlockSpec((tc, c), lambda i: (i, 0)),
        out_shape=jax.ShapeDtypeStruct((n, c), BF16),
        scratch_shapes=[pltpu.VMEM((halo + tc, c), F32)],
        compiler_params=_cparams(("parallel",)), name="conv",
    )(u, halo_rows, w, b.reshape(1, c), g.reshape(1, c), beta.reshape(1, c))


def _outproj_kernel(alpha, d, x_ref, a_ref, c_ref, gate_ref, woa_ref, wco_ref, wout_ref,
                    g_ref, b_ref, o_ref):
    br_a = _dot(a_ref[...], woa_ref[...])
    br_c = _dot(c_ref[...], wco_ref[...])
    merged = gate_ref[:, :d].astype(F32) * br_a + gate_ref[:, d:].astype(F32) * br_c
    mix = _dot(merged.astype(BF16), wout_ref[...])
    o_ref[...] = _layernorm(alpha * x_ref[...] + mix, g_ref[...], b_ref[...])


def _outproj_call(x, attn, cact, gates, wl, alpha):
    n, d = x.shape
    tm = _pick_tile(n, (512, 256, 128, 64, 32))
    const = lambda i: (0, 0)
    row = lambda i: (i, 0)
    ha, hc = attn.shape[1], cact.shape[1]
    return pl.pallas_call(
        functools.partial(_outproj_kernel, alpha, d),
        grid=(n // tm,),
        in_specs=[pl.BlockSpec((tm, d), row), pl.BlockSpec((tm, ha), row),
                  pl.BlockSpec((tm, hc), row), pl.BlockSpec((tm, 2 * d), row),
                  pl.BlockSpec((ha, d), const), pl.BlockSpec((hc, d), const),
                  pl.BlockSpec((d, d), const), pl.BlockSpec((1, d), const),
                  pl.BlockSpec((1, d), const)],
        out_specs=pl.BlockSpec((tm, d), row),
        out_shape=jax.ShapeDtypeStruct((n, d), F32),
        compiler_params=_cparams(("parallel",)), name="outproj",
    )(x, attn, cact, gates, wl["woa"], wl["wco"], wl["wout"], wl["ln1g"], wl["ln1b"])


def _ffn_kernel(alpha, x_ref, wg_ref, wu_ref, wd_ref, g_ref, b_ref, o_ref, acc_ref):
    f = pl.program_id(1)

    @pl.when(f == 0)
    def _():
        acc_ref[...] = jnp.zeros(acc_ref.shape, F32)

    xb = x_ref[...].astype(BF16)
    h = _silu(_dot(xb, wg_ref[...])) * _dot(xb, wu_ref[...])
    acc_ref[...] += _dot(h.astype(BF16), wd_ref[...])

    @pl.when(f == pl.num_programs(1) - 1)
    def _():
        o_ref[...] = _layernorm(alpha * x_ref[...] + acc_ref[...], g_ref[...], b_ref[...])


def _ffn_call(x, wg, wu, wd, g, b, alpha):
    n, d = x.shape
    dff = wg.shape[1]
    tm = _pick_tile(n, (512, 256, 128, 64, 32))
    tf = _pick_tile(dff, (1408, 1024, 512, 256, 128))
    return pl.pallas_call(
        functools.partial(_ffn_kernel, alpha),
        grid=(n // tm, dff // tf),
        in_specs=[pl.BlockSpec((tm, d), lambda i, f: (i, 0)),
                  pl.BlockSpec((d, tf), lambda i, f: (0, f)),
                  pl.BlockSpec((d, tf), lambda i, f: (0, f)),
                  pl.BlockSpec((tf, d), lambda i, f: (f, 0)),
                  pl.BlockSpec((1, d), lambda i, f: (0, 0)),
                  pl.BlockSpec((1, d), lambda i, f: (0, 0))],
        out_specs=pl.BlockSpec((tm, d), lambda i, f: (i, 0)),
        out_shape=jax.ShapeDtypeStruct((n, d), F32),
        scratch_shapes=[pltpu.VMEM((tm, d), F32)],
        compiler_params=_cparams(("parallel", "arbitrary")), name="ffn",
    )(x, wg, wu, wd, g, b)


def _route_kernel(n_tok, n_exp, tm, x_ref, wr_ref, xb_ref, rank_ref, gate_ref, rankt_ref, cnt_ref):
    i = pl.program_id(0)
    x = x_ref[...]
    row = lax.broadcasted_iota(jnp.int32, (tm, 1), 0) + i * tm
    valid = row < n_tok
    xb_ref[...] = jnp.where(valid, x, 0.0).astype(BF16)
    logits = jnp.dot(x, wr_ref[...], preferred_element_type=F32, precision=lax.Precision.HIGHEST)
    lane = lax.broadcasted_iota(jnp.int32, (tm, LANES), 1)
    logits = jnp.where(lane < n_exp, logits, -jnp.inf)
    m1 = jnp.max(logits, axis=1, keepdims=True)
    i1 = jnp.min(jnp.where(logits == m1, lane, LANES), axis=1, keepdims=True)
    rest = jnp.where(lane == i1, -jnp.inf, logits)
    m2 = jnp.max(rest, axis=1, keepdims=True)
    i2 = jnp.min(jnp.where(rest == m2, lane, LANES), axis=1, keepdims=True)
    e2 = jnp.exp(m2 - m1)
    g1 = 1.0 / (1.0 + e2)
    g2 = e2 / (1.0 + e2)
    sel1 = (lane == i1) & valid
    sel2 = (lane == i2) & valid
    gate_ref[...] = jnp.where(sel1, g1, 0.0) + jnp.where(sel2, g2, 0.0)
    sel = jnp.where(sel1 | sel2, 1.0, 0.0)
    cs = min(tm, 256)
    r = lax.broadcasted_iota(jnp.int32, (cs, cs), 0)
    c = lax.broadcasted_iota(jnp.int32, (cs, cs), 1)
    lower = jnp.where(c < r, 1.0, 0.0).astype(BF16)
    offs = jnp.zeros((1, LANES), F32)
    for ch in range(tm // cs):
        sc = sel[ch * cs:(ch + 1) * cs, :]
        rk = _dot(lower, sc.astype(BF16)) + offs
        rank_ref[ch * cs:(ch + 1) * cs, :] = jnp.where(sc > 0.0, rk, -1.0)
        offs = offs + jnp.sum(sc, axis=0, keepdims=True)
    cnt_ref[...] = offs.astype(jnp.int32)
    rankt_ref[...] = jnp.transpose(rank_ref[...])[:rankt_ref.shape[0], :]


def _route_call(x, wr_pad, n_exp, tm):
    n, d = x.shape
    nt = -(-n // tm)
    return pl.pallas_call(
        functools.partial(_route_kernel, n, n_exp, tm),
        grid=(nt,),
        in_specs=[pl.BlockSpec((tm, d), lambda i: (i, 0)),
                  pl.BlockSpec((d, LANES), lambda i: (0, 0))],
        out_specs=[pl.BlockSpec((tm, d), lambda i: (i, 0)),
                   pl.BlockSpec((tm, LANES), lambda i: (i, 0)),
                   pl.BlockSpec((tm, LANES), lambda i: (i, 0)),
                   pl.BlockSpec((n_exp, tm), lambda i: (0, i)),
                   pl.BlockSpec((None, 1, LANES), lambda i: (i, 0, 0))],
        out_shape=[jax.ShapeDtypeStruct((nt * tm, d), BF16),
                   jax.ShapeDtypeStruct((nt * tm, LANES), F32),
                   jax.ShapeDtypeStruct((nt * tm, LANES), F32),
                   jax.ShapeDtypeStruct((n_exp, nt * tm), F32),
                   jax.ShapeDtypeStruct((nt, 1, LANES), jnp.int32)],
        compiler_params=_cparams(("parallel",)), name="route",
    )(x, wr_pad)


def _moe_kernel(tm, nblk_ref, xb_ref, rank_ref, gate_ref, rankt_ref, wg_ref, wu_ref, wd_ref,
                y_ref, xe_ref, oe_ref, rcol_ref, gcol_ref):
    i = pl.program_id(0)
    e = pl.program_id(1)
    f = pl.program_id(2)
    n_e = pl.num_programs(1)
    nb = nblk_ref[i * n_e + e]
    rows = MOE_ROWS

    @pl.when((e == 0) & (f == 0))
    def _():
        y_ref[...] = jnp.zeros(y_ref.shape, F32)

    @pl.when(f == 0)
    def _():
        lane = lax.broadcasted_iota(jnp.int32, (tm, LANES), 1)
        rcol_ref[...] = jnp.max(jnp.where(lane == e, rank_ref[...], -1.0), axis=1, keepdims=True)
        gcol_ref[...] = jnp.sum(jnp.where(lane == e, gate_ref[...], 0.0), axis=1, keepdims=True)
        rrow = rankt_ref[pl.ds(e, 1), :]

        def gather(j, carry):
            slot = (lax.broadcasted_iota(jnp.int32, (rows, tm), 0) + j * rows).astype(F32)
            onehot = jnp.where(rrow == slot, 1.0, 0.0).astype(BF16)
            r0 = pl.multiple_of(j * rows, rows)
            xe_ref[pl.ds(r0, rows), :] = _dot(onehot, xb_ref[...]).astype(BF16)
            return carry

        lax.fori_loop(0, nb, gather, 0)

    def expert(j, carry):
        r0 = pl.multiple_of(j * rows, rows)
        xe = xe_ref[pl.ds(r0, rows), :]
        h = _silu(_dot(xe, wg_ref[...])) * _dot(xe, wu_ref[...])
        o = _dot(h.astype(BF16), wd_ref[...])

        @pl.when(f == 0)
        def _():
            oe_ref[pl.ds(r0, rows), :] = o

        @pl.when(f != 0)
        def _():
            oe_ref[pl.ds(r0, rows), :] += o

        return carry

    lax.fori_loop(0, nb, expert, 0)

    @pl.when(f == pl.num_programs(2) - 1)
    def _():
        def scatter(j, carry):
            r0 = pl.multiple_of(j * rows, rows)
            ob = oe_ref[pl.ds(r0, rows), :].astype(BF16)
            sb = min(tm, 512)
            for t in range(tm // sb):
                ts = slice(t * sb, (t + 1) * sb)
                slot = (lax.broadcasted_iota(jnp.int32, (sb, rows), 1) + j * rows).astype(F32)
                onehot_t = jnp.where(rcol_ref[ts, :] == slot, 1.0, 0.0).astype(BF16)
                y_ref[ts, :] += gcol_ref[ts, :] * _dot(onehot_t, ob)
            return carry

        lax.fori_loop(0, nb, scatter, 0)


def _moe_call(n_tok, xb, rank, gate, rankt, nblk, wg, wu, wd, tm):
    npad, d = xb.shape
    nt = npad // tm
    n_exp, _, dff = wg.shape
    tf = _pick_tile(dff, (512, 256, 128))
    grid_spec = pltpu.PrefetchScalarGridSpec(
        num_scalar_prefetch=1,
        grid=(nt, n_exp, dff // tf),
        in_specs=[pl.BlockSpec((tm, d), lambda i, e, f, nb: (i, 0)),
                  pl.BlockSpec((tm, LANES), lambda i, e, f, nb: (i, 0)),
                  pl.BlockSpec((tm, LANES), lambda i, e, f, nb: (i, 0)),
                  pl.BlockSpec((n_exp, tm), lambda i, e, f, nb: (0, i)),
                  pl.BlockSpec((None, d, tf), lambda i, e, f, nb: (e, 0, f)),
                  pl.BlockSpec((None, d, tf), lambda i, e, f, nb: (e, 0, f)),
                  pl.BlockSpec((None, tf, d), lambda i, e, f, nb: (e, f, 0))],
        out_specs=pl.BlockSpec((tm, d), lambda i, e, f, nb: (i, 0)),
        scratch_shapes=[pltpu.VMEM((tm, d), BF16), pltpu.VMEM((tm, d), F32),
                        pltpu.VMEM((tm, 1), F32), pltpu.VMEM((tm, 1), F32)],
    )
    return pl.pallas_call(
        functools.partial(_moe_kernel, tm),
        grid_spec=grid_spec,
        out_shape=jax.ShapeDtypeStruct((n_tok, d), F32),
        compiler_params=_cparams(("parallel", "arbitrary", "arbitrary")), name="moe",
    )(nblk, xb, rank, gate, rankt, wg, wu, wd)


def _resid_ln_kernel(alpha, x_ref, y_ref, g_ref, b_ref, o_ref):
    o_ref[...] = _layernorm(alpha * x_ref[...] + y_ref[...], g_ref[...], b_ref[...])


def _resid_ln_call(x, y, g, b, alpha):
    n, d = x.shape
    tm = _pick_tile(n, (1024, 512, 256, 128, 64, 32))
    row = lambda i: (i, 0)
    const = lambda i: (0, 0)
    return pl.pallas_call(
        functools.partial(_resid_ln_kernel, alpha),
        grid=(n // tm,),
        in_specs=[pl.BlockSpec((tm, d), row), pl.BlockSpec((tm, d), row),
                  pl.BlockSpec((1, d), const), pl.BlockSpec((1, d), const)],
        out_specs=pl.BlockSpec((tm, d), row),
        out_shape=jax.ShapeDtypeStruct((n, d), F32),
        compiler_params=_cparams(("parallel",)), name="resid_ln",
    )(x, y, g, b)


def _moe_layer(x, w_router, wg, wu, wd, g, b, alpha):
    n, d = x.shape
    n_exp = w_router.shape[1]
    tm = 2048 if n >= 2048 else _pick_tile(n, (1024, 512, 256, 128))
    wr_pad = jnp.zeros((d, LANES), F32).at[:, :n_exp].set(w_router)
    xb, rank, gate, rankt, cnt = _route_call(x, wr_pad, n_exp, tm)
    nblk = ((cnt[:, 0, :n_exp] + (MOE_ROWS - 1)) // MOE_ROWS).reshape(-1)
    y = _moe_call(n, xb, rank, gate, rankt, nblk, wg, wu, wd, tm)
    return _resid_ln_call(x, y, g, b, alpha)


def _swap_halves(w):
    half = w.shape[-1] // 2
    return jnp.concatenate([w[..., half:], w[..., :half]], axis=-1)


def _rope_tables(pos, scale):
    inv = ROPE_THETA ** (-jnp.arange(QK_ROPE // 2, dtype=F32) * (2.0 / QK_ROPE))
    ang = pos.astype(F32)[:, None] * inv[None, :]
    cos, sin = jnp.cos(ang), jnp.sin(ang)
    n = pos.shape[0]
    pad = jnp.zeros((n, HEAD_PAD - QK_NOPE - QK_ROPE), F32)
    ct = jnp.concatenate([jnp.ones((n, QK_NOPE), F32), cos, cos, pad], axis=1) * scale
    st = jnp.concatenate([jnp.zeros((n, QK_NOPE), F32), -sin, sin, pad], axis=1) * scale
    return ct, st


def kernel(x_prompt, x_sample, cache_ckv, cache_kr, state_conv, ln_in_g, ln_in_b, w_in, b_in, g_qnorm, w_uq, g_kvnorm, w_uk, w_uv, w_o_attn, conv_w, conv_b, conv_ln_g, conv_ln_b, w_conv_out, w_out, ln1_g, ln1_b, ln2_g, ln2_b, w_ffn_gate, w_ffn_up, w_ffn_down, w_router, w_exp_gate, w_exp_up, w_exp_down):
    batch, seq, d = x_prompt.shape
    n_str, t_new, _ = x_sample.shape
    depth, _, past_len, kv_rank = cache_ckv.shape
    q_rank = g_qnorm.shape[1]
    c_conv = conv_w.shape[2]
    conv_width = conv_w.shape[1]
    halo = 32
    assert conv_width - 1 <= halo and t_new >= conv_width - 1 and t_new % 16 == 0
    assert w_uq.shape[2] == N_HEADS * (QK_NOPE + QK_ROPE) and cache_kr.shape[3] == QK_ROPE
    alpha = (2 * depth) ** 0.25
    n_p = batch * seq
    n_s = n_str * t_new
    n = n_p + n_s

    o1, o2, o3 = q_rank, q_rank + kv_rank, q_rank + kv_rank + QK_ROPE
    o4 = o3 + 2 * c_conv

    def place_kr(w):
        lead = jnp.zeros(w.shape[:-1] + (QK_NOPE,), F32)
        tail = jnp.zeros(w.shape[:-1] + (HEAD_PAD - QK_NOPE - QK_ROPE,), F32)
        return jnp.concatenate([lead, w, tail], axis=-1)

    def split_a(w):
        kr = w[..., o2:o3]
        return jnp.concatenate([w[..., :o2], place_kr(kr), place_kr(_swap_halves(kr))], axis=-1)

    wa = split_a(w_in).astype(BF16)
    ba = split_a(b_in)[:, None, :]
    wb = w_in[..., o3:].astype(BF16)
    bb = b_in[:, None, o3:]
    wq = w_uq.reshape(depth, q_rank, N_HEADS, QK_NOPE + QK_ROPE)
    wq_n, wq_r = wq[..., :QK_NOPE], wq[..., QK_NOPE:]
    z_tail = jnp.zeros(wq_r.shape[:-1] + (HEAD_PAD - QK_NOPE - QK_ROPE,), F32)
    wqa = jnp.concatenate([wq_n, wq_r, z_tail], -1).reshape(depth, q_rank, -1).astype(BF16)
    wqb = jnp.concatenate([jnp.zeros_like(wq_n), _swap_halves(wq_r), z_tail], -1)
    wqb = wqb.reshape(depth, q_rank, -1).astype(BF16)
    wuk = jnp.concatenate([w_uk, jnp.zeros(w_uk.shape[:-1] + (HEAD_PAD - QK_NOPE,), F32)], -1)
    wuk = wuk.reshape(depth, kv_rank, -1).astype(BF16)
    wuv = w_uv.reshape(depth, kv_rank, -1).astype(BF16)
    place = jnp.tile(place_kr(jnp.eye(QK_ROPE, dtype=F32)), (1, N_HEADS)).astype(BF16)
    woa = w_o_attn.astype(BF16)
    wco = w_conv_out.astype(BF16)
    wout = w_out.astype(BF16)
    wfg, wfu, wfd = w_ffn_gate.astype(BF16), w_ffn_up.astype(BF16), w_ffn_down.astype(BF16)
    weg, weu, wed = w_exp_gate.astype(BF16), w_exp_up.astype(BF16), w_exp_down.astype(BF16)

    pos = jnp.concatenate([jnp.tile(jnp.arange(seq, dtype=jnp.int32), batch),
                           jnp.tile(past_len + jnp.arange(t_new, dtype=jnp.int32), n_str)])
    ctq, stq = _rope_tables(pos, ATTN_SCALE)
    ck, sk = _rope_tables(pos, 1.0)
    ck = ck.at[:, :QK_NOPE].set(0.0)
    tabs = dict(ctq=ctq, stq=stq, ck=ck, sk=sk)

    kpast, vpast = _kvpast_call(cache_ckv.reshape(depth, n_str * past_len, kv_rank),
                                cache_kr.reshape(depth, n_str * past_len, QK_ROPE), wuk, wuv, place)
    kpast = kpast.reshape(depth * n_str, past_len, -1)
    vpast = vpast.reshape(depth * n_str, past_len, -1)

    x = _ln_call(jnp.concatenate([x_prompt.reshape(n_p, d), x_sample.reshape(n_s, d)], axis=0),
                 ln_in_g, ln_in_b)
    tc = _pick_tile(seq, (512, 256, 128, 64))
    ckv_l, kr_l, conv_p_l, conv_s_l = [], [], [], []
    for l in range(depth):
        wl = dict(wa=wa[l], ba=ba[l], wb=wb[l], bb=bb[l], gq=g_qnorm[l][None], wqa=wqa[l], wqb=wqb[l],
                  gkv=g_kvnorm[l][None], wuk=wuk[l], wuv=wuv[l], woa=woa[l], wco=wco[l], wout=wout[l],
                  ln1g=ln1_g[l][None], ln1b=ln1_b[l][None])
        q, k, v, ckv, kr, u, gates = _inproj_call(x, wl, tabs, (q_rank, kv_rank, c_conv))
        attn_p = _attn_prompt_call(q, k, v, batch, seq)
        attn_s = _attn_sample_call(q, k, v, kpast, vpast, l, n_p, n_str, t_new, past_len)
        u_p = u[:n_p].reshape(batch, seq // tc, tc, c_conv)
        halo_p = jnp.concatenate([jnp.zeros((batch, 1, halo, c_conv), F32),
                                  u_p[:, :-1, tc - halo:, :]], axis=1).reshape(-1, c_conv)
        u_s = u[n_p:]
        halo_s = jnp.concatenate([jnp.zeros((n_str, halo - (conv_width - 1), c_conv), F32),
                                  state_conv[l]], axis=1).reshape(-1, c_conv)
        conv_args = (conv_w[l], conv_b[l], conv_ln_g[l], conv_ln_b[l])
        cact_p = _conv_call(u[:n_p], halo_p, tc, *conv_args)
        cact_s = _conv_call(u_s, halo_s, t_new, *conv_args)
        x = _outproj_call(x, jnp.concatenate([attn_p, attn_s], axis=0),
                          jnp.concatenate([cact_p, cact_s], axis=0), gates, wl, alpha)
        i = l // 2
        if l % 2 == 0:
            x = _ffn_call(x, wfg[i], wfu[i], wfd[i], ln2_g[l][None], ln2_b[l][None], alpha)
        else:
            x = _moe_layer(x, w_router[i], weg[i], weu[i], wed[i], ln2_g[l][None], ln2_b[l][None],
                           alpha)
        ckv_l.append(ckv)
        kr_l.append(kr[:, QK_NOPE:QK_NOPE + QK_ROPE])
        conv_p_l.append(u[:n_p].reshape(batch, seq, c_conv)[:, seq - (conv_width - 1):, :])
        conv_s_l.append(u_s.reshape(n_str, t_new, c_conv)[:, t_new - (conv_width - 1):, :])
    ckv_all = jnp.stack(ckv_l)
    kr_all = jnp.stack(kr_l)
    return (x[:n_p].reshape(batch, seq, d), x[n_p:].reshape(n_str, t_new, d),
            ckv_all[:, :n_p].reshape(depth, batch, seq, kv_rank),
            kr_all[:, :n_p].reshape(depth, batch, seq, QK_ROPE),
            jnp.stack(conv_p_l),
            ckv_all[:, n_p:].reshape(depth, n_str, t_new, kv_rank),
            kr_all[:, n_p:].reshape(depth, n_str, t_new, QK_ROPE),
            jnp.stack(conv_s_l))
```

```python
import functools
import math

import numpy as np
import jax
import jax.numpy as jnp
from jax import lax
from jax.experimental import pallas as pl
from jax.experimental.pallas import tpu as pltpu

CHUNK = 64
N_HEADS = 8
QK_NOPE = 64
QK_ROPE = 32
V_HEAD = 64
ROPE_THETA = 10000.0
TOP_K = 2
LN_EPS = 1e-5
NEG_INF = -1e30
ATTN_SCALE = (QK_NOPE + QK_ROPE) ** -0.5

LANES = 128
HEAD_PAD = 128
HEADS_PER_STEP = 2
VT_ROWS = V_HEAD + 16
VT_BLOCK = 256
LOG2E = math.log2(math.e)
MOE_ROWS = 256
VMEM_LIMIT = 56 * 1024 * 1024

BF16 = jnp.bfloat16
F32 = jnp.float32


def _cparams(sem):
    return pltpu.CompilerParams(dimension_semantics=sem, vmem_limit_bytes=VMEM_LIMIT)


def _pick_tile(n, candidates):
    for c in candidates:
        if n % c == 0:
            return c
    raise ValueError(f"no tile in {candidates} divides {n}")


def _dot(a, b):
    return jnp.dot(a, b, preferred_element_type=F32)


def _dot_nt(a, b):
    return lax.dot_general(a, b, (((1,), (1,)), ((), ())), preferred_element_type=F32)


def _layernorm(x, g, b):
    mu = jnp.mean(x, axis=-1, keepdims=True)
    xc = x - mu
    var = jnp.mean(xc * xc, axis=-1, keepdims=True)
    return xc * lax.rsqrt(var + LN_EPS) * g + b


def _rmsnorm(x, g):
    return x * lax.rsqrt(jnp.mean(x * x, axis=-1, keepdims=True) + LN_EPS) * g


def _silu(x):
    return x * jax.nn.sigmoid(x)


def _ln_kernel(x_ref, g_ref, b_ref, o_ref):
    o_ref[...] = _layernorm(x_ref[...], g_ref[...], b_ref[...])


def _ln_call(x, g, b):
    n, d = x.shape
    tm = _pick_tile(n, (1024, 512, 256, 128, 64, 32))
    return pl.pallas_call(
        _ln_kernel,
        grid=(n // tm,),
        in_specs=[pl.BlockSpec((tm, d), lambda i: (i, 0)),
                  pl.BlockSpec((1, d), lambda i: (0, 0)),
                  pl.BlockSpec((1, d), lambda i: (0, 0))],
        out_specs=pl.BlockSpec((tm, d), lambda i: (i, 0)),
        out_shape=jax.ShapeDtypeStruct((n, d), F32),
        compiler_params=_cparams(("parallel",)),
        name="ln_in",
    )(x, g.reshape(1, d), b.reshape(1, d))


def _inproj_kernel(q_rank, kv_rank, c_conv,
                   x_ref, wa_ref, ba_ref, wb_ref, bb_ref, gq_ref, wqa_ref, wqb_ref, gkv_ref,
                   wuk_ref, wuv_ref, wuvt_ref, ctq_ref, stq_ref, ck_ref, sk_ref,
                   q_ref, k_ref, v_ref, vt_ref, ckv_ref, kr_ref, u_ref, gate_ref):
    xb = x_ref[...].astype(BF16)
    pa = _dot(xb, wa_ref[...]) + ba_ref[...]
    o1, o2, o3 = q_rank, q_rank + kv_rank, q_rank + kv_rank + HEAD_PAD
    cqn = _rmsnorm(pa[:, :o1], gq_ref[...]).astype(BF16)
    ct = jnp.concatenate([ctq_ref[...]] * N_HEADS, axis=1)
    st = jnp.concatenate([stq_ref[...]] * N_HEADS, axis=1)
    q = _dot(cqn, wqa_ref[...]) * ct + _dot(cqn, wqb_ref[...]) * st
    q_ref[...] = q.astype(BF16)
    ckv = _rmsnorm(pa[:, o1:o2], gkv_ref[...])
    ckv_ref[...] = ckv
    kr = pa[:, o2:o3] * ck_ref[...] + pa[:, o3:] * sk_ref[...]
    kr_ref[...] = kr
    ckvb = ckv.astype(BF16)
    k = _dot(ckvb, wuk_ref[...]) + jnp.concatenate([kr] * N_HEADS, axis=1)
    k_ref[...] = k.astype(BF16)
    v_ref[...] = _dot(ckvb, wuv_ref[...]).astype(BF16)
    vt = _dot_nt(wuvt_ref[...], ckvb)
    ones_row = lax.broadcasted_iota(jnp.int32, vt.shape, 0) % VT_ROWS >= V_HEAD
    vt_ref[...] = jnp.where(ones_row, 1.0, vt).astype(BF16)
    pu = _dot(xb, wb_ref[:, :2 * c_conv]) + bb_ref[:, :2 * c_conv]
    u_ref[...] = pu[:, :c_conv] * jax.nn.sigmoid(pu[:, c_conv:])
    pg = _dot(xb, wb_ref[:, 2 * c_conv:]) + bb_ref[:, 2 * c_conv:]
    gate_ref[...] = jax.nn.sigmoid(pg).astype(BF16)


def _inproj_call(x, wl, tabs, dims):
    n, d = x.shape
    q_rank, kv_rank, c_conv = dims
    tm = VT_BLOCK
    assert n % tm == 0
    na = wl["wa"].shape[1]
    nb = wl["wb"].shape[1]
    hq = N_HEADS * HEAD_PAD
    hv = N_HEADS * V_HEAD
    hvt = N_HEADS * VT_ROWS
    const = lambda i: (0, 0)
    row = lambda i: (i, 0)
    in_specs = [
        pl.BlockSpec((tm, d), row),
        pl.BlockSpec((d, na), const), pl.BlockSpec((1, na), const),
        pl.BlockSpec((d, nb), const), pl.BlockSpec((1, nb), const),
        pl.BlockSpec((1, q_rank), const),
        pl.BlockSpec((q_rank, hq), const), pl.BlockSpec((q_rank, hq), const),
        pl.BlockSpec((1, kv_rank), const),
        pl.BlockSpec((kv_rank, hq), const), pl.BlockSpec((kv_rank, hv), const),
        pl.BlockSpec((hvt, kv_rank), const),
        pl.BlockSpec((tm, HEAD_PAD), row), pl.BlockSpec((tm, HEAD_PAD), row),
        pl.BlockSpec((tm, HEAD_PAD), row), pl.BlockSpec((tm, HEAD_PAD), row),
    ]
    out_specs = [
        pl.BlockSpec((tm, hq), row), pl.BlockSpec((tm, hq), row), pl.BlockSpec((tm, hv), row),
        pl.BlockSpec((None, hvt, tm), lambda i: (i, 0, 0)),
        pl.BlockSpec((tm, kv_rank), row), pl.BlockSpec((tm, HEAD_PAD), row),
        pl.BlockSpec((tm, c_conv), row), pl.BlockSpec((tm, 2 * d), row),
    ]
    out_shape = [
        jax.ShapeDtypeStruct((n, hq), BF16), jax.ShapeDtypeStruct((n, hq), BF16),
        jax.ShapeDtypeStruct((n, hv), BF16), jax.ShapeDtypeStruct((n // tm, hvt, tm), BF16),
        jax.ShapeDtypeStruct((n, kv_rank), F32), jax.ShapeDtypeStruct((n, HEAD_PAD), F32),
        jax.ShapeDtypeStruct((n, c_conv), F32), jax.ShapeDtypeStruct((n, 2 * d), BF16),
    ]
    return pl.pallas_call(
        functools.partial(_inproj_kernel, q_rank, kv_rank, c_conv),
        grid=(n // tm,), in_specs=in_specs, out_specs=out_specs, out_shape=out_shape,
        compiler_params=_cparams(("parallel",)), name="inproj",
    )(x, wl["wa"], wl["ba"], wl["wb"], wl["bb"], wl["gq"], wl["wqa"], wl["wqb"], wl["gkv"],
      wl["wuk"], wl["wuv"], wl["wuvt"], tabs["ctq"], tabs["stq"], tabs["ck"], tabs["sk"])


def _kvpast_kernel(ckv_ref, kr_ref, wuk_ref, wuv_ref, place_ref, k_ref, v_ref):
    ckvb = ckv_ref[...].astype(BF16)
    k = _dot(ckvb, wuk_ref[...]) + _dot(kr_ref[...].astype(BF16), place_ref[...])
    k_ref[...] = k.astype(BF16)
    v_ref[...] = _dot(ckvb, wuv_ref[...]).astype(BF16)


def _kvpast_call(ckv, kr, wuk, wuv, place):
    depth, rows, kv_rank = ckv.shape
    rope = kr.shape[-1]
    tm = _pick_tile(rows, (1024, 512, 256, 128, 64, 32))
    tpl = rows // tm
    hq = N_HEADS * HEAD_PAD
    hv = N_HEADS * V_HEAD
    return pl.pallas_call(
        _kvpast_kernel,
        grid=(depth, tpl),
        in_specs=[pl.BlockSpec((None, tm, kv_rank), lambda l, i: (l, i, 0)),
                  pl.BlockSpec((None, tm, rope), lambda l, i: (l, i, 0)),
                  pl.BlockSpec((None, kv_rank, hq), lambda l, i: (l, 0, 0)),
                  pl.BlockSpec((None, kv_rank, hv), lambda l, i: (l, 0, 0)),
                  pl.BlockSpec((rope, hq), lambda l, i: (0, 0))],
        out_specs=[pl.BlockSpec((None, tm, hq), lambda l, i: (l, i, 0)),
                   pl.BlockSpec((None, tm, hv), lambda l, i: (l, i, 0))],
        out_shape=[jax.ShapeDtypeStruct((depth, rows, hq), BF16),
                   jax.ShapeDtypeStruct((depth, rows, hv), BF16)],
        compiler_params=_cparams(("parallel", "parallel")), name="kvpast",
    )(ckv, kr, wuk, wuv, place)


def _attn_prompt_kernel(tq, q_ref, k_ref, vt_ref, o_ref, sa_ref, sb_ref):
    qi = pl.program_id(2)
    heads = range(HEADS_PER_STEP)
    nvb = tq // VT_BLOCK

    def scores(j, s_ref):
        start = pl.multiple_of(j * tq, tq)
        for h in heads:
            s_ref[h] = _dot_nt(k_ref[pl.ds(start, tq), h * HEAD_PAD:(h + 1) * HEAD_PAD],
                               q_ref[:, h * HEAD_PAD:(h + 1) * HEAD_PAD])

    def softmax_pv(j, s_ref, state, masked):
        out = []
        for h in heads:
            s = s_ref[h]
            m_old, acc = state[h]
            if masked:
                kc = lax.broadcasted_iota(jnp.int32, s.shape, 0) // CHUNK
                qc = lax.broadcasted_iota(jnp.int32, s.shape, 1) // CHUNK
                s = jnp.where(kc <= qc, s, NEG_INF)
            m_new = jnp.maximum(m_old, jnp.max(s, axis=0, keepdims=True))
            p = jnp.exp2(s - m_new).astype(BF16)
            pv = None
            for b in range(nvb):
                vth = vt_ref[j * nvb + b, h * VT_ROWS:(h + 1) * VT_ROWS, :]
                d = _dot(vth, p[b * VT_BLOCK:(b + 1) * VT_BLOCK, :])
                pv = d if pv is None else pv + d
            out.append((m_new, jnp.exp2(m_old - m_new) * acc + pv))
        return tuple(out)

    def pair(t, state):
        j = 2 * t
        scores(j + 1, sb_ref)
        state = softmax_pv(j, sa_ref, state, False)
        scores(j + 2, sa_ref)
        return softmax_pv(j + 1, sb_ref, state, False)

    def odd_tail(state):
        scores(qi, sb_ref)
        state = softmax_pv(qi - 1, sa_ref, state, False)
        return softmax_pv(qi, sb_ref, state, True)

    def even_tail(state):
        return softmax_pv(qi, sa_ref, state, True)

    init = tuple((jnp.full((1, tq), NEG_INF, F32), jnp.zeros((VT_ROWS, tq), F32)) for _ in heads)
    scores(0, sa_ref)
    state = lax.fori_loop(0, qi // 2, pair, init)
    state = lax.cond(qi % 2 == 1, odd_tail, even_tail, state)
    rows = [acc[:V_HEAD, :] / acc[V_HEAD:V_HEAD + 1, :] for _, acc in state]
    o_ref[...] = jnp.transpose(jnp.concatenate(rows, axis=0)).astype(BF16)


def _attn_prompt_call(q, k, vt, batch, seq):
    tq = _pick_tile(seq, (512, 256))
    nq = seq // tq
    hp = N_HEADS // HEADS_PER_STEP
    qw = HEADS_PER_STEP * HEAD_PAD
    vw = HEADS_PER_STEP * V_HEAD
    vr = HEADS_PER_STEP * VT_ROWS
    nvb = seq // VT_BLOCK
    return pl.pallas_call(
        functools.partial(_attn_prompt_kernel, tq),
        grid=(batch, hp, nq),
        in_specs=[pl.BlockSpec((tq, qw), lambda b, h, i: (b * nq + i, h)),
                  pl.BlockSpec((seq, qw), lambda b, h, i: (b, h)),
                  pl.BlockSpec((nvb, vr, VT_BLOCK), lambda b, h, i: (b, h, 0))],
        out_specs=pl.BlockSpec((tq, vw), lambda b, h, i: (b * nq + i, h)),
        out_shape=jax.ShapeDtypeStruct((batch * seq, N_HEADS * V_HEAD), BF16),
        scratch_shapes=[pltpu.VMEM((HEADS_PER_STEP, tq, tq), F32),
                        pltpu.VMEM((HEADS_PER_STEP, tq, tq), F32)],
        compiler_params=_cparams(("parallel", "parallel", "arbitrary")), name="attn_prompt",
    )(q, k, vt)


def _attn_sample_kernel(mask_past, mask_new, past_len,
                        q_ref, kn_ref, vn_ref, kp_ref, vp_ref, o_ref):
    vp = vp_ref[...]
    vn = vn_ref[...]
    outs = []
    for h in range(HEADS_PER_STEP):
        sl = slice(h * HEAD_PAD, (h + 1) * HEAD_PAD)
        qh = q_ref[:, sl]
        sp = _dot_nt(qh, kp_ref[:, sl])
        sn = _dot_nt(qh, kn_ref[:, sl])
        if mask_past:
            qc = (lax.broadcasted_iota(jnp.int32, sp.shape, 0) + past_len) // CHUNK
            kc = lax.broadcasted_iota(jnp.int32, sp.shape, 1) // CHUNK
            sp = jnp.where(kc <= qc, sp, NEG_INF)
        if mask_new:
            qc = (lax.broadcasted_iota(jnp.int32, sn.shape, 0) + past_len) // CHUNK
            kc = (lax.broadcasted_iota(jnp.int32, sn.shape, 1) + past_len) // CHUNK
            sn = jnp.where(kc <= qc, sn, NEG_INF)
        m = jnp.maximum(jnp.max(sp, axis=1, keepdims=True), jnp.max(sn, axis=1, keepdims=True))
        pp = jnp.exp2(sp - m)
        pn = jnp.exp2(sn - m)
        l = jnp.sum(pp, axis=1, keepdims=True) + jnp.sum(pn, axis=1, keepdims=True)
        outs.append((_dot(pp.astype(BF16), vp) + _dot(pn.astype(BF16), vn)) / l)
    lane = lax.broadcasted_iota(jnp.int32, outs[0].shape, 1)
    o_ref[...] = jnp.where(lane < V_HEAD, outs[0], outs[1]).astype(BF16)


def _attn_sample_call(q, k, v, kpast, vpast, layer, row0, n_streams, t_new, past_len):
    hp = N_HEADS // HEADS_PER_STEP
    qw = HEADS_PER_STEP * HEAD_PAD
    vw = HEADS_PER_STEP * V_HEAD
    blk0 = row0 // t_new
    q_pos = past_len + np.arange(t_new)
    mask_past = not bool(np.all((np.arange(past_len) // CHUNK)[None, :] <= (q_pos // CHUNK)[:, None]))
    mask_new = not bool(np.all((q_pos // CHUNK)[None, :] <= (q_pos // CHUNK)[:, None]))
    new_spec = lambda w: pl.BlockSpec((t_new, w), lambda s, h: (blk0 + s, h))
    return pl.pallas_call(
        functools.partial(_attn_sample_kernel, mask_past, mask_new, past_len),
        grid=(n_streams, hp),
        in_specs=[new_spec(qw), new_spec(qw), new_spec(vw),
                  pl.BlockSpec((None, past_len, qw), lambda s, h: (layer * n_streams + s, 0, h)),
                  pl.BlockSpec((None, past_len, vw), lambda s, h: (layer * n_streams + s, 0, h))],
        out_specs=pl.BlockSpec((t_new, vw), lambda s, h: (s, h)),
        out_shape=jax.ShapeDtypeStruct((n_streams * t_new, N_HEADS * V_HEAD), BF16),
        compiler_params=_cparams(("parallel", "parallel")), name="attn_sample",
    )(q, k, v, kpast, vpast)


def _conv_kernel(tc, halo, conv_w, rc, u_ref, h_ref, w_ref, b_ref, g_ref, beta_ref, o_ref, ext_ref):
    ext_ref[0:halo, :] = h_ref[...]
    ext_ref[halo:halo + tc, :] = u_ref[...]
    lead = halo - (conv_w - 1)
    for c in range(tc // rc):
        acc = jnp.broadcast_to(b_ref[...], (rc, u_ref.shape[1]))
        for j in range(conv_w):
            r0 = c * rc + lead + j
            acc = acc + w_ref[j:j + 1, :] * ext_ref[r0:r0 + rc, :]
        y = _layernorm(acc, g_ref[...], beta_ref[...])
        o_ref[c * rc:(c + 1) * rc, :] = _silu(y).astype(BF16)


def _conv_call(u, halo_rows, tc, w, b, g, beta):
    n, c = u.shape
    s = n // tc
    halo = halo_rows.shape[0] // s
    conv_w = w.shape[0]
    rc = min(tc, 64)
    const = lambda i: (0, 0)
    return pl.pallas_call(
        functools.partial(_conv_kernel, tc, halo, conv_w, rc),
        grid=(s,),
        in_specs=[pl.BlockSpec((tc, c), lambda i: (i, 0)),
                  pl.BlockSpec((halo, c), lambda i: (i, 0)),
                  pl.BlockSpec((conv_w, c), const), pl.BlockSpec((1, c), const),
                  pl.BlockSpec((1, c), const), pl.BlockSpec((1, c), const)],
        out_specs=pl.BlockSpec((tc, c), lambda i: (i, 0)),
        out_shape=jax.ShapeDtypeStruct((n, c), BF16),
        scratch_shapes=[pltpu.VMEM((halo + tc, c), F32)],
        compiler_params=_cparams(("parallel",)), name="conv",
    )(u, halo_rows, w, b.reshape(1, c), g.reshape(1, c), beta.reshape(1, c))


def _outproj_kernel(alpha, d, x_ref, a_ref, c_ref, gate_ref, woa_ref, wco_ref, wout_ref,
                    g_ref, b_ref, o_ref):
    br_a = _dot(a_ref[...], woa_ref[...])
    br_c = _dot(c_ref[...], wco_ref[...])
    merged = gate_ref[:, :d].astype(F32) * br_a + gate_ref[:, d:].astype(F32) * br_c
    mix = _dot(merged.astype(BF16), wout_ref[...])
    o_ref[...] = _layernorm(alpha * x_ref[...] + mix, g_ref[...], b_ref[...])


def _outproj_call(x, attn, cact, gates, wl, alpha):
    n, d = x.shape
    tm = _pick_tile(n, (512, 256, 128, 64, 32))
    const = lambda i: (0, 0)
    row = lambda i: (i, 0)
    ha, hc = attn.shape[1], cact.shape[1]
    return pl.pallas_call(
        functools.partial(_outproj_kernel, alpha, d),
        grid=(n // tm,),
        in_specs=[pl.BlockSpec((tm, d), row), pl.BlockSpec((tm, ha), row),
                  pl.BlockSpec((tm, hc), row), pl.BlockSpec((tm, 2 * d), row),
                  pl.BlockSpec((ha, d), const), pl.BlockSpec((hc, d), const),
                  pl.BlockSpec((d, d), const), pl.BlockSpec((1, d), const),
                  pl.BlockSpec((1, d), const)],
        out_specs=pl.BlockSpec((tm, d), row),
        out_shape=jax.ShapeDtypeStruct((n, d), F32),
        compiler_params=_cparams(("parallel",)), name="outproj",
    )(x, attn, cact, gates, wl["woa"], wl["wco"], wl["wout"], wl["ln1g"], wl["ln1b"])


def _ffn_kernel(alpha, x_ref, wg_ref, wu_ref, wd_ref, g_ref, b_ref, o_ref, acc_ref):
    f = pl.program_id(1)

    @pl.when(f == 0)
    def _():
        acc_ref[...] = jnp.zeros(acc_ref.shape, F32)

    xb = x_ref[...].astype(BF16)
    h = _silu(_dot(xb, wg_ref[...])) * _dot(xb, wu_ref[...])
    acc_ref[...] += _dot(h.astype(BF16), wd_ref[...])

    @pl.when(f == pl.num_programs(1) - 1)
    def _():
        o_ref[...] = _layernorm(alpha * x_ref[...] + acc_ref[...], g_ref[...], b_ref[...])


def _ffn_call(x, wg, wu, wd, g, b, alpha):
    n, d = x.shape
    dff = wg.shape[1]
    tm = _pick_tile(n, (512, 256, 128, 64, 32))
    tf = _pick_tile(dff, (1408, 1024, 512, 256, 128))
    return pl.pallas_call(
        functools.partial(_ffn_kernel, alpha),
        grid=(n // tm, dff // tf),
        in_specs=[pl.BlockSpec((tm, d), lambda i, f: (i, 0)),
                  pl.BlockSpec((d, tf), lambda i, f: (0, f)),
                  pl.BlockSpec((d, tf), lambda i, f: (0, f)),
                  pl.BlockSpec((tf, d), lambda i, f: (f, 0)),
                  pl.BlockSpec((1, d), lambda i, f: (0, 0)),
                  pl.BlockSpec((1, d), lambda i, f: (0, 0))],
        out_specs=pl.BlockSpec((tm, d), lambda i, f: (i, 0)),
        out_shape=jax.ShapeDtypeStruct((n, d), F32),
        scratch_shapes=[pltpu.VMEM((tm, d), F32)],
        compiler_params=_cparams(("parallel", "arbitrary")), name="ffn",
    )(x, wg, wu, wd, g, b)


def _route_kernel(n_tok, n_exp, tm, x_ref, wr_ref, xb_ref, rank_ref, gate_ref, rankt_ref, cnt_ref):
    i = pl.program_id(0)
    x = x_ref[...]
    row = lax.broadcasted_iota(jnp.int32, (tm, 1), 0) + i * tm
    valid = row < n_tok
    xb_ref[...] = jnp.where(valid, x, 0.0).astype(BF16)
    logits = jnp.dot(x, wr_ref[...], preferred_element_type=F32, precision=lax.Precision.HIGHEST)
    lane = lax.broadcasted_iota(jnp.int32, (tm, LANES), 1)
    logits = jnp.where(lane < n_exp, logits, -jnp.inf)
    m1 = jnp.max(logits, axis=1, keepdims=True)
    i1 = jnp.min(jnp.where(logits == m1, lane, LANES), axis=1, keepdims=True)
    rest = jnp.where(lane == i1, -jnp.inf, logits)
    m2 = jnp.max(rest, axis=1, keepdims=True)
    i2 = jnp.min(jnp.where(rest == m2, lane, LANES), axis=1, keepdims=True)
    e2 = jnp.exp(m2 - m1)
    g1 = 1.0 / (1.0 + e2)
    g2 = e2 / (1.0 + e2)
    sel1 = (lane == i1) & valid
    sel2 = (lane == i2) & valid
    gate_ref[...] = jnp.where(sel1, g1, 0.0) + jnp.where(sel2, g2, 0.0)
    sel = jnp.where(sel1 | sel2, 1.0, 0.0)
    cs = min(tm, 256)
    r = lax.broadcasted_iota(jnp.int32, (cs, cs), 0)
    c = lax.broadcasted_iota(jnp.int32, (cs, cs), 1)
    lower = jnp.where(c < r, 1.0, 0.0).astype(BF16)
    offs = jnp.zeros((1, LANES), F32)
    for ch in range(tm // cs):
        sc = sel[ch * cs:(ch + 1) * cs, :]
        rk = _dot(lower, sc.astype(BF16)) + offs
        rank_ref[ch * cs:(ch + 1) * cs, :] = jnp.where(sc > 0.0, rk, -1.0)
        offs = offs + jnp.sum(sc, axis=0, keepdims=True)
    cnt_ref[...] = offs.astype(jnp.int32)
    rankt_ref[...] = jnp.transpose(rank_ref[...])[:rankt_ref.shape[0], :]


def _route_call(x, wr_pad, n_exp, tm):
    n, d = x.shape
    nt = -(-n // tm)
    return pl.pallas_call(
        functools.partial(_route_kernel, n, n_exp, tm),
        grid=(nt,),
        in_specs=[pl.BlockSpec((tm, d), lambda i: (i, 0)),
                  pl.BlockSpec((d, LANES), lambda i: (0, 0))],
        out_specs=[pl.BlockSpec((tm, d), lambda i: (i, 0)),
                   pl.BlockSpec((tm, LANES), lambda i: (i, 0)),
                   pl.BlockSpec((tm, LANES), lambda i: (i, 0)),
                   pl.BlockSpec((n_exp, tm), lambda i: (0, i)),
                   pl.BlockSpec((None, 1, LANES), lambda i: (i, 0, 0))],
        out_shape=[jax.ShapeDtypeStruct((nt * tm, d), BF16),
                   jax.ShapeDtypeStruct((nt * tm, LANES), F32),
                   jax.ShapeDtypeStruct((nt * tm, LANES), F32),
                   jax.ShapeDtypeStruct((n_exp, nt * tm), F32),
                   jax.ShapeDtypeStruct((nt, 1, LANES), jnp.int32)],
        compiler_params=_cparams(("parallel",)), name="route",
    )(x, wr_pad)


def _moe_kernel(tm, nblk_ref, xb_ref, rank_ref, gate_ref, rankt_ref, wg_ref, wu_ref, wd_ref,
                y_ref, xe_ref, oe_ref, rcol_ref, gcol_ref):
    i = pl.program_id(0)
    e = pl.program_id(1)
    f = pl.program_id(2)
    n_e = pl.num_programs(1)
    nb = nblk_ref[i * n_e + e]
    rows = MOE_ROWS

    @pl.when((e == 0) & (f == 0))
    def _():
        y_ref[...] = jnp.zeros(y_ref.shape, F32)

    @pl.when(f == 0)
    def _():
        lane = lax.broadcasted_iota(jnp.int32, (tm, LANES), 1)
        rcol_ref[...] = jnp.max(jnp.where(lane == e, rank_ref[...], -1.0), axis=1, keepdims=True)
        gcol_ref[...] = jnp.sum(jnp.where(lane == e, gate_ref[...], 0.0), axis=1, keepdims=True)
        rrow = rankt_ref[pl.ds(e, 1), :]

        def gather(j, carry):
            slot = (lax.broadcasted_iota(jnp.int32, (rows, tm), 0) + j * rows).astype(F32)
            onehot = jnp.where(rrow == slot, 1.0, 0.0).astype(BF16)
            r0 = pl.multiple_of(j * rows, rows)
            xe_ref[pl.ds(r0, rows), :] = _dot(onehot, xb_ref[...]).astype(BF16)
            return carry

        lax.fori_loop(0, nb, gather, 0)

    def expert(j, carry):
        r0 = pl.multiple_of(j * rows, rows)
        xe = xe_ref[pl.ds(r0, rows), :]
        h = _silu(_dot(xe, wg_ref[...])) * _dot(xe, wu_ref[...])
        o = _dot(h.astype(BF16), wd_ref[...])

        @pl.when(f == 0)
        def _():
            oe_ref[pl.ds(r0, rows), :] = o

        @pl.when(f != 0)
        def _():
            oe_ref[pl.ds(r0, rows), :] += o

        return carry

    lax.fori_loop(0, nb, expert, 0)

    @pl.when(f == pl.num_programs(2) - 1)
    def _():
        def scatter(j, carry):
            r0 = pl.multiple_of(j * rows, rows)
            ob = oe_ref[pl.ds(r0, rows), :].astype(BF16)
            sb = min(tm, 512)
            for t in range(tm // sb):
                ts = slice(t * sb, (t + 1) * sb)
                slot = (lax.broadcasted_iota(jnp.int32, (sb, rows), 1) + j * rows).astype(F32)
                onehot_t = jnp.where(rcol_ref[ts, :] == slot, 1.0, 0.0).astype(BF16)
                y_ref[ts, :] += gcol_ref[ts, :] * _dot(onehot_t, ob)
            return carry

        lax.fori_loop(0, nb, scatter, 0)


def _moe_call(n_tok, xb, rank, gate, rankt, nblk, wg, wu, wd, tm):
    npad, d = xb.shape
    nt = npad // tm
    n_exp, _, dff = wg.shape
    tf = _pick_tile(dff, (512, 256, 128))
    grid_spec = pltpu.PrefetchScalarGridSpec(
        num_scalar_prefetch=1,
        grid=(nt, n_exp, dff // tf),
        in_specs=[pl.BlockSpec((tm, d), lambda i, e, f, nb: (i, 0)),
                  pl.BlockSpec((tm, LANES), lambda i, e, f, nb: (i, 0)),
                  pl.BlockSpec((tm, LANES), lambda i, e, f, nb: (i, 0)),
                  pl.BlockSpec((n_exp, tm), lambda i, e, f, nb: (0, i)),
                  pl.BlockSpec((None, d, tf), lambda i, e, f, nb: (e, 0, f)),
                  pl.BlockSpec((None, d, tf), lambda i, e, f, nb: (e, 0, f)),
                  pl.BlockSpec((None, tf, d), lambda i, e, f, nb: (e, f, 0))],
        out_specs=pl.BlockSpec((tm, d), lambda i, e, f, nb: (i, 0)),
        scratch_shapes=[pltpu.VMEM((tm, d), BF16), pltpu.VMEM((tm, d), F32),
                        pltpu.VMEM((tm, 1), F32), pltpu.VMEM((tm, 1), F32)],
    )
    return pl.pallas_call(
        functools.partial(_moe_kernel, tm),
        grid_spec=grid_spec,
        out_shape=jax.ShapeDtypeStruct((n_tok, d), F32),
        compiler_params=_cparams(("parallel", "arbitrary", "arbitrary")), name="moe",
    )(nblk, xb, rank, gate, rankt, wg, wu, wd)


def _resid_ln_kernel(alpha, x_ref, y_ref, g_ref, b_ref, o_ref):
    o_ref[...] = _layernorm(alpha * x_ref[...] + y_ref[...], g_ref[...], b_ref[...])


def _resid_ln_call(x, y, g, b, alpha):
    n, d = x.shape
    tm = _pick_tile(n, (1024, 512, 256, 128, 64, 32))
    row = lambda i: (i, 0)
    const = lambda i: (0, 0)
    return pl.pallas_call(
        functools.partial(_resid_ln_kernel, alpha),
        grid=(n // tm,),
        in_specs=[pl.BlockSpec((tm, d), row), pl.BlockSpec((tm, d), row),
                  pl.BlockSpec((1, d), const), pl.BlockSpec((1, d), const)],
        out_specs=pl.BlockSpec((tm, d), row),
        out_shape=jax.ShapeDtypeStruct((n, d), F32),
        compiler_params=_cparams(("parallel",)), name="resid_ln",
    )(x, y, g, b)


def _moe_layer(x, w_router, wg, wu, wd, g, b, alpha):
    n, d = x.shape
    n_exp = w_router.shape[1]
    tm = 2048 if n >= 2048 else _pick_tile(n, (1024, 512, 256, 128))
    wr_pad = jnp.zeros((d, LANES), F32).at[:, :n_exp].set(w_router)
    xb, rank, gate, rankt, cnt = _route_call(x, wr_pad, n_exp, tm)
    nblk = ((cnt[:, 0, :n_exp] + (MOE_ROWS - 1)) // MOE_ROWS).reshape(-1)
    y = _moe_call(n, xb, rank, gate, rankt, nblk, wg, wu, wd, tm)
    return _resid_ln_call(x, y, g, b, alpha)


def _swap_halves(w):
    half = w.shape[-1] // 2
    return jnp.concatenate([w[..., half:], w[..., :half]], axis=-1)


def _rope_tables(pos, scale):
    inv = ROPE_THETA ** (-jnp.arange(QK_ROPE // 2, dtype=F32) * (2.0 / QK_ROPE))
    ang = pos.astype(F32)[:, None] * inv[None, :]
    cos, sin = jnp.cos(ang), jnp.sin(ang)
    n = pos.shape[0]
    pad = jnp.zeros((n, HEAD_PAD - QK_NOPE - QK_ROPE), F32)
    ct = jnp.concatenate([jnp.ones((n, QK_NOPE), F32), cos, cos, pad], axis=1) * scale
    st = jnp.concatenate([jnp.zeros((n, QK_NOPE), F32), -sin, sin, pad], axis=1) * scale
    return ct, st


def kernel(x_prompt, x_sample, cache_ckv, cache_kr, state_conv, ln_in_g, ln_in_b, w_in, b_in, g_qnorm, w_uq, g_kvnorm, w_uk, w_uv, w_o_attn, conv_w, conv_b, conv_ln_g, conv_ln_b, w_conv_out, w_out, ln1_g, ln1_b, ln2_g, ln2_b, w_ffn_gate, w_ffn_up, w_ffn_down, w_router, w_exp_gate, w_exp_up, w_exp_down):
    batch, seq, d = x_prompt.shape
    n_str, t_new, _ = x_sample.shape
    depth, _, past_len, kv_rank = cache_ckv.shape
    q_rank = g_qnorm.shape[1]
    c_conv = conv_w.shape[2]
    conv_width = conv_w.shape[1]
    halo = 32
    assert conv_width - 1 <= halo and t_new >= conv_width - 1 and t_new % 16 == 0
    assert w_uq.shape[2] == N_HEADS * (QK_NOPE + QK_ROPE) and cache_kr.shape[3] == QK_ROPE
    alpha = (2 * depth) ** 0.25
    n_p = batch * seq
    n_s = n_str * t_new
    n = n_p + n_s

    o1, o2, o3 = q_rank, q_rank + kv_rank, q_rank + kv_rank + QK_ROPE
    o4 = o3 + 2 * c_conv

    def place_kr(w):
        lead = jnp.zeros(w.shape[:-1] + (QK_NOPE,), F32)
        tail = jnp.zeros(w.shape[:-1] + (HEAD_PAD - QK_NOPE - QK_ROPE,), F32)
        return jnp.concatenate([lead, w, tail], axis=-1)

    def split_a(w):
        kr = w[..., o2:o3]
        return jnp.concatenate([w[..., :o2], place_kr(kr), place_kr(_swap_halves(kr))], axis=-1)

    wa = split_a(w_in).astype(BF16)
    ba = split_a(b_in)[:, None, :]
    wb = w_in[..., o3:].astype(BF16)
    bb = b_in[:, None, o3:]
    wq = w_uq.reshape(depth, q_rank, N_HEADS, QK_NOPE + QK_ROPE)
    wq_n, wq_r = wq[..., :QK_NOPE], wq[..., QK_NOPE:]
    z_tail = jnp.zeros(wq_r.shape[:-1] + (HEAD_PAD - QK_NOPE - QK_ROPE,), F32)
    wqa = jnp.concatenate([wq_n, wq_r, z_tail], -1).reshape(depth, q_rank, -1).astype(BF16)
    wqb = jnp.concatenate([jnp.zeros_like(wq_n), _swap_halves(wq_r), z_tail], -1)
    wqb = wqb.reshape(depth, q_rank, -1).astype(BF16)
    wuk = jnp.concatenate([w_uk, jnp.zeros(w_uk.shape[:-1] + (HEAD_PAD - QK_NOPE,), F32)], -1)
    wuk = wuk.reshape(depth, kv_rank, -1).astype(BF16)
    wuv = w_uv.reshape(depth, kv_rank, -1).astype(BF16)
    wuvt = jnp.transpose(w_uv, (0, 2, 3, 1))
    wuvt = jnp.concatenate([wuvt, jnp.zeros((depth, N_HEADS, VT_ROWS - V_HEAD, kv_rank), F32)], 2)
    wuvt = wuvt.reshape(depth, N_HEADS * VT_ROWS, kv_rank).astype(BF16)
    place = jnp.tile(place_kr(jnp.eye(QK_ROPE, dtype=F32)), (1, N_HEADS)).astype(BF16)
    woa = w_o_attn.astype(BF16)
    wco = w_conv_out.astype(BF16)
    wout = w_out.astype(BF16)
    wfg, wfu, wfd = w_ffn_gate.astype(BF16), w_ffn_up.astype(BF16), w_ffn_down.astype(BF16)
    weg, weu, wed = w_exp_gate.astype(BF16), w_exp_up.astype(BF16), w_exp_down.astype(BF16)

    pos = jnp.concatenate([jnp.tile(jnp.arange(seq, dtype=jnp.int32), batch),
                           jnp.tile(past_len + jnp.arange(t_new, dtype=jnp.int32), n_str)])
    ctq, stq = _rope_tables(pos, ATTN_SCALE * LOG2E)
    ck, sk = _rope_tables(pos, 1.0)
    ck = ck.at[:, :QK_NOPE].set(0.0)
    tabs = dict(ctq=ctq, stq=stq, ck=ck, sk=sk)

    kpast, vpast = _kvpast_call(cache_ckv.reshape(depth, n_str * past_len, kv_rank),
                                cache_kr.reshape(depth, n_str * past_len, QK_ROPE), wuk, wuv, place)
    kpast = kpast.reshape(depth * n_str, past_len, -1)
    vpast = vpast.reshape(depth * n_str, past_len, -1)

    x = _ln_call(jnp.concatenate([x_prompt.reshape(n_p, d), x_sample.reshape(n_s, d)], axis=0),
                 ln_in_g, ln_in_b)
    tc = _pick_tile(seq, (512, 256, 128, 64))
    ckv_l, kr_l, conv_p_l, conv_s_l = [], [], [], []
    for l in range(depth):
        wl = dict(wa=wa[l], ba=ba[l], wb=wb[l], bb=bb[l], gq=g_qnorm[l][None], wqa=wqa[l], wqb=wqb[l],
                  gkv=g_kvnorm[l][None], wuk=wuk[l], wuv=wuv[l], wuvt=wuvt[l], woa=woa[l], wco=wco[l], wout=wout[l],
                  ln1g=ln1_g[l][None], ln1b=ln1_b[l][None])
        q, k, v, vt, ckv, kr, u, gates = _inproj_call(x, wl, tabs, (q_rank, kv_rank, c_conv))
        attn_p = _attn_prompt_call(q, k, vt, batch, seq)
        attn_s = _attn_sample_call(q, k, v, kpast, vpast, l, n_p, n_str, t_new, past_len)
        u_p = u[:n_p].reshape(batch, seq // tc, tc, c_conv)
        halo_p = jnp.concatenate([jnp.zeros((batch, 1, halo, c_conv), F32),
                                  u_p[:, :-1, tc - halo:, :]], axis=1).reshape(-1, c_conv)
        u_s = u[n_p:]
        halo_s = jnp.concatenate([jnp.zeros((n_str, halo - (conv_width - 1), c_conv), F32),
                                  state_conv[l]], axis=1).reshape(-1, c_conv)
        conv_args = (conv_w[l], conv_b[l], conv_ln_g[l], conv_ln_b[l])
        cact_p = _conv_call(u[:n_p], halo_p, tc, *conv_args)
        cact_s = _conv_call(u_s, halo_s, t_new, *conv_args)
        x = _outproj_call(x, jnp.concatenate([attn_p, attn_s], axis=0),
                          jnp.concatenate([cact_p, cact_s], axis=0), gates, wl, alpha)
        i = l // 2
        if l % 2 == 0:
            x = _ffn_call(x, wfg[i], wfu[i], wfd[i], ln2_g[l][None], ln2_b[l][None], alpha)
        else:
            x = _moe_layer(x, w_router[i], weg[i], weu[i], wed[i], ln2_g[l][None], ln2_b[l][None],
                           alpha)
        ckv_l.append(ckv)
        kr_l.append(kr[:, QK_NOPE:QK_NOPE + QK_ROPE])
        conv_p_l.append(u[:n_p].reshape(batch, seq, c_conv)[:, seq - (conv_width - 1):, :])
        conv_s_l.append(u_s.reshape(n_str, t_new, c_conv)[:, t_new - (conv_width - 1):, :])
    ckv_all = jnp.stack(ckv_l)
    kr_all = jnp.stack(kr_l)
    return (x[:n_p].reshape(batch, seq, d), x[n_p:].reshape(n_str, t_new, d),
            ckv_all[:, :n_p].reshape(depth, batch, seq, kv_rank),
            kr_all[:, :n_p].reshape(depth, batch, seq, QK_ROPE),
            jnp.stack(conv_p_l),
            ckv_all[:, n_p:].reshape(depth, n_str, t_new, kv_rank),
            kr_all[:, n_p:].reshape(depth, n_str, t_new, QK_ROPE),
            jnp.stack(conv_s_l))
```

```python
import functools
import math

import numpy as np
import jax
import jax.numpy as jnp
from jax import lax
from jax.experimental import pallas as pl
from jax.experimental.pallas import tpu as pltpu

CHUNK = 64
N_HEADS = 8
QK_NOPE = 64
QK_ROPE = 32
V_HEAD = 64
ROPE_THETA = 10000.0
TOP_K = 2
LN_EPS = 1e-5
NEG_INF = -1e30
ATTN_SCALE = (QK_NOPE + QK_ROPE) ** -0.5

LANES = 128
SUBLANES = 8
HEAD_PAD = 128
HEADS_PER_STEP = 2
VT_ROWS = V_HEAD + 16
VT_BLOCK = 256
LOG2E = math.log2(math.e)
MOE_ROWS = 256
VMEM_LIMIT = 56 * 1024 * 1024

BF16 = jnp.bfloat16
F32 = jnp.float32


def _cparams(sem):
    return pltpu.CompilerParams(dimension_semantics=sem, vmem_limit_bytes=VMEM_LIMIT)


def _pick_tile(n, candidates):
    for c in candidates:
        if n % c == 0:
            return c
    raise ValueError(f"no tile in {candidates} divides {n}")


def _dot(a, b):
    return jnp.dot(a, b, preferred_element_type=F32)


def _dot_nt(a, b):
    return lax.dot_general(a, b, (((1,), (1,)), ((), ())), preferred_element_type=F32)


def _layernorm(x, g, b):
    mu = jnp.mean(x, axis=-1, keepdims=True)
    xc = x - mu
    var = jnp.mean(xc * xc, axis=-1, keepdims=True)
    return xc * lax.rsqrt(var + LN_EPS) * g + b


def _rmsnorm(x, g):
    return x * lax.rsqrt(jnp.mean(x * x, axis=-1, keepdims=True) + LN_EPS) * g


def _silu(x):
    return x * jax.nn.sigmoid(x)


def _ln_kernel(x_ref, g_ref, b_ref, o_ref):
    o_ref[...] = _layernorm(x_ref[...], g_ref[...], b_ref[...])


def _ln_call(x, g, b):
    n, d = x.shape
    tm = _pick_tile(n, (1024, 512, 256, 128, 64, 32))
    return pl.pallas_call(
        _ln_kernel,
        grid=(n // tm,),
        in_specs=[pl.BlockSpec((tm, d), lambda i: (i, 0)),
                  pl.BlockSpec((1, d), lambda i: (0, 0)),
                  pl.BlockSpec((1, d), lambda i: (0, 0))],
        out_specs=pl.BlockSpec((tm, d), lambda i: (i, 0)),
        out_shape=jax.ShapeDtypeStruct((n, d), F32),
        compiler_params=_cparams(("parallel",)),
        name="ln_in",
    )(x, g.reshape(1, d), b.reshape(1, d))


def _inproj_kernel(q_rank, kv_rank, c_conv,
                   x_ref, wa_ref, ba_ref, wb_ref, bb_ref, gq_ref, wqa_ref, wqb_ref, gkv_ref,
                   wuk_ref, wuv_ref, wuvt_ref, ctq_ref, stq_ref, ck_ref, sk_ref,
                   q_ref, k_ref, v_ref, vt_ref, ckv_ref, kr_ref, u_ref, gate_ref):
    xb = x_ref[...].astype(BF16)
    pa = _dot(xb, wa_ref[...]) + ba_ref[...]
    o1, o2, o3 = q_rank, q_rank + kv_rank, q_rank + kv_rank + HEAD_PAD
    cqn = _rmsnorm(pa[:, :o1], gq_ref[...]).astype(BF16)
    ct = jnp.concatenate([ctq_ref[...]] * N_HEADS, axis=1)
    st = jnp.concatenate([stq_ref[...]] * N_HEADS, axis=1)
    q = _dot(cqn, wqa_ref[...]) * ct + _dot(cqn, wqb_ref[...]) * st
    q_ref[...] = q.astype(BF16)
    ckv = _rmsnorm(pa[:, o1:o2], gkv_ref[...])
    ckv_ref[...] = ckv
    kr = pa[:, o2:o3] * ck_ref[...] + pa[:, o3:] * sk_ref[...]
    kr_ref[...] = kr
    ckvb = ckv.astype(BF16)
    k = _dot(ckvb, wuk_ref[...]) + jnp.concatenate([kr] * N_HEADS, axis=1)
    k_ref[...] = k.astype(BF16)
    v_ref[...] = _dot(ckvb, wuv_ref[...]).astype(BF16)
    vt = _dot_nt(wuvt_ref[...], ckvb)
    ones_row = lax.broadcasted_iota(jnp.int32, vt.shape, 0) % VT_ROWS >= V_HEAD
    vt_ref[...] = jnp.where(ones_row, 1.0, vt).astype(BF16)
    pu = _dot(xb, wb_ref[:, :2 * c_conv]) + bb_ref[:, :2 * c_conv]
    u_ref[...] = pu[:, :c_conv] * jax.nn.sigmoid(pu[:, c_conv:])
    pg = _dot(xb, wb_ref[:, 2 * c_conv:]) + bb_ref[:, 2 * c_conv:]
    gate_ref[...] = jax.nn.sigmoid(pg).astype(BF16)


def _inproj_call(x, wl, tabs, dims):
    n, d = x.shape
    q_rank, kv_rank, c_conv = dims
    tm = VT_BLOCK
    assert n % tm == 0
    na = wl["wa"].shape[1]
    nb = wl["wb"].shape[1]
    hq = N_HEADS * HEAD_PAD
    hv = N_HEADS * V_HEAD
    hvt = N_HEADS * VT_ROWS
    const = lambda i: (0, 0)
    row = lambda i: (i, 0)
    in_specs = [
        pl.BlockSpec((tm, d), row),
        pl.BlockSpec((d, na), const), pl.BlockSpec((1, na), const),
        pl.BlockSpec((d, nb), const), pl.BlockSpec((1, nb), const),
        pl.BlockSpec((1, q_rank), const),
        pl.BlockSpec((q_rank, hq), const), pl.BlockSpec((q_rank, hq), const),
        pl.BlockSpec((1, kv_rank), const),
        pl.BlockSpec((kv_rank, hq), const), pl.BlockSpec((kv_rank, hv), const),
        pl.BlockSpec((hvt, kv_rank), const),
        pl.BlockSpec((tm, HEAD_PAD), row), pl.BlockSpec((tm, HEAD_PAD), row),
        pl.BlockSpec((tm, HEAD_PAD), row), pl.BlockSpec((tm, HEAD_PAD), row),
    ]
    out_specs = [
        pl.BlockSpec((tm, hq), row), pl.BlockSpec((tm, hq), row), pl.BlockSpec((tm, hv), row),
        pl.BlockSpec((None, hvt, tm), lambda i: (i, 0, 0)),
        pl.BlockSpec((tm, kv_rank), row), pl.BlockSpec((tm, HEAD_PAD), row),
        pl.BlockSpec((tm, c_conv), row), pl.BlockSpec((tm, 2 * d), row),
    ]
    out_shape = [
        jax.ShapeDtypeStruct((n, hq), BF16), jax.ShapeDtypeStruct((n, hq), BF16),
        jax.ShapeDtypeStruct((n, hv), BF16), jax.ShapeDtypeStruct((n // tm, hvt, tm), BF16),
        jax.ShapeDtypeStruct((n, kv_rank), F32), jax.ShapeDtypeStruct((n, HEAD_PAD), F32),
        jax.ShapeDtypeStruct((n, c_conv), F32), jax.ShapeDtypeStruct((n, 2 * d), BF16),
    ]
    return pl.pallas_call(
        functools.partial(_inproj_kernel, q_rank, kv_rank, c_conv),
        grid=(n // tm,), in_specs=in_specs, out_specs=out_specs, out_shape=out_shape,
        compiler_params=_cparams(("parallel",)), name="inproj",
    )(x, wl["wa"], wl["ba"], wl["wb"], wl["bb"], wl["gq"], wl["wqa"], wl["wqb"], wl["gkv"],
      wl["wuk"], wl["wuv"], wl["wuvt"], tabs["ctq"], tabs["stq"], tabs["ck"], tabs["sk"])


def _kvpast_kernel(ckv_ref, kr_ref, wuk_ref, wuv_ref, place_ref, k_ref, v_ref):
    ckvb = ckv_ref[...].astype(BF16)
    k = _dot(ckvb, wuk_ref[...]) + _dot(kr_ref[...].astype(BF16), place_ref[...])
    k_ref[...] = k.astype(BF16)
    v_ref[...] = _dot(ckvb, wuv_ref[...]).astype(BF16)


def _kvpast_call(ckv, kr, wuk, wuv, place):
    depth, rows, kv_rank = ckv.shape
    rope = kr.shape[-1]
    tm = _pick_tile(rows, (1024, 512, 256, 128, 64, 32))
    tpl = rows // tm
    hq = N_HEADS * HEAD_PAD
    hv = N_HEADS * V_HEAD
    return pl.pallas_call(
        _kvpast_kernel,
        grid=(depth, tpl),
        in_specs=[pl.BlockSpec((None, tm, kv_rank), lambda l, i: (l, i, 0)),
                  pl.BlockSpec((None, tm, rope), lambda l, i: (l, i, 0)),
                  pl.BlockSpec((None, kv_rank, hq), lambda l, i: (l, 0, 0)),
                  pl.BlockSpec((None, kv_rank, hv), lambda l, i: (l, 0, 0)),
                  pl.BlockSpec((rope, hq), lambda l, i: (0, 0))],
        out_specs=[pl.BlockSpec((None, tm, hq), lambda l, i: (l, i, 0)),
                   pl.BlockSpec((None, tm, hv), lambda l, i: (l, i, 0))],
        out_shape=[jax.ShapeDtypeStruct((depth, rows, hq), BF16),
                   jax.ShapeDtypeStruct((depth, rows, hv), BF16)],
        compiler_params=_cparams(("parallel", "parallel")), name="kvpast",
    )(ckv, kr, wuk, wuv, place)


def _attn_prompt_kernel(tq, q_ref, k_ref, vt_ref, bias_ref, o_ref, sa_ref, sb_ref):
    qi = pl.program_id(2)
    heads = range(HEADS_PER_STEP)
    nvb = tq // VT_BLOCK

    def scores(j, s_ref):
        start = pl.multiple_of(j * tq, tq)
        for h in heads:
            s_ref[h] = _dot_nt(k_ref[pl.ds(start, tq), h * HEAD_PAD:(h + 1) * HEAD_PAD],
                               q_ref[:, h * HEAD_PAD:(h + 1) * HEAD_PAD])

    def softmax_pv(j, s_ref, state, masked):
        out = []
        for h in heads:
            s = s_ref[h]
            m_old, acc = state[h]
            if masked:
                s = s + bias_ref[...]
            m_new = jnp.maximum(m_old, jnp.max(s, axis=0, keepdims=True))
            p = jnp.exp2(s - m_new).astype(BF16)
            pv = None
            for b in range(nvb):
                vth = vt_ref[j * nvb + b, h * VT_ROWS:(h + 1) * VT_ROWS, :]
                d = _dot(vth, p[b * VT_BLOCK:(b + 1) * VT_BLOCK, :])
                pv = d if pv is None else pv + d
            out.append((m_new, jnp.exp2(m_old - m_new) * acc + pv))
        return tuple(out)

    def pair(t, state):
        j = 2 * t
        scores(j + 1, sb_ref)
        state = softmax_pv(j, sa_ref, state, False)
        scores(j + 2, sa_ref)
        return softmax_pv(j + 1, sb_ref, state, False)

    def odd_tail(state):
        scores(qi, sb_ref)
        state = softmax_pv(qi - 1, sa_ref, state, False)
        return softmax_pv(qi, sb_ref, state, True)

    def even_tail(state):
        return softmax_pv(qi, sa_ref, state, True)

    init = tuple((jnp.full((1, tq), NEG_INF, F32), jnp.zeros((VT_ROWS, tq), F32)) for _ in heads)
    scores(0, sa_ref)
    state = lax.fori_loop(0, qi // 2, pair, init)
    state = lax.cond(qi % 2 == 1, odd_tail, even_tail, state)
    rows = [acc[:V_HEAD, :] / acc[V_HEAD:V_HEAD + 1, :] for _, acc in state]
    o_ref[...] = jnp.transpose(jnp.concatenate(rows, axis=0)).astype(BF16)


def _attn_prompt_call(q, k, vt, batch, seq):
    tq = _pick_tile(seq, (512, 256))
    nq = seq // tq
    hp = N_HEADS // HEADS_PER_STEP
    qw = HEADS_PER_STEP * HEAD_PAD
    vw = HEADS_PER_STEP * V_HEAD
    vr = HEADS_PER_STEP * VT_ROWS
    nvb = seq // VT_BLOCK
    chunk = np.arange(tq) // CHUNK
    bias = np.where(chunk[:, None] <= chunk[None, :], 0.0, NEG_INF).astype(np.float32)
    return pl.pallas_call(
        functools.partial(_attn_prompt_kernel, tq),
        grid=(batch, hp, nq),
        in_specs=[pl.BlockSpec((tq, qw), lambda b, h, i: (b * nq + i, h)),
                  pl.BlockSpec((seq, qw), lambda b, h, i: (b, h)),
                  pl.BlockSpec((nvb, vr, VT_BLOCK), lambda b, h, i: (b, h, 0)),
                  pl.BlockSpec((tq, tq), lambda b, h, i: (0, 0))],
        out_specs=pl.BlockSpec((tq, vw), lambda b, h, i: (b * nq + i, h)),
        out_shape=jax.ShapeDtypeStruct((batch * seq, N_HEADS * V_HEAD), BF16),
        scratch_shapes=[pltpu.VMEM((HEADS_PER_STEP, tq, tq), F32),
                        pltpu.VMEM((HEADS_PER_STEP, tq, tq), F32)],
        compiler_params=_cparams(("parallel", "parallel", "arbitrary")), name="attn_prompt",
    )(q, k, vt, jnp.asarray(bias))


def _attn_sample_kernel(mask_past, mask_new, past_len,
                        q_ref, kn_ref, vn_ref, kp_ref, vp_ref, o_ref):
    vp = vp_ref[...]
    vn = vn_ref[...]
    outs = []
    for h in range(HEADS_PER_STEP):
        sl = slice(h * HEAD_PAD, (h + 1) * HEAD_PAD)
        qh = q_ref[:, sl]
        sp = _dot_nt(qh, kp_ref[:, sl])
        sn = _dot_nt(qh, kn_ref[:, sl])
        if mask_past:
            qc = (lax.broadcasted_iota(jnp.int32, sp.shape, 0) + past_len) // CHUNK
            kc = lax.broadcasted_iota(jnp.int32, sp.shape, 1) // CHUNK
            sp = jnp.where(kc <= qc, sp, NEG_INF)
        if mask_new:
            qc = (lax.broadcasted_iota(jnp.int32, sn.shape, 0) + past_len) // CHUNK
            kc = (lax.broadcasted_iota(jnp.int32, sn.shape, 1) + past_len) // CHUNK
            sn = jnp.where(kc <= qc, sn, NEG_INF)
        m = jnp.maximum(jnp.max(sp, axis=1, keepdims=True), jnp.max(sn, axis=1, keepdims=True))
        pp = jnp.exp2(sp - m)
        pn = jnp.exp2(sn - m)
        l = jnp.sum(pp, axis=1, keepdims=True) + jnp.sum(pn, axis=1, keepdims=True)
        outs.append((_dot(pp.astype(BF16), vp) + _dot(pn.astype(BF16), vn)) / l)
    lane = lax.broadcasted_iota(jnp.int32, outs[0].shape, 1)
    o_ref[...] = jnp.where(lane < V_HEAD, outs[0], outs[1]).astype(BF16)


def _attn_sample_call(q, k, v, kpast, vpast, layer, row0, n_streams, t_new, past_len):
    hp = N_HEADS // HEADS_PER_STEP
    qw = HEADS_PER_STEP * HEAD_PAD
    vw = HEADS_PER_STEP * V_HEAD
    blk0 = row0 // t_new
    q_pos = past_len + np.arange(t_new)
    mask_past = not bool(np.all((np.arange(past_len) // CHUNK)[None, :] <= (q_pos // CHUNK)[:, None]))
    mask_new = not bool(np.all((q_pos // CHUNK)[None, :] <= (q_pos // CHUNK)[:, None]))
    new_spec = lambda w: pl.BlockSpec((t_new, w), lambda s, h: (blk0 + s, h))
    return pl.pallas_call(
        functools.partial(_attn_sample_kernel, mask_past, mask_new, past_len),
        grid=(n_streams, hp),
        in_specs=[new_spec(qw), new_spec(qw), new_spec(vw),
                  pl.BlockSpec((None, past_len, qw), lambda s, h: (layer * n_streams + s, 0, h)),
                  pl.BlockSpec((None, past_len, vw), lambda s, h: (layer * n_streams + s, 0, h))],
        out_specs=pl.BlockSpec((t_new, vw), lambda s, h: (s, h)),
        out_shape=jax.ShapeDtypeStruct((n_streams * t_new, N_HEADS * V_HEAD), BF16),
        compiler_params=_cparams(("parallel", "parallel")), name="attn_sample",
    )(q, k, v, kpast, vpast)


def _conv_kernel(tc, halo, conv_w, rc, u_ref, h_ref, w_ref, b_ref, g_ref, beta_ref, o_ref, sh_ref):
    sh_ref[0, 0:halo, :] = h_ref[...]
    sh_ref[0, halo:halo + tc, :] = u_ref[...]
    span = halo + tc - SUBLANES
    for r in range(1, SUBLANES):
        sh_ref[r, 0:span, :] = sh_ref[0, r:r + span, :]
    lead = halo - (conv_w - 1)
    for c in range(tc // rc):
        acc = jnp.broadcast_to(b_ref[...], (rc, u_ref.shape[1]))
        for j in range(conv_w):
            r = (lead + j) % SUBLANES
            r0 = c * rc + lead + j - r
            acc = acc + w_ref[j:j + 1, :] * sh_ref[r, r0:r0 + rc, :]
        y = _layernorm(acc, g_ref[...], beta_ref[...])
        o_ref[c * rc:(c + 1) * rc, :] = _silu(y).astype(BF16)


def _conv_call(u, row0, n_tiles, halo_rows, tc, w, b, g, beta):
    c = u.shape[1]
    blk0 = row0 // tc
    halo = halo_rows.shape[0] // n_tiles
    conv_w = w.shape[0]
    assert halo % SUBLANES == 0 and conv_w - 1 <= halo
    rc = min(tc, 64)
    const = lambda i: (0, 0)
    return pl.pallas_call(
        functools.partial(_conv_kernel, tc, halo, conv_w, rc),
        grid=(n_tiles,),
        in_specs=[pl.BlockSpec((tc, c), lambda i: (blk0 + i, 0)),
                  pl.BlockSpec((halo, c), lambda i: (i, 0)),
                  pl.BlockSpec((conv_w, c), const), pl.BlockSpec((1, c), const),
                  pl.BlockSpec((1, c), const), pl.BlockSpec((1, c), const)],
        out_specs=pl.BlockSpec((tc, c), lambda i: (i, 0)),
        out_shape=jax.ShapeDtypeStruct((n_tiles * tc, c), BF16),
        scratch_shapes=[pltpu.VMEM((SUBLANES, halo + tc, c), F32)],
        compiler_params=_cparams(("parallel",)), name="conv",
    )(u, halo_rows, w, b.reshape(1, c), g.reshape(1, c), beta.reshape(1, c))


def _outproj_kernel(alpha, d, p_tiles, x_ref, ap_ref, as_ref, cp_ref, cs_ref, gate_ref,
                    woa_ref, wco_ref, wout_ref, g_ref, b_ref, o_ref):
    is_prompt = pl.program_id(0) < p_tiles
    br_a = _dot(jnp.where(is_prompt, ap_ref[...], as_ref[...]), woa_ref[...])
    br_c = _dot(jnp.where(is_prompt, cp_ref[...], cs_ref[...]), wco_ref[...])
    merged = gate_ref[:, :d].astype(F32) * br_a + gate_ref[:, d:].astype(F32) * br_c
    mix = _dot(merged.astype(BF16), wout_ref[...])
    o_ref[...] = _layernorm(alpha * x_ref[...] + mix, g_ref[...], b_ref[...])


def _outproj_call(x, attn_p, attn_s, cact_p, cact_s, gates, wl, alpha):
    n, d = x.shape
    n_p, n_s = attn_p.shape[0], attn_s.shape[0]
    tm = _pick_tile(math.gcd(n_p, n_s), (512, 256, 128, 64, 32))
    p_tiles = n_p // tm
    const = lambda i: (0, 0)
    row = lambda i: (i, 0)
    prow = lambda i: (jnp.minimum(i, p_tiles - 1), 0)
    srow = lambda i: (jnp.maximum(i - p_tiles, 0), 0)
    ha, hc = attn_p.shape[1], cact_p.shape[1]
    return pl.pallas_call(
        functools.partial(_outproj_kernel, alpha, d, p_tiles),
        grid=(n // tm,),
        in_specs=[pl.BlockSpec((tm, d), row), pl.BlockSpec((tm, ha), prow),
                  pl.BlockSpec((tm, ha), srow), pl.BlockSpec((tm, hc), prow),
                  pl.BlockSpec((tm, hc), srow), pl.BlockSpec((tm, 2 * d), row),
                  pl.BlockSpec((ha, d), const), pl.BlockSpec((hc, d), const),
                  pl.BlockSpec((d, d), const), pl.BlockSpec((1, d), const),
                  pl.BlockSpec((1, d), const)],
        out_specs=pl.BlockSpec((tm, d), row),
        out_shape=jax.ShapeDtypeStruct((n, d), F32),
        compiler_params=_cparams(("parallel",)), name="outproj",
    )(x, attn_p, attn_s, cact_p, cact_s, gates, wl["woa"], wl["wco"], wl["wout"], wl["ln1g"],
      wl["ln1b"])


def _ffn_kernel(alpha, x_ref, wg_ref, wu_ref, wd_ref, g_ref, b_ref, o_ref, acc_ref):
    f = pl.program_id(1)

    @pl.when(f == 0)
    def _():
        acc_ref[...] = jnp.zeros(acc_ref.shape, F32)

    xb = x_ref[...].astype(BF16)
    h = _silu(_dot(xb, wg_ref[...])) * _dot(xb, wu_ref[...])
    acc_ref[...] += _dot(h.astype(BF16), wd_ref[...])

    @pl.when(f == pl.num_programs(1) - 1)
    def _():
        o_ref[...] = _layernorm(alpha * x_ref[...] + acc_ref[...], g_ref[...], b_ref[...])


def _ffn_call(x, wg, wu, wd, g, b, alpha):
    n, d = x.shape
    dff = wg.shape[1]
    tm = _pick_tile(n, (512, 256, 128, 64, 32))
    tf = _pick_tile(dff, (1408, 1024, 512, 256, 128))
    return pl.pallas_call(
        functools.partial(_ffn_kernel, alpha),
        grid=(n // tm, dff // tf),
        in_specs=[pl.BlockSpec((tm, d), lambda i, f: (i, 0)),
                  pl.BlockSpec((d, tf), lambda i, f: (0, f)),
                  pl.BlockSpec((d, tf), lambda i, f: (0, f)),
                  pl.BlockSpec((tf, d), lambda i, f: (f, 0)),
                  pl.BlockSpec((1, d), lambda i, f: (0, 0)),
                  pl.BlockSpec((1, d), lambda i, f: (0, 0))],
        out_specs=pl.BlockSpec((tm, d), lambda i, f: (i, 0)),
        out_shape=jax.ShapeDtypeStruct((n, d), F32),
        scratch_shapes=[pltpu.VMEM((tm, d), F32)],
        compiler_params=_cparams(("parallel", "arbitrary")), name="ffn",
    )(x, wg, wu, wd, g, b)


def _route_kernel(n_tok, n_exp, tm, x_ref, wr_ref, xb_ref, rank_ref, gate_ref, rankt_ref, cnt_ref):
    i = pl.program_id(0)
    x = x_ref[...]
    row = lax.broadcasted_iota(jnp.int32, (tm, 1), 0) + i * tm
    valid = row < n_tok
    xb_ref[...] = jnp.where(valid, x, 0.0).astype(BF16)
    logits = jnp.dot(x, wr_ref[...], preferred_element_type=F32, precision=lax.Precision.HIGHEST)
    lane = lax.broadcasted_iota(jnp.int32, (tm, LANES), 1)
    logits = jnp.where(lane < n_exp, logits, -jnp.inf)
    m1 = jnp.max(logits, axis=1, keepdims=True)
    i1 = jnp.min(jnp.where(logits == m1, lane, LANES), axis=1, keepdims=True)
    rest = jnp.where(lane == i1, -jnp.inf, logits)
    m2 = jnp.max(rest, axis=1, keepdims=True)
    i2 = jnp.min(jnp.where(rest == m2, lane, LANES), axis=1, keepdims=True)
    e2 = jnp.exp(m2 - m1)
    g1 = 1.0 / (1.0 + e2)
    g2 = e2 / (1.0 + e2)
    sel1 = (lane == i1) & valid
    sel2 = (lane == i2) & valid
    gate_ref[...] = jnp.where(sel1, g1, 0.0) + jnp.where(sel2, g2, 0.0)
    sel = jnp.where(sel1 | sel2, 1.0, 0.0)
    cs = min(tm, 256)
    r = lax.broadcasted_iota(jnp.int32, (cs, cs), 0)
    c = lax.broadcasted_iota(jnp.int32, (cs, cs), 1)
    lower = jnp.where(c < r, 1.0, 0.0).astype(BF16)
    offs = jnp.zeros((1, LANES), F32)
    for ch in range(tm // cs):
        sc = sel[ch * cs:(ch + 1) * cs, :]
        rk = _dot(lower, sc.astype(BF16)) + offs
        rank_ref[ch * cs:(ch + 1) * cs, :] = jnp.where(sc > 0.0, rk, -1.0)
        offs = offs + jnp.sum(sc, axis=0, keepdims=True)
    cnt_ref[...] = offs.astype(jnp.int32)
    rankt_ref[...] = jnp.transpose(rank_ref[...])[:rankt_ref.shape[0], :]


def _route_call(x, wr_pad, n_exp, tm):
    n, d = x.shape
    nt = -(-n // tm)
    return pl.pallas_call(
        functools.partial(_route_kernel, n, n_exp, tm),
        grid=(nt,),
        in_specs=[pl.BlockSpec((tm, d), lambda i: (i, 0)),
                  pl.BlockSpec((d, LANES), lambda i: (0, 0))],
        out_specs=[pl.BlockSpec((tm, d), lambda i: (i, 0)),
                   pl.BlockSpec((tm, LANES), lambda i: (i, 0)),
                   pl.BlockSpec((tm, LANES), lambda i: (i, 0)),
                   pl.BlockSpec((n_exp, tm), lambda i: (0, i)),
                   pl.BlockSpec((None, 1, LANES), lambda i: (i, 0, 0))],
        out_shape=[jax.ShapeDtypeStruct((nt * tm, d), BF16),
                   jax.ShapeDtypeStruct((nt * tm, LANES), F32),
                   jax.ShapeDtypeStruct((nt * tm, LANES), F32),
                   jax.ShapeDtypeStruct((n_exp, nt * tm), F32),
                   jax.ShapeDtypeStruct((nt, 1, LANES), jnp.int32)],
        compiler_params=_cparams(("parallel",)), name="route",
    )(x, wr_pad)


def _moe_kernel(alpha, tm, n256_ref, t128_ref, x_ref, xb_ref, rank_ref, gate_ref, rankt_ref,
                wg_ref, wu_ref, wd_ref, g_ref, b_ref, o_ref, xe_ref, oe_ref, rcol_ref, gcol_ref):
    i = pl.program_id(0)
    e = pl.program_id(1)
    f = pl.program_id(2)
    n_e = pl.num_programs(1)
    nb = n256_ref[i * n_e + e]
    has_half = t128_ref[i * n_e + e] == 1
    half = MOE_ROWS // 2
    half0 = pl.multiple_of(nb * MOE_ROWS, half)

    def for_blocks(fn):
        def body(j, carry):
            fn(pl.multiple_of(j * MOE_ROWS, MOE_ROWS), MOE_ROWS)
            return carry
        lax.fori_loop(0, nb, body, 0)

        @pl.when(has_half)
        def _():
            fn(half0, half)

    @pl.when((e == 0) & (f == 0))
    def _():
        o_ref[...] = jnp.zeros(o_ref.shape, F32)

    @pl.when(f == 0)
    def _():
        lane = lax.broadcasted_iota(jnp.int32, (tm, LANES), 1)
        rcol_ref[...] = jnp.max(jnp.where(lane == e, rank_ref[...], -1.0), axis=1, keepdims=True)
        gcol_ref[...] = jnp.sum(jnp.where(lane == e, gate_ref[...], 0.0), axis=1, keepdims=True)

        def gather(r0, rows):
            rrow = rankt_ref[pl.ds(e, 1), :]
            slot = (lax.broadcasted_iota(jnp.int32, (rows, tm), 0) + r0).astype(F32)
            onehot = jnp.where(rrow == slot, 1.0, 0.0).astype(BF16)
            xe_ref[pl.ds(r0, rows), :] = _dot(onehot, xb_ref[...]).astype(BF16)

        for_blocks(gather)

    def expert(r0, rows):
        xe = xe_ref[pl.ds(r0, rows), :]
        h = _silu(_dot(xe, wg_ref[...])) * _dot(xe, wu_ref[...])
        o = _dot(h.astype(BF16), wd_ref[...])

        @pl.when(f == 0)
        def _():
            oe_ref[pl.ds(r0, rows), :] = o

        @pl.when(f != 0)
        def _():
            oe_ref[pl.ds(r0, rows), :] += o

    for_blocks(expert)

    @pl.when(f == pl.num_programs(2) - 1)
    def _():
        def scatter(r0, rows):
            ob = oe_ref[pl.ds(r0, rows), :].astype(BF16)
            sb = min(tm, 512)
            for t in range(tm // sb):
                ts = slice(t * sb, (t + 1) * sb)
                slot = (lax.broadcasted_iota(jnp.int32, (sb, rows), 1) + r0).astype(F32)
                onehot_t = jnp.where(rcol_ref[ts, :] == slot, 1.0, 0.0).astype(BF16)
                o_ref[ts, :] += gcol_ref[ts, :] * _dot(onehot_t, ob)

        for_blocks(scatter)

    @pl.when((e == n_e - 1) & (f == pl.num_programs(2) - 1))
    def _():
        rb = min(tm, 256)
        for t in range(tm // rb):
            ts = slice(t * rb, (t + 1) * rb)
            o_ref[ts, :] = _layernorm(alpha * x_ref[ts, :] + o_ref[ts, :], g_ref[...], b_ref[...])


def _moe_call(x, xb, rank, gate, rankt, n256, t128, wg, wu, wd, g, b, alpha, tm):
    n_tok, d = x.shape
    nt = xb.shape[0] // tm
    n_exp, _, dff = wg.shape
    tf = _pick_tile(dff, (896, 512, 256, 128))
    once = pl.Buffered(1)
    tile = lambda i, e, f, *_: (i, 0)
    const = lambda i, e, f, *_: (0, 0)
    grid_spec = pltpu.PrefetchScalarGridSpec(
        num_scalar_prefetch=2,
        grid=(nt, n_exp, dff // tf),
        in_specs=[pl.BlockSpec((tm, d), tile, pipeline_mode=once),
                  pl.BlockSpec((tm, d), tile, pipeline_mode=once),
                  pl.BlockSpec((tm, LANES), tile, pipeline_mode=once),
                  pl.BlockSpec((tm, LANES), tile, pipeline_mode=once),
                  pl.BlockSpec((n_exp, tm), lambda i, e, f, *_: (0, i), pipeline_mode=once),
                  pl.BlockSpec((None, d, tf), lambda i, e, f, *_: (e, 0, f)),
                  pl.BlockSpec((None, d, tf), lambda i, e, f, *_: (e, 0, f)),
                  pl.BlockSpec((None, tf, d), lambda i, e, f, *_: (e, f, 0)),
                  pl.BlockSpec((1, d), const), pl.BlockSpec((1, d), const)],
        out_specs=pl.BlockSpec((tm, d), tile, pipeline_mode=once),
        scratch_shapes=[pltpu.VMEM((tm, d), BF16), pltpu.VMEM((tm, d), F32),
                        pltpu.VMEM((tm, 1), F32), pltpu.VMEM((tm, 1), F32)],
    )
    return pl.pallas_call(
        functools.partial(_moe_kernel, alpha, tm),
        grid_spec=grid_spec,
        out_shape=jax.ShapeDtypeStruct((n_tok, d), F32),
        compiler_params=_cparams(("parallel", "arbitrary", "arbitrary")), name="moe",
    )(n256, t128, x, xb, rank, gate, rankt, wg, wu, wd, g, b)


def _moe_layer(x, w_router, wg, wu, wd, g, b, alpha):
    n, d = x.shape
    n_exp = w_router.shape[1]
    tm = 2048 if n >= 2048 else _pick_tile(n, (1024, 512, 256, 128))
    wr_pad = jnp.zeros((d, LANES), F32).at[:, :n_exp].set(w_router)
    xb, rank, gate, rankt, cnt = _route_call(x, wr_pad, n_exp, tm)
    cnt = cnt[:, 0, :n_exp].reshape(-1)
    rem = cnt % MOE_ROWS
    t128 = ((rem > 0) & (rem <= MOE_ROWS // 2)).astype(jnp.int32)
    n256 = cnt // MOE_ROWS + (rem > MOE_ROWS // 2).astype(jnp.int32)
    return _moe_call(x, xb, rank, gate, rankt, n256, t128, wg, wu, wd, g, b, alpha, tm)


def _swap_halves(w):
    half = w.shape[-1] // 2
    return jnp.concatenate([w[..., half:], w[..., :half]], axis=-1)


def _rope_tables(pos, scale):
    inv = ROPE_THETA ** (-jnp.arange(QK_ROPE // 2, dtype=F32) * (2.0 / QK_ROPE))
    ang = pos.astype(F32)[:, None] * inv[None, :]
    cos, sin = jnp.cos(ang), jnp.sin(ang)
    n = pos.shape[0]
    pad = jnp.zeros((n, HEAD_PAD - QK_NOPE - QK_ROPE), F32)
    ct = jnp.concatenate([jnp.ones((n, QK_NOPE), F32), cos, cos, pad], axis=1) * scale
    st = jnp.concatenate([jnp.zeros((n, QK_NOPE), F32), -sin, sin, pad], axis=1) * scale
    return ct, st


def kernel(x_prompt, x_sample, cache_ckv, cache_kr, state_conv, ln_in_g, ln_in_b, w_in, b_in, g_qnorm, w_uq, g_kvnorm, w_uk, w_uv, w_o_attn, conv_w, conv_b, conv_ln_g, conv_ln_b, w_conv_out, w_out, ln1_g, ln1_b, ln2_g, ln2_b, w_ffn_gate, w_ffn_up, w_ffn_down, w_router, w_exp_gate, w_exp_up, w_exp_down):
    batch, seq, d = x_prompt.shape
    n_str, t_new, _ = x_sample.shape
    depth, _, past_len, kv_rank = cache_ckv.shape
    q_rank = g_qnorm.shape[1]
    c_conv = conv_w.shape[2]
    conv_width = conv_w.shape[1]
    halo = 32
    assert conv_width - 1 <= halo and t_new >= conv_width - 1 and t_new % 16 == 0
    assert w_uq.shape[2] == N_HEADS * (QK_NOPE + QK_ROPE) and cache_kr.shape[3] == QK_ROPE
    alpha = (2 * depth) ** 0.25
    n_p = batch * seq
    n_s = n_str * t_new
    n = n_p + n_s

    o1, o2, o3 = q_rank, q_rank + kv_rank, q_rank + kv_rank + QK_ROPE
    o4 = o3 + 2 * c_conv

    def place_kr(w):
        lead = jnp.zeros(w.shape[:-1] + (QK_NOPE,), F32)
        tail = jnp.zeros(w.shape[:-1] + (HEAD_PAD - QK_NOPE - QK_ROPE,), F32)
        return jnp.concatenate([lead, w, tail], axis=-1)

    def split_a(w):
        kr = w[..., o2:o3]
        return jnp.concatenate([w[..., :o2], place_kr(kr), place_kr(_swap_halves(kr))], axis=-1)

    wa = split_a(w_in).astype(BF16)
    ba = split_a(b_in)[:, None, :]
    wb = w_in[..., o3:].astype(BF16)
    bb = b_in[:, None, o3:]
    wq = w_uq.reshape(depth, q_rank, N_HEADS, QK_NOPE + QK_ROPE)
    wq_n, wq_r = wq[..., :QK_NOPE], wq[..., QK_NOPE:]
    z_tail = jnp.zeros(wq_r.shape[:-1] + (HEAD_PAD - QK_NOPE - QK_ROPE,), F32)
    wqa = jnp.concatenate([wq_n, wq_r, z_tail], -1).reshape(depth, q_rank, -1).astype(BF16)
    wqb = jnp.concatenate([jnp.zeros_like(wq_n), _swap_halves(wq_r), z_tail], -1)
    wqb = wqb.reshape(depth, q_rank, -1).astype(BF16)
    wuk = jnp.concatenate([w_uk, jnp.zeros(w_uk.shape[:-1] + (HEAD_PAD - QK_NOPE,), F32)], -1)
    wuk = wuk.reshape(depth, kv_rank, -1).astype(BF16)
    wuv = w_uv.reshape(depth, kv_rank, -1).astype(BF16)
    wuvt = jnp.transpose(w_uv, (0, 2, 3, 1))
    wuvt = jnp.concatenate([wuvt, jnp.zeros((depth, N_HEADS, VT_ROWS - V_HEAD, kv_rank), F32)], 2)
    wuvt = wuvt.reshape(depth, N_HEADS * VT_ROWS, kv_rank).astype(BF16)
    place = jnp.tile(place_kr(jnp.eye(QK_ROPE, dtype=F32)), (1, N_HEADS)).astype(BF16)
    woa = w_o_attn.astype(BF16)
    wco = w_conv_out.astype(BF16)
    wout = w_out.astype(BF16)
    wfg, wfu, wfd = w_ffn_gate.astype(BF16), w_ffn_up.astype(BF16), w_ffn_down.astype(BF16)
    weg, weu, wed = w_exp_gate.astype(BF16), w_exp_up.astype(BF16), w_exp_down.astype(BF16)

    pos = jnp.concatenate([jnp.tile(jnp.arange(seq, dtype=jnp.int32), batch),
                           jnp.tile(past_len + jnp.arange(t_new, dtype=jnp.int32), n_str)])
    ctq, stq = _rope_tables(pos, ATTN_SCALE * LOG2E)
    ck, sk = _rope_tables(pos, 1.0)
    ck = ck.at[:, :QK_NOPE].set(0.0)
    tabs = dict(ctq=ctq, stq=stq, ck=ck, sk=sk)

    kpast, vpast = _kvpast_call(cache_ckv.reshape(depth, n_str * past_len, kv_rank),
                                cache_kr.reshape(depth, n_str * past_len, QK_ROPE), wuk, wuv, place)
    kpast = kpast.reshape(depth * n_str, past_len, -1)
    vpast = vpast.reshape(depth * n_str, past_len, -1)

    x = _ln_call(jnp.concatenate([x_prompt.reshape(n_p, d), x_sample.reshape(n_s, d)], axis=0),
                 ln_in_g, ln_in_b)
    tc = _pick_tile(seq, (512, 256, 128, 64))
    ckv_l, kr_l, conv_p_l, conv_s_l = [], [], [], []
    for l in range(depth):
        wl = dict(wa=wa[l], ba=ba[l], wb=wb[l], bb=bb[l], gq=g_qnorm[l][None], wqa=wqa[l], wqb=wqb[l],
                  gkv=g_kvnorm[l][None], wuk=wuk[l], wuv=wuv[l], wuvt=wuvt[l], woa=woa[l], wco=wco[l], wout=wout[l],
                  ln1g=ln1_g[l][None], ln1b=ln1_b[l][None])
        q, k, v, vt, ckv, kr, u, gates = _inproj_call(x, wl, tabs, (q_rank, kv_rank, c_conv))
        attn_p = _attn_prompt_call(q, k, vt, batch, seq)
        attn_s = _attn_sample_call(q, k, v, kpast, vpast, l, n_p, n_str, t_new, past_len)
        u_p = u[:n_p].reshape(batch, seq // tc, tc, c_conv)
        halo_p = jnp.concatenate([jnp.zeros((batch, 1, halo, c_conv), F32),
                                  u_p[:, :-1, tc - halo:, :]], axis=1).reshape(-1, c_conv)
        u_s = u[n_p:]
        halo_s = jnp.concatenate([jnp.zeros((n_str, halo - (conv_width - 1), c_conv), F32),
                                  state_conv[l]], axis=1).reshape(-1, c_conv)
        conv_args = (conv_w[l], conv_b[l], conv_ln_g[l], conv_ln_b[l])
        cact_p = _conv_call(u, 0, n_p // tc, halo_p, tc, *conv_args)
        cact_s = _conv_call(u, n_p, n_str, halo_s, t_new, *conv_args)
        x = _outproj_call(x, attn_p, attn_s, cact_p, cact_s, gates, wl, alpha)
        i = l // 2
        if l % 2 == 0:
            x = _ffn_call(x, wfg[i], wfu[i], wfd[i], ln2_g[l][None], ln2_b[l][None], alpha)
        else:
            x = _moe_layer(x, w_router[i], weg[i], weu[i], wed[i], ln2_g[l][None], ln2_b[l][None],
                           alpha)
        ckv_l.append(ckv)
        kr_l.append(kr[:, QK_NOPE:QK_NOPE + QK_ROPE])
        conv_p_l.append(u[:n_p].reshape(batch, seq, c_conv)[:, seq - (conv_width - 1):, :])
        conv_s_l.append(u_s.reshape(n_str, t_new, c_conv)[:, t_new - (conv_width - 1):, :])
    ckv_all = jnp.stack(ckv_l)
    kr_all = jnp.stack(kr_l)
    return (x[:n_p].reshape(batch, seq, d), x[n_p:].reshape(n_str, t_new, d),
            ckv_all[:, :n_p].reshape(depth, batch, seq, kv_rank),
            kr_all[:, :n_p].reshape(depth, batch, seq, QK_ROPE),
            jnp.stack(conv_p_l),
            ckv_all[:, n_p:].reshape(depth, n_str, t_new, kv_rank),
            kr_all[:, n_p:].reshape(depth, n_str, t_new, QK_ROPE),
            jnp.stack(conv_s_l))
```

```python
import functools
import math

import numpy as np
import jax
import jax.numpy as jnp
from jax import lax
from jax.experimental import pallas as pl
from jax.experimental.pallas import tpu as pltpu

CHUNK = 64
N_HEADS = 8
QK_NOPE = 64
QK_ROPE = 32
V_HEAD = 64
ROPE_THETA = 10000.0
TOP_K = 2
LN_EPS = 1e-5
NEG_INF = -1e30
ATTN_SCALE = (QK_NOPE + QK_ROPE) ** -0.5

LANES = 128
SUBLANES = 8
HEAD_PAD = 128
HEADS_PER_STEP = 2
VT_ROWS = V_HEAD + 16
VT_BLOCK = 256
ATTN_UNROLL = 4
LOG2E = math.log2(math.e)
MOE_ROWS = 256
VMEM_LIMIT = 56 * 1024 * 1024

BF16 = jnp.bfloat16
F32 = jnp.float32


def _cparams(sem):
    return pltpu.CompilerParams(dimension_semantics=sem, vmem_limit_bytes=VMEM_LIMIT)


def _pick_tile(n, candidates):
    for c in candidates:
        if n % c == 0:
            return c
    raise ValueError(f"no tile in {candidates} divides {n}")


def _dot(a, b):
    return jnp.dot(a, b, preferred_element_type=F32)


def _dot_nt(a, b):
    return lax.dot_general(a, b, (((1,), (1,)), ((), ())), preferred_element_type=F32)


def _layernorm(x, g, b):
    mu = jnp.mean(x, axis=-1, keepdims=True)
    xc = x - mu
    var = jnp.mean(xc * xc, axis=-1, keepdims=True)
    return xc * lax.rsqrt(var + LN_EPS) * g + b


def _rmsnorm(x, g):
    return x * lax.rsqrt(jnp.mean(x * x, axis=-1, keepdims=True) + LN_EPS) * g


def _silu(x):
    return x * jax.nn.sigmoid(x)


def _ln_kernel(x_ref, g_ref, b_ref, o_ref):
    o_ref[...] = _layernorm(x_ref[...], g_ref[...], b_ref[...])


def _ln_call(x, g, b):
    n, d = x.shape
    tm = _pick_tile(n, (1024, 512, 256, 128, 64, 32))
    return pl.pallas_call(
        _ln_kernel,
        grid=(n // tm,),
        in_specs=[pl.BlockSpec((tm, d), lambda i: (i, 0)),
                  pl.BlockSpec((1, d), lambda i: (0, 0)),
                  pl.BlockSpec((1, d), lambda i: (0, 0))],
        out_specs=pl.BlockSpec((tm, d), lambda i: (i, 0)),
        out_shape=jax.ShapeDtypeStruct((n, d), F32),
        compiler_params=_cparams(("parallel",)),
        name="ln_in",
    )(x, g.reshape(1, d), b.reshape(1, d))


def _inproj_kernel(q_rank, kv_rank, c_conv,
                   x_ref, wa_ref, ba_ref, wb_ref, bb_ref, gq_ref, wqa_ref, wqb_ref, gkv_ref,
                   wuk_ref, wuv_ref, wuvt_ref, ct_ref, st_ref,
                   q_ref, k_ref, v_ref, vt_ref, ckv_ref, kr_ref, u_ref, gate_ref):
    xb = x_ref[...].astype(BF16)
    pa = _dot(xb, wa_ref[...]) + ba_ref[...]
    o1, o2, o3 = q_rank, q_rank + kv_rank, q_rank + kv_rank + HEAD_PAD
    cqn = _rmsnorm(pa[:, :o1], gq_ref[...]).astype(BF16)
    ct = jnp.concatenate([ct_ref[...]] * N_HEADS, axis=1)
    st = jnp.concatenate([st_ref[...]] * N_HEADS, axis=1)
    q = (_dot(cqn, wqa_ref[...]) * ct + _dot(cqn, wqb_ref[...]) * st) * (ATTN_SCALE * LOG2E)
    q_ref[...] = q.astype(BF16)
    ckv = _rmsnorm(pa[:, o1:o2], gkv_ref[...])
    ckv_ref[...] = ckv
    kr = pa[:, o2:o3] * ct_ref[...] + pa[:, o3:] * st_ref[...]
    kr_ref[...] = kr
    ckvb = ckv.astype(BF16)
    k = _dot(ckvb, wuk_ref[...]) + jnp.concatenate([kr] * N_HEADS, axis=1)
    k_ref[...] = k.astype(BF16)
    v_ref[...] = _dot(ckvb, wuv_ref[...]).astype(BF16)
    vt = _dot_nt(wuvt_ref[...], ckvb)
    ones_row = lax.broadcasted_iota(jnp.int32, vt.shape, 0) % VT_ROWS >= V_HEAD
    vt_ref[...] = jnp.where(ones_row, 1.0, vt).astype(BF16)
    pu = _dot(xb, wb_ref[:, :2 * c_conv]) + bb_ref[:, :2 * c_conv]
    u_ref[...] = pu[:, :c_conv] * jax.nn.sigmoid(pu[:, c_conv:])
    pg = _dot(xb, wb_ref[:, 2 * c_conv:]) + bb_ref[:, 2 * c_conv:]
    gate_ref[...] = jax.nn.sigmoid(pg).astype(BF16)


def _layer_spec(w, layer):
    return pl.BlockSpec((None,) + w.shape[1:], lambda *_: (layer, 0, 0))


def _inproj_call(x, ws, layer, tabs, dims):
    n, d = x.shape
    q_rank, kv_rank, c_conv = dims
    tm = VT_BLOCK
    assert n % tm == 0
    hq = N_HEADS * HEAD_PAD
    hv = N_HEADS * V_HEAD
    hvt = N_HEADS * VT_ROWS
    row = lambda i: (i, 0)
    names = ("wa", "ba", "wb", "bb", "gq", "wqa", "wqb", "gkv", "wuk", "wuv", "wuvt")
    in_specs = ([pl.BlockSpec((tm, d), row)] + [_layer_spec(ws[k], layer) for k in names]
                + [pl.BlockSpec((tm, HEAD_PAD), row), pl.BlockSpec((tm, HEAD_PAD), row)])
    out_specs = [
        pl.BlockSpec((tm, hq), row), pl.BlockSpec((tm, hq), row), pl.BlockSpec((tm, hv), row),
        pl.BlockSpec((None, hvt, tm), lambda i: (i, 0, 0)),
        pl.BlockSpec((tm, kv_rank), row), pl.BlockSpec((tm, HEAD_PAD), row),
        pl.BlockSpec((tm, c_conv), row), pl.BlockSpec((tm, 2 * d), row),
    ]
    out_shape = [
        jax.ShapeDtypeStruct((n, hq), BF16), jax.ShapeDtypeStruct((n, hq), BF16),
        jax.ShapeDtypeStruct((n, hv), BF16), jax.ShapeDtypeStruct((n // tm, hvt, tm), BF16),
        jax.ShapeDtypeStruct((n, kv_rank), F32), jax.ShapeDtypeStruct((n, HEAD_PAD), F32),
        jax.ShapeDtypeStruct((n, c_conv), F32), jax.ShapeDtypeStruct((n, 2 * d), BF16),
    ]
    return pl.pallas_call(
        functools.partial(_inproj_kernel, q_rank, kv_rank, c_conv),
        grid=(n // tm,), in_specs=in_specs, out_specs=out_specs, out_shape=out_shape,
        compiler_params=_cparams(("parallel",)), name="inproj",
    )(x, *[ws[k] for k in names], tabs["ct"], tabs["st"])


def _kvpast_kernel(ckv_ref, kr_ref, wuk_ref, wuv_ref, place_ref, k_ref, v_ref):
    ckvb = ckv_ref[...].astype(BF16)
    k = _dot(ckvb, wuk_ref[...]) + _dot(kr_ref[...].astype(BF16), place_ref[...])
    k_ref[...] = k.astype(BF16)
    v_ref[...] = _dot(ckvb, wuv_ref[...]).astype(BF16)


def _kvpast_call(ckv, kr, wuk, wuv, place):
    depth, rows, kv_rank = ckv.shape
    rope = kr.shape[-1]
    tm = _pick_tile(rows, (1024, 512, 256, 128, 64, 32))
    tpl = rows // tm
    hq = N_HEADS * HEAD_PAD
    hv = N_HEADS * V_HEAD
    return pl.pallas_call(
        _kvpast_kernel,
        grid=(depth, tpl),
        in_specs=[pl.BlockSpec((None, tm, kv_rank), lambda l, i: (l, i, 0)),
                  pl.BlockSpec((None, tm, rope), lambda l, i: (l, i, 0)),
                  pl.BlockSpec((None, kv_rank, hq), lambda l, i: (l, 0, 0)),
                  pl.BlockSpec((None, kv_rank, hv), lambda l, i: (l, 0, 0)),
                  pl.BlockSpec((rope, hq), lambda l, i: (0, 0))],
        out_specs=[pl.BlockSpec((None, tm, hq), lambda l, i: (l, i, 0)),
                   pl.BlockSpec((None, tm, hv), lambda l, i: (l, i, 0))],
        out_shape=[jax.ShapeDtypeStruct((depth, rows, hq), BF16),
                   jax.ShapeDtypeStruct((depth, rows, hv), BF16)],
        compiler_params=_cparams(("parallel", "parallel")), name="kvpast",
    )(ckv, kr, wuk, wuv, place)


def _attn_prompt_kernel(tq, q_ref, k_ref, vt_ref, bias_ref, o_ref, sa_ref, sb_ref):
    qi = pl.program_id(2)
    heads = range(HEADS_PER_STEP)
    nvb = tq // VT_BLOCK

    def scores(j, s_ref):
        start = pl.multiple_of(j * tq, tq)
        for h in heads:
            s_ref[h] = _dot_nt(k_ref[pl.ds(start, tq), h * HEAD_PAD:(h + 1) * HEAD_PAD],
                               q_ref[:, h * HEAD_PAD:(h + 1) * HEAD_PAD])

    def softmax_pv(j, s_ref, state, masked):
        out = []
        for h in heads:
            s = s_ref[h]
            m_old, acc = state[h]
            if masked:
                s = s + bias_ref[...]
            m_new = jnp.maximum(m_old, jnp.max(s, axis=0, keepdims=True))
            p = jnp.exp2(s - m_new).astype(BF16)
            pv = None
            for b in range(nvb):
                vth = vt_ref[j * nvb + b, h * VT_ROWS:(h + 1) * VT_ROWS, :]
                d = _dot(vth, p[b * VT_BLOCK:(b + 1) * VT_BLOCK, :])
                pv = d if pv is None else pv + d
            out.append((m_new, jnp.exp2(m_old - m_new) * acc + pv))
        return tuple(out)

    bufs = (sa_ref, sb_ref)

    def step(j, parity, state):
        scores(j + 1, bufs[1 - parity])
        return softmax_pv(j, bufs[parity], state, False)

    def unrolled(t, state):
        for u in range(ATTN_UNROLL):
            state = step(ATTN_UNROLL * t + u, u % 2, state)
        return state

    init = tuple((jnp.full((1, tq), NEG_INF, F32), jnp.zeros((VT_ROWS, tq), F32)) for _ in heads)
    scores(0, sa_ref)
    state = lax.fori_loop(0, qi // ATTN_UNROLL, unrolled, init)
    done = (qi // ATTN_UNROLL) * ATTN_UNROLL
    left = qi - done
    for u in range(ATTN_UNROLL - 1):
        state = lax.cond(u < left, functools.partial(step, done + u, u % 2), lambda s: s, state)
    state = lax.cond(left % 2 == 1,
                     lambda s: softmax_pv(qi, sb_ref, s, True),
                     lambda s: softmax_pv(qi, sa_ref, s, True), state)
    rows = [acc[:V_HEAD, :] / acc[V_HEAD:V_HEAD + 1, :] for _, acc in state]
    o_ref[...] = jnp.transpose(jnp.concatenate(rows, axis=0)).astype(BF16)


def _attn_prompt_call(q, k, vt, batch, seq):
    tq = _pick_tile(seq, (512, 256))
    nq = seq // tq
    hp = N_HEADS // HEADS_PER_STEP
    qw = HEADS_PER_STEP * HEAD_PAD
    vw = HEADS_PER_STEP * V_HEAD
    vr = HEADS_PER_STEP * VT_ROWS
    nvb = seq // VT_BLOCK
    chunk = np.arange(tq) // CHUNK
    bias = np.where(chunk[:, None] <= chunk[None, :], 0.0, NEG_INF).astype(np.float32)
    return pl.pallas_call(
        functools.partial(_attn_prompt_kernel, tq),
        grid=(batch, hp, nq),
        in_specs=[pl.BlockSpec((tq, qw), lambda b, h, i: (b * nq + i, h)),
                  pl.BlockSpec((seq, qw), lambda b, h, i: (b, h)),
                  pl.BlockSpec((nvb, vr, VT_BLOCK), lambda b, h, i: (b, h, 0)),
                  pl.BlockSpec((tq, tq), lambda b, h, i: (0, 0))],
        out_specs=pl.BlockSpec((tq, vw), lambda b, h, i: (b * nq + i, h)),
        out_shape=jax.ShapeDtypeStruct((batch * seq, N_HEADS * V_HEAD), BF16),
        scratch_shapes=[pltpu.VMEM((HEADS_PER_STEP, tq, tq), F32),
                        pltpu.VMEM((HEADS_PER_STEP, tq, tq), F32)],
        compiler_params=_cparams(("parallel", "parallel", "arbitrary")), name="attn_prompt",
    )(q, k, vt, jnp.asarray(bias))


def _attn_sample_kernel(mask_past, mask_new, past_len,
                        q_ref, kn_ref, vn_ref, kp_ref, vp_ref, o_ref):
    vp = vp_ref[...]
    vn = vn_ref[...]
    outs = []
    for h in range(HEADS_PER_STEP):
        sl = slice(h * HEAD_PAD, (h + 1) * HEAD_PAD)
        qh = q_ref[:, sl]
        sp = _dot_nt(qh, kp_ref[:, sl])
        sn = _dot_nt(qh, kn_ref[:, sl])
        if mask_past:
            qc = (lax.broadcasted_iota(jnp.int32, sp.shape, 0) + past_len) // CHUNK
            kc = lax.broadcasted_iota(jnp.int32, sp.shape, 1) // CHUNK
            sp = jnp.where(kc <= qc, sp, NEG_INF)
        if mask_new:
            qc = (lax.broadcasted_iota(jnp.int32, sn.shape, 0) + past_len) // CHUNK
            kc = (lax.broadcasted_iota(jnp.int32, sn.shape, 1) + past_len) // CHUNK
            sn = jnp.where(kc <= qc, sn, NEG_INF)
        m = jnp.maximum(jnp.max(sp, axis=1, keepdims=True), jnp.max(sn, axis=1, keepdims=True))
        pp = jnp.exp2(sp - m)
        pn = jnp.exp2(sn - m)
        l = jnp.sum(pp, axis=1, keepdims=True) + jnp.sum(pn, axis=1, keepdims=True)
        outs.append((_dot(pp.astype(BF16), vp) + _dot(pn.astype(BF16), vn)) / l)
    lane = lax.broadcasted_iota(jnp.int32, outs[0].shape, 1)
    o_ref[...] = jnp.where(lane < V_HEAD, outs[0], outs[1]).astype(BF16)


def _attn_sample_call(q, k, v, kpast, vpast, layer, row0, n_streams, t_new, past_len):
    hp = N_HEADS // HEADS_PER_STEP
    qw = HEADS_PER_STEP * HEAD_PAD
    vw = HEADS_PER_STEP * V_HEAD
    blk0 = row0 // t_new
    q_pos = past_len + np.arange(t_new)
    mask_past = not bool(np.all((np.arange(past_len) // CHUNK)[None, :] <= (q_pos // CHUNK)[:, None]))
    mask_new = not bool(np.all((q_pos // CHUNK)[None, :] <= (q_pos // CHUNK)[:, None]))
    new_spec = lambda w: pl.BlockSpec((t_new, w), lambda s, h: (blk0 + s, h))
    return pl.pallas_call(
        functools.partial(_attn_sample_kernel, mask_past, mask_new, past_len),
        grid=(n_streams, hp),
        in_specs=[new_spec(qw), new_spec(qw), new_spec(vw),
                  pl.BlockSpec((None, past_len, qw), lambda s, h: (layer * n_streams + s, 0, h)),
                  pl.BlockSpec((None, past_len, vw), lambda s, h: (layer * n_streams + s, 0, h))],
        out_specs=pl.BlockSpec((t_new, vw), lambda s, h: (s, h)),
        out_shape=jax.ShapeDtypeStruct((n_streams * t_new, N_HEADS * V_HEAD), BF16),
        compiler_params=_cparams(("parallel", "parallel")), name="attn_sample",
    )(q, k, v, kpast, vpast)


def _conv_kernel(tc, halo, conv_w, rc, u_ref, h_ref, w_ref, b_ref, g_ref, beta_ref, o_ref, sh_ref):
    sh_ref[0, 0:halo, :] = h_ref[...]
    sh_ref[0, halo:halo + tc, :] = u_ref[...]
    span = halo + tc - SUBLANES
    for r in range(1, SUBLANES):
        sh_ref[r, 0:span, :] = sh_ref[0, r:r + span, :]
    lead = halo - (conv_w - 1)
    for c in range(tc // rc):
        acc = jnp.broadcast_to(b_ref[...], (rc, u_ref.shape[1]))
        for j in range(conv_w):
            r = (lead + j) % SUBLANES
            r0 = c * rc + lead + j - r
            acc = acc + w_ref[j:j + 1, :] * sh_ref[r, r0:r0 + rc, :]
        y = _layernorm(acc, g_ref[...], beta_ref[...])
        o_ref[c * rc:(c + 1) * rc, :] = _silu(y).astype(BF16)


def _conv_call(u, row0, n_tiles, halo_rows, tc, w, b, g, beta):
    c = u.shape[1]
    blk0 = row0 // tc
    halo = halo_rows.shape[0] // n_tiles
    conv_w = w.shape[0]
    assert halo % SUBLANES == 0 and conv_w - 1 <= halo
    rc = min(tc, 64)
    const = lambda i: (0, 0)
    return pl.pallas_call(
        functools.partial(_conv_kernel, tc, halo, conv_w, rc),
        grid=(n_tiles,),
        in_specs=[pl.BlockSpec((tc, c), lambda i: (blk0 + i, 0)),
                  pl.BlockSpec((halo, c), lambda i: (i, 0)),
                  pl.BlockSpec((conv_w, c), const), pl.BlockSpec((1, c), const),
                  pl.BlockSpec((1, c), const), pl.BlockSpec((1, c), const)],
        out_specs=pl.BlockSpec((tc, c), lambda i: (i, 0)),
        out_shape=jax.ShapeDtypeStruct((n_tiles * tc, c), BF16),
        scratch_shapes=[pltpu.VMEM((SUBLANES, halo + tc, c), F32)],
        compiler_params=_cparams(("parallel",)), name="conv",
    )(u, halo_rows, w, b.reshape(1, c), g.reshape(1, c), beta.reshape(1, c))


def _outproj_kernel(alpha, d, p_tiles, x_ref, ap_ref, as_ref, cp_ref, cs_ref, gate_ref,
                    woa_ref, wco_ref, wout_ref, g_ref, b_ref, o_ref):
    is_prompt = pl.program_id(0) < p_tiles
    br_a = _dot(jnp.where(is_prompt, ap_ref[...], as_ref[...]), woa_ref[...])
    br_c = _dot(jnp.where(is_prompt, cp_ref[...], cs_ref[...]), wco_ref[...])
    merged = gate_ref[:, :d].astype(F32) * br_a + gate_ref[:, d:].astype(F32) * br_c
    mix = _dot(merged.astype(BF16), wout_ref[...])
    o_ref[...] = _layernorm(alpha * x_ref[...] + mix, g_ref[...], b_ref[...])


def _outproj_call(x, attn_p, attn_s, cact_p, cact_s, gates, ws, layer, alpha):
    n, d = x.shape
    n_p, n_s = attn_p.shape[0], attn_s.shape[0]
    tm = _pick_tile(math.gcd(n_p, n_s), (512, 256, 128, 64, 32))
    p_tiles = n_p // tm
    names = ("woa", "wco", "wout", "ln1g", "ln1b")
    row = lambda i: (i, 0)
    prow = lambda i: (jnp.minimum(i, p_tiles - 1), 0)
    srow = lambda i: (jnp.maximum(i - p_tiles, 0), 0)
    ha, hc = attn_p.shape[1], cact_p.shape[1]
    return pl.pallas_call(
        functools.partial(_outproj_kernel, alpha, d, p_tiles),
        grid=(n // tm,),
        in_specs=[pl.BlockSpec((tm, d), row), pl.BlockSpec((tm, ha), prow),
                  pl.BlockSpec((tm, ha), srow), pl.BlockSpec((tm, hc), prow),
                  pl.BlockSpec((tm, hc), srow), pl.BlockSpec((tm, 2 * d), row)]
                 + [_layer_spec(ws[k], layer) for k in names],
        out_specs=pl.BlockSpec((tm, d), row),
        out_shape=jax.ShapeDtypeStruct((n, d), F32),
        compiler_params=_cparams(("parallel",)), name="outproj",
    )(x, attn_p, attn_s, cact_p, cact_s, gates, *[ws[k] for k in names])


def _ffn_kernel(alpha, x_ref, wg_ref, wu_ref, wd_ref, g_ref, b_ref, o_ref, acc_ref):
    f = pl.program_id(1)

    @pl.when(f == 0)
    def _():
        acc_ref[...] = jnp.zeros(acc_ref.shape, F32)

    xb = x_ref[...].astype(BF16)
    h = _silu(_dot(xb, wg_ref[...])) * _dot(xb, wu_ref[...])
    acc_ref[...] += _dot(h.astype(BF16), wd_ref[...])

    @pl.when(f == pl.num_programs(1) - 1)
    def _():
        o_ref[...] = _layernorm(alpha * x_ref[...] + acc_ref[...], g_ref[...], b_ref[...])


def _ffn_call(x, wg, wu, wd, li, g, b, alpha):
    n, d = x.shape
    dff = wg.shape[2]
    tm = _pick_tile(n, (512, 256, 128, 64, 32))
    tf = _pick_tile(dff, (1408, 1024, 512, 256, 128))
    return pl.pallas_call(
        functools.partial(_ffn_kernel, alpha),
        grid=(n // tm, dff // tf),
        in_specs=[pl.BlockSpec((tm, d), lambda i, f: (i, 0)),
                  pl.BlockSpec((None, d, tf), lambda i, f: (li, 0, f)),
                  pl.BlockSpec((None, d, tf), lambda i, f: (li, 0, f)),
                  pl.BlockSpec((None, tf, d), lambda i, f: (li, f, 0)),
                  pl.BlockSpec((1, d), lambda i, f: (0, 0)),
                  pl.BlockSpec((1, d), lambda i, f: (0, 0))],
        out_specs=pl.BlockSpec((tm, d), lambda i, f: (i, 0)),
        out_shape=jax.ShapeDtypeStruct((n, d), F32),
        scratch_shapes=[pltpu.VMEM((tm, d), F32)],
        compiler_params=_cparams(("parallel", "arbitrary")), name="ffn",
    )(x, wg, wu, wd, g, b)


def _route_kernel(n_tok, n_exp, tm, x_ref, wr_ref, xb_ref, rank_ref, gate_ref, rankt_ref, cnt_ref):
    i = pl.program_id(0)
    x = x_ref[...]
    row = lax.broadcasted_iota(jnp.int32, (tm, 1), 0) + i * tm
    valid = row < n_tok
    xb_ref[...] = jnp.where(valid, x, 0.0).astype(BF16)
    logits = jnp.dot(x, wr_ref[...], preferred_element_type=F32, precision=lax.Precision.HIGHEST)
    lane = lax.broadcasted_iota(jnp.int32, (tm, LANES), 1)
    logits = jnp.where(lane < n_exp, logits, -jnp.inf)
    m1 = jnp.max(logits, axis=1, keepdims=True)
    i1 = jnp.min(jnp.where(logits == m1, lane, LANES), axis=1, keepdims=True)
    rest = jnp.where(lane == i1, -jnp.inf, logits)
    m2 = jnp.max(rest, axis=1, keepdims=True)
    i2 = jnp.min(jnp.where(rest == m2, lane, LANES), axis=1, keepdims=True)
    e2 = jnp.exp(m2 - m1)
    g1 = 1.0 / (1.0 + e2)
    g2 = e2 / (1.0 + e2)
    sel1 = (lane == i1) & valid
    sel2 = (lane == i2) & valid
    gate_ref[...] = jnp.where(sel1, g1, 0.0) + jnp.where(sel2, g2, 0.0)
    sel = jnp.where(sel1 | sel2, 1.0, 0.0)
    cs = min(tm, 256)
    r = lax.broadcasted_iota(jnp.int32, (cs, cs), 0)
    c = lax.broadcasted_iota(jnp.int32, (cs, cs), 1)
    lower = jnp.where(c < r, 1.0, 0.0).astype(BF16)
    offs = jnp.zeros((1, LANES), F32)
    for ch in range(tm // cs):
        sc = sel[ch * cs:(ch + 1) * cs, :]
        rk = _dot(lower, sc.astype(BF16)) + offs
        rank_ref[ch * cs:(ch + 1) * cs, :] = jnp.where(sc > 0.0, rk, -1.0)
        offs = offs + jnp.sum(sc, axis=0, keepdims=True)
    cnt_ref[...] = offs.astype(jnp.int32)
    rankt_ref[...] = jnp.transpose(rank_ref[...])[:rankt_ref.shape[0], :]


def _route_call(x, wr_pad, n_exp, tm):
    n, d = x.shape
    nt = -(-n // tm)
    return pl.pallas_call(
        functools.partial(_route_kernel, n, n_exp, tm),
        grid=(nt,),
        in_specs=[pl.BlockSpec((tm, d), lambda i: (i, 0)),
                  pl.BlockSpec((d, LANES), lambda i: (0, 0))],
        out_specs=[pl.BlockSpec((tm, d), lambda i: (i, 0)),
                   pl.BlockSpec((tm, LANES), lambda i: (i, 0)),
                   pl.BlockSpec((tm, LANES), lambda i: (i, 0)),
                   pl.BlockSpec((n_exp, tm), lambda i: (0, i)),
                   pl.BlockSpec((None, 1, LANES), lambda i: (i, 0, 0))],
        out_shape=[jax.ShapeDtypeStruct((nt * tm, d), BF16),
                   jax.ShapeDtypeStruct((nt * tm, LANES), F32),
                   jax.ShapeDtypeStruct((nt * tm, LANES), F32),
                   jax.ShapeDtypeStruct((n_exp, nt * tm), F32),
                   jax.ShapeDtypeStruct((nt, 1, LANES), jnp.int32)],
        compiler_params=_cparams(("parallel",)), name="route",
    )(x, wr_pad)


def _moe_kernel(alpha, tm, n256_ref, t128_ref, x_ref, xb_ref, rank_ref, gate_ref, rankt_ref,
                wg_ref, wu_ref, wd_ref, g_ref, b_ref, o_ref, xe_ref, oe_ref, rcol_ref, gcol_ref):
    i = pl.program_id(0)
    e = pl.program_id(1)
    f = pl.program_id(2)
    n_e = pl.num_programs(1)
    nb = n256_ref[i * n_e + e]
    has_half = t128_ref[i * n_e + e] == 1
    half = MOE_ROWS // 2
    half0 = pl.multiple_of(nb * MOE_ROWS, half)

    def for_blocks(fn):
        def body(j, carry):
            fn(pl.multiple_of(j * MOE_ROWS, MOE_ROWS), MOE_ROWS)
            return carry
        lax.fori_loop(0, nb, body, 0)

        @pl.when(has_half)
        def _():
            fn(half0, half)

    @pl.when((e == 0) & (f == 0))
    def _():
        o_ref[...] = jnp.zeros(o_ref.shape, F32)

    @pl.when(f == 0)
    def _():
        lane = lax.broadcasted_iota(jnp.int32, (tm, LANES), 1)
        rcol_ref[...] = jnp.max(jnp.where(lane == e, rank_ref[...], -1.0), axis=1, keepdims=True)
        gcol_ref[...] = jnp.sum(jnp.where(lane == e, gate_ref[...], 0.0), axis=1, keepdims=True)

        def gather(r0, rows):
            rrow = rankt_ref[pl.ds(e, 1), :]
            slot = (lax.broadcasted_iota(jnp.int32, (rows, tm), 0) + r0).astype(F32)
            onehot = jnp.where(rrow == slot, 1.0, 0.0).astype(BF16)
            xe_ref[pl.ds(r0, rows), :] = _dot(onehot, xb_ref[...]).astype(BF16)

        for_blocks(gather)

    def expert(r0, rows):
        xe = xe_ref[pl.ds(r0, rows), :]
        h = _silu(_dot(xe, wg_ref[...])) * _dot(xe, wu_ref[...])
        o = _dot(h.astype(BF16), wd_ref[...])

        @pl.when(f == 0)
        def _():
            oe_ref[pl.ds(r0, rows), :] = o

        @pl.when(f != 0)
        def _():
            oe_ref[pl.ds(r0, rows), :] += o

    for_blocks(expert)

    @pl.when(f == pl.num_programs(2) - 1)
    def _():
        def scatter(r0, rows):
            ob = oe_ref[pl.ds(r0, rows), :].astype(BF16)
            sb = min(tm, 512)
            for t in range(tm // sb):
                ts = slice(t * sb, (t + 1) * sb)
                slot = (lax.broadcasted_iota(jnp.int32, (sb, rows), 1) + r0).astype(F32)
                onehot_t = jnp.where(rcol_ref[ts, :] == slot, 1.0, 0.0).astype(BF16)
                o_ref[ts, :] += gcol_ref[ts, :] * _dot(onehot_t, ob)

        for_blocks(scatter)

    @pl.when((e == n_e - 1) & (f == pl.num_programs(2) - 1))
    def _():
        rb = min(tm, 256)
        for t in range(tm // rb):
            ts = slice(t * rb, (t + 1) * rb)
            o_ref[ts, :] = _layernorm(alpha * x_ref[ts, :] + o_ref[ts, :], g_ref[...], b_ref[...])


def _moe_call(x, xb, rank, gate, rankt, n256, t128, wg, wu, wd, li, g, b, alpha, tm):
    n_tok, d = x.shape
    nt = xb.shape[0] // tm
    _, n_exp, _, dff = wg.shape
    tf = _pick_tile(dff, (896, 512, 256, 128))
    once = pl.Buffered(1)
    tile = lambda i, e, f, *_: (i, 0)
    const = lambda i, e, f, *_: (0, 0)
    grid_spec = pltpu.PrefetchScalarGridSpec(
        num_scalar_prefetch=2,
        grid=(nt, n_exp, dff // tf),
        in_specs=[pl.BlockSpec((tm, d), tile, pipeline_mode=once),
                  pl.BlockSpec((tm, d), tile, pipeline_mode=once),
                  pl.BlockSpec((tm, LANES), tile, pipeline_mode=once),
                  pl.BlockSpec((tm, LANES), tile, pipeline_mode=once),
                  pl.BlockSpec((n_exp, tm), lambda i, e, f, *_: (0, i), pipeline_mode=once),
                  pl.BlockSpec((None, None, d, tf), lambda i, e, f, *_: (li, e, 0, f)),
                  pl.BlockSpec((None, None, d, tf), lambda i, e, f, *_: (li, e, 0, f)),
                  pl.BlockSpec((None, None, tf, d), lambda i, e, f, *_: (li, e, f, 0)),
                  pl.BlockSpec((1, d), const), pl.BlockSpec((1, d), const)],
        out_specs=pl.BlockSpec((tm, d), tile, pipeline_mode=once),
        scratch_shapes=[pltpu.VMEM((tm, d), BF16), pltpu.VMEM((tm, d), F32),
                        pltpu.VMEM((tm, 1), F32), pltpu.VMEM((tm, 1), F32)],
    )
    return pl.pallas_call(
        functools.partial(_moe_kernel, alpha, tm),
        grid_spec=grid_spec,
        out_shape=jax.ShapeDtypeStruct((n_tok, d), F32),
        compiler_params=_cparams(("parallel", "arbitrary", "arbitrary")), name="moe",
    )(n256, t128, x, xb, rank, gate, rankt, wg, wu, wd, g, b)


def _moe_layer(x, w_router, wg, wu, wd, li, g, b, alpha):
    n, d = x.shape
    n_exp = w_router.shape[1]
    tm = 2048 if n >= 2048 else _pick_tile(n, (1024, 512, 256, 128))
    wr_pad = jnp.zeros((d, LANES), F32).at[:, :n_exp].set(w_router)
    xb, rank, gate, rankt, cnt = _route_call(x, wr_pad, n_exp, tm)
    cnt = cnt[:, 0, :n_exp].reshape(-1)
    rem = cnt % MOE_ROWS
    t128 = ((rem > 0) & (rem <= MOE_ROWS // 2)).astype(jnp.int32)
    n256 = cnt // MOE_ROWS + (rem > MOE_ROWS // 2).astype(jnp.int32)
    return _moe_call(x, xb, rank, gate, rankt, n256, t128, wg, wu, wd, li, g, b, alpha, tm)


def _swap_halves(w):
    half = w.shape[-1] // 2
    return jnp.concatenate([w[..., half:], w[..., :half]], axis=-1)


def _rope_tables(pos):
    inv = ROPE_THETA ** (-jnp.arange(QK_ROPE // 2, dtype=F32) * (2.0 / QK_ROPE))
    ang = pos.astype(F32)[:, None] * inv[None, :]
    cos, sin = jnp.cos(ang), jnp.sin(ang)
    n = pos.shape[0]
    pad = jnp.zeros((n, HEAD_PAD - QK_NOPE - QK_ROPE), F32)
    ct = jnp.concatenate([jnp.ones((n, QK_NOPE), F32), cos, cos, pad], axis=1)
    st = jnp.concatenate([jnp.zeros((n, QK_NOPE), F32), -sin, sin, pad], axis=1)
    return ct, st


def kernel(x_prompt, x_sample, cache_ckv, cache_kr, state_conv, ln_in_g, ln_in_b, w_in, b_in, g_qnorm, w_uq, g_kvnorm, w_uk, w_uv, w_o_attn, conv_w, conv_b, conv_ln_g, conv_ln_b, w_conv_out, w_out, ln1_g, ln1_b, ln2_g, ln2_b, w_ffn_gate, w_ffn_up, w_ffn_down, w_router, w_exp_gate, w_exp_up, w_exp_down):
    batch, seq, d = x_prompt.shape
    n_str, t_new, _ = x_sample.shape
    depth, _, past_len, kv_rank = cache_ckv.shape
    q_rank = g_qnorm.shape[1]
    c_conv = conv_w.shape[2]
    conv_width = conv_w.shape[1]
    halo = 32
    assert conv_width - 1 <= halo and t_new >= conv_width - 1 and t_new % 16 == 0
    assert w_uq.shape[2] == N_HEADS * (QK_NOPE + QK_ROPE) and cache_kr.shape[3] == QK_ROPE
    alpha = (2 * depth) ** 0.25
    n_p = batch * seq
    n_s = n_str * t_new
    n = n_p + n_s

    o1, o2, o3 = q_rank, q_rank + kv_rank, q_rank + kv_rank + QK_ROPE
    o4 = o3 + 2 * c_conv

    def place_kr(w):
        lead = jnp.zeros(w.shape[:-1] + (QK_NOPE,), F32)
        tail = jnp.zeros(w.shape[:-1] + (HEAD_PAD - QK_NOPE - QK_ROPE,), F32)
        return jnp.concatenate([lead, w, tail], axis=-1)

    def split_a(w):
        kr = w[..., o2:o3]
        return jnp.concatenate([w[..., :o2], place_kr(kr), place_kr(_swap_halves(kr))], axis=-1)

    wa = split_a(w_in).astype(BF16)
    ba = split_a(b_in)[:, None, :]
    wb = w_in[..., o3:].astype(BF16)
    bb = b_in[:, None, o3:]
    wq = w_uq.reshape(depth, q_rank, N_HEADS, QK_NOPE + QK_ROPE)
    wq_n, wq_r = wq[..., :QK_NOPE], wq[..., QK_NOPE:]
    z_tail = jnp.zeros(wq_r.shape[:-1] + (HEAD_PAD - QK_NOPE - QK_ROPE,), F32)
    wqa = jnp.concatenate([wq_n, wq_r, z_tail], -1).reshape(depth, q_rank, -1).astype(BF16)
    wqb = jnp.concatenate([jnp.zeros_like(wq_n), _swap_halves(wq_r), z_tail], -1)
    wqb = wqb.reshape(depth, q_rank, -1).astype(BF16)
    wuk = jnp.concatenate([w_uk, jnp.zeros(w_uk.shape[:-1] + (HEAD_PAD - QK_NOPE,), F32)], -1)
    wuk = wuk.reshape(depth, kv_rank, -1).astype(BF16)
    wuv = w_uv.reshape(depth, kv_rank, -1).astype(BF16)
    wuvt = jnp.transpose(w_uv, (0, 2, 3, 1))
    wuvt = jnp.concatenate([wuvt, jnp.zeros((depth, N_HEADS, VT_ROWS - V_HEAD, kv_rank), F32)], 2)
    wuvt = wuvt.reshape(depth, N_HEADS * VT_ROWS, kv_rank).astype(BF16)
    place = jnp.tile(place_kr(jnp.eye(QK_ROPE, dtype=F32)), (1, N_HEADS)).astype(BF16)
    woa = w_o_attn.astype(BF16)
    wco = w_conv_out.astype(BF16)
    wout = w_out.astype(BF16)
    wfg, wfu, wfd = w_ffn_gate.astype(BF16), w_ffn_up.astype(BF16), w_ffn_down.astype(BF16)
    weg, weu, wed = w_exp_gate.astype(BF16), w_exp_up.astype(BF16), w_exp_down.astype(BF16)

    ct_p, st_p = _rope_tables(jnp.arange(seq, dtype=jnp.int32))
    ct_s, st_s = _rope_tables(past_len + jnp.arange(t_new, dtype=jnp.int32))
    tabs = dict(ct=jnp.concatenate([jnp.tile(ct_p, (batch, 1)), jnp.tile(ct_s, (n_str, 1))]),
                st=jnp.concatenate([jnp.tile(st_p, (batch, 1)), jnp.tile(st_s, (n_str, 1))]))

    kpast, vpast = _kvpast_call(cache_ckv.reshape(depth, n_str * past_len, kv_rank),
                                cache_kr.reshape(depth, n_str * past_len, QK_ROPE), wuk, wuv, place)
    kpast = kpast.reshape(depth * n_str, past_len, -1)
    vpast = vpast.reshape(depth * n_str, past_len, -1)

    x = _ln_call(jnp.concatenate([x_prompt.reshape(n_p, d), x_sample.reshape(n_s, d)], axis=0),
                 ln_in_g, ln_in_b)
    tc = _pick_tile(seq, (512, 256, 128, 64))
    ws = dict(wa=wa, ba=ba, wb=wb, bb=bb, gq=g_qnorm[:, None, :], wqa=wqa, wqb=wqb,
              gkv=g_kvnorm[:, None, :], wuk=wuk, wuv=wuv, wuvt=wuvt, woa=woa, wco=wco, wout=wout,
              ln1g=ln1_g[:, None, :], ln1b=ln1_b[:, None, :])
    first_tile = (np.arange(n_p // tc) % (seq // tc) == 0)[:, None, None]
    keep = conv_width - 1
    ckv_l, kr_l, conv_p_l, conv_s_l = [], [], [], []
    for l in range(depth):
        q, k, v, vt, ckv, kr, u, gates = _inproj_call(x, ws, l, tabs, (q_rank, kv_rank, c_conv))
        attn_p = _attn_prompt_call(q, k, vt, batch, seq)
        attn_s = _attn_sample_call(q, k, v, kpast, vpast, l, n_p, n_str, t_new, past_len)
        tails = u[:n_p - tc].reshape(n_p // tc - 1, tc, c_conv)[:, tc - halo:, :]
        prev = jnp.concatenate([jnp.zeros((1, halo, c_conv), F32), tails], axis=0)
        halo_p = jnp.where(first_tile, 0.0, prev).reshape(-1, c_conv)
        halo_s = jnp.concatenate([jnp.zeros((n_str, halo - keep, c_conv), F32),
                                  state_conv[l]], axis=1).reshape(-1, c_conv)
        conv_args = (conv_w[l], conv_b[l], conv_ln_g[l], conv_ln_b[l])
        cact_p = _conv_call(u, 0, n_p // tc, halo_p, tc, *conv_args)
        cact_s = _conv_call(u, n_p, n_str, halo_s, t_new, *conv_args)
        x = _outproj_call(x, attn_p, attn_s, cact_p, cact_s, gates, ws, l, alpha)
        i = l // 2
        if l % 2 == 0:
            x = _ffn_call(x, wfg, wfu, wfd, i, ln2_g[l][None], ln2_b[l][None], alpha)
        else:
            x = _moe_layer(x, w_router[i], weg, weu, wed, i, ln2_g[l][None], ln2_b[l][None], alpha)
        ckv_l.append(ckv)
        kr_l.append(kr[:, QK_NOPE:QK_NOPE + QK_ROPE])
        conv_p_l.append(jnp.stack([u[(b + 1) * seq - keep:(b + 1) * seq] for b in range(batch)]))
        conv_s_l.append(u[n_p:].reshape(n_str, t_new, c_conv)[:, t_new - keep:, :])
    ckv_all = jnp.stack(ckv_l)
    kr_all = jnp.stack(kr_l)
    return (x[:n_p].reshape(batch, seq, d), x[n_p:].reshape(n_str, t_new, d),
            ckv_all[:, :n_p].reshape(depth, batch, seq, kv_rank),
            kr_all[:, :n_p].reshape(depth, batch, seq, QK_ROPE),
            jnp.stack(conv_p_l),
            ckv_all[:, n_p:].reshape(depth, n_str, t_new, kv_rank),
            kr_all[:, n_p:].reshape(depth, n_str, t_new, QK_ROPE),
            jnp.stack(conv_s_l))
```

```python
import functools
import math

import numpy as np
import jax
import jax.numpy as jnp
from jax import lax
from jax.experimental import pallas as pl
from jax.experimental.pallas import tpu as pltpu

CHUNK = 64
N_HEADS = 8
QK_NOPE = 64
QK_ROPE = 32
V_HEAD = 64
ROPE_THETA = 10000.0
TOP_K = 2
LN_EPS = 1e-5
NEG_INF = -1e30
ATTN_SCALE = (QK_NOPE + QK_ROPE) ** -0.5

LANES = 128
SUBLANES = 8
HEAD_PAD = 128
HEADS_PER_STEP = 2
VT_ROWS = V_HEAD + 16
VT_BLOCK = 256
LOG2E = math.log2(math.e)
MOE_ROWS = 256
MOE_CHUNK = 512
VMEM_LIMIT = 56 * 1024 * 1024

BF16 = jnp.bfloat16
F32 = jnp.float32


def _cparams(sem):
    return pltpu.CompilerParams(dimension_semantics=sem, vmem_limit_bytes=VMEM_LIMIT)


def _pick_tile(n, candidates):
    for c in candidates:
        if n % c == 0:
            return c
    raise ValueError(f"no tile in {candidates} divides {n}")


def _dot(a, b):
    return jnp.dot(a, b, preferred_element_type=F32)


def _dot_nt(a, b):
    return lax.dot_general(a, b, (((1,), (1,)), ((), ())), preferred_element_type=F32)


def _layernorm(x, g, b):
    mu = jnp.mean(x, axis=-1, keepdims=True)
    xc = x - mu
    var = jnp.mean(xc * xc, axis=-1, keepdims=True)
    return xc * lax.rsqrt(var + LN_EPS) * g + b


def _rmsnorm(x, g):
    return x * lax.rsqrt(jnp.mean(x * x, axis=-1, keepdims=True) + LN_EPS) * g


def _silu(x):
    return x * jax.nn.sigmoid(x)


def _ln_kernel(x_ref, g_ref, b_ref, o_ref):
    o_ref[...] = _layernorm(x_ref[...], g_ref[...], b_ref[...])


def _ln_call(x, g, b):
    n, d = x.shape
    tm = _pick_tile(n, (1024, 512, 256, 128, 64, 32))
    return pl.pallas_call(
        _ln_kernel,
        grid=(n // tm,),
        in_specs=[pl.BlockSpec((tm, d), lambda i: (i, 0)),
                  pl.BlockSpec((1, d), lambda i: (0, 0)),
                  pl.BlockSpec((1, d), lambda i: (0, 0))],
        out_specs=pl.BlockSpec((tm, d), lambda i: (i, 0)),
        out_shape=jax.ShapeDtypeStruct((n, d), F32),
        compiler_params=_cparams(("parallel",)),
        name="ln_in",
    )(x, g.reshape(1, d), b.reshape(1, d))


def _inproj_kernel(q_rank, kv_rank, c_conv,
                   x_ref, wa_ref, ba_ref, wb_ref, bb_ref, gq_ref, wqa_ref, wqb_ref, gkv_ref,
                   wuk_ref, wuv_ref, wuvt_ref, ct_ref, st_ref,
                   q_ref, k_ref, v_ref, vt_ref, ckv_ref, kr_ref, u_ref, gate_ref):
    xb = x_ref[...].astype(BF16)
    pa = _dot(xb, wa_ref[...]) + ba_ref[...]
    o1, o2, o3 = q_rank, q_rank + kv_rank, q_rank + kv_rank + HEAD_PAD
    cqn = _rmsnorm(pa[:, :o1], gq_ref[...]).astype(BF16)
    ct = jnp.concatenate([ct_ref[...]] * N_HEADS, axis=1)
    st = jnp.concatenate([st_ref[...]] * N_HEADS, axis=1)
    q = (_dot(cqn, wqa_ref[...]) * ct + _dot(cqn, wqb_ref[...]) * st) * (ATTN_SCALE * LOG2E)
    q_ref[...] = q.astype(BF16)
    ckv = _rmsnorm(pa[:, o1:o2], gkv_ref[...])
    ckv_ref[...] = ckv
    kr = pa[:, o2:o3] * ct_ref[...] + pa[:, o3:] * st_ref[...]
    kr_ref[...] = kr
    ckvb = ckv.astype(BF16)
    k = _dot(ckvb, wuk_ref[...]) + jnp.concatenate([kr] * N_HEADS, axis=1)
    k_ref[...] = k.astype(BF16)
    v_ref[...] = _dot(ckvb, wuv_ref[...]).astype(BF16)
    vt = _dot_nt(wuvt_ref[...], ckvb)
    ones_row = lax.broadcasted_iota(jnp.int32, vt.shape, 0) % VT_ROWS >= V_HEAD
    vt_ref[...] = jnp.where(ones_row, 1.0, vt).astype(BF16)
    pu = _dot(xb, wb_ref[:, :2 * c_conv]) + bb_ref[:, :2 * c_conv]
    u_ref[...] = pu[:, :c_conv] * jax.nn.sigmoid(pu[:, c_conv:])
    pg = _dot(xb, wb_ref[:, 2 * c_conv:]) + bb_ref[:, 2 * c_conv:]
    gate_ref[...] = jax.nn.sigmoid(pg).astype(BF16)


def _layer_spec(w, layer):
    return pl.BlockSpec((None,) + w.shape[1:], lambda *_: (layer, 0, 0))


def _inproj_call(x, ws, layer, tabs, dims):
    n, d = x.shape
    q_rank, kv_rank, c_conv = dims
    tm = VT_BLOCK
    assert n % tm == 0
    hq = N_HEADS * HEAD_PAD
    hv = N_HEADS * V_HEAD
    hvt = N_HEADS * VT_ROWS
    row = lambda i: (i, 0)
    names = ("wa", "ba", "wb", "bb", "gq", "wqa", "wqb", "gkv", "wuk", "wuv", "wuvt")
    in_specs = ([pl.BlockSpec((tm, d), row)] + [_layer_spec(ws[k], layer) for k in names]
                + [pl.BlockSpec((tm, HEAD_PAD), row), pl.BlockSpec((tm, HEAD_PAD), row)])
    out_specs = [
        pl.BlockSpec((tm, hq), row), pl.BlockSpec((tm, hq), row), pl.BlockSpec((tm, hv), row),
        pl.BlockSpec((None, hvt, tm), lambda i: (i, 0, 0)),
        pl.BlockSpec((tm, kv_rank), row), pl.BlockSpec((tm, HEAD_PAD), row),
        pl.BlockSpec((tm, c_conv), row), pl.BlockSpec((tm, 2 * d), row),
    ]
    out_shape = [
        jax.ShapeDtypeStruct((n, hq), BF16), jax.ShapeDtypeStruct((n, hq), BF16),
        jax.ShapeDtypeStruct((n, hv), BF16), jax.ShapeDtypeStruct((n // tm, hvt, tm), BF16),
        jax.ShapeDtypeStruct((n, kv_rank), F32), jax.ShapeDtypeStruct((n, HEAD_PAD), F32),
        jax.ShapeDtypeStruct((n, c_conv), F32), jax.ShapeDtypeStruct((n, 2 * d), BF16),
    ]
    return pl.pallas_call(
        functools.partial(_inproj_kernel, q_rank, kv_rank, c_conv),
        grid=(n // tm,), in_specs=in_specs, out_specs=out_specs, out_shape=out_shape,
        compiler_params=_cparams(("parallel",)), name="inproj",
    )(x, *[ws[k] for k in names], tabs["ct"], tabs["st"])


def _kvpast_kernel(ckv_ref, kr_ref, wuk_ref, wuv_ref, place_ref, k_ref, v_ref):
    ckvb = ckv_ref[...].astype(BF16)
    k = _dot(ckvb, wuk_ref[...]) + _dot(kr_ref[...].astype(BF16), place_ref[...])
    k_ref[...] = k.astype(BF16)
    v_ref[...] = _dot(ckvb, wuv_ref[...]).astype(BF16)


def _kvpast_call(ckv, kr, wuk, wuv, place):
    depth, rows, kv_rank = ckv.shape
    rope = kr.shape[-1]
    tm = _pick_tile(rows, (1024, 512, 256, 128, 64, 32))
    tpl = rows // tm
    hq = N_HEADS * HEAD_PAD
    hv = N_HEADS * V_HEAD
    return pl.pallas_call(
        _kvpast_kernel,
        grid=(depth, tpl),
        in_specs=[pl.BlockSpec((None, tm, kv_rank), lambda l, i: (l, i, 0)),
                  pl.BlockSpec((None, tm, rope), lambda l, i: (l, i, 0)),
                  pl.BlockSpec((None, kv_rank, hq), lambda l, i: (l, 0, 0)),
                  pl.BlockSpec((None, kv_rank, hv), lambda l, i: (l, 0, 0)),
                  pl.BlockSpec((rope, hq), lambda l, i: (0, 0))],
        out_specs=[pl.BlockSpec((None, tm, hq), lambda l, i: (l, i, 0)),
                   pl.BlockSpec((None, tm, hv), lambda l, i: (l, i, 0))],
        out_shape=[jax.ShapeDtypeStruct((depth, rows, hq), BF16),
                   jax.ShapeDtypeStruct((depth, rows, hv), BF16)],
        compiler_params=_cparams(("parallel", "parallel")), name="kvpast",
    )(ckv, kr, wuk, wuv, place)


def _attn_prompt_kernel(tq, q_ref, k_ref, vt_ref, bias_ref, o_ref, sa_ref, sb_ref, m_ref, acc_ref):
    qi = pl.program_id(2)
    tk = tq // 2
    heads = range(HEADS_PER_STEP)
    halves = range(2)
    nvb = tk // VT_BLOCK

    def scores(j, s_ref, q_lo=0):
        start = pl.multiple_of(j * tk, tk)
        for h in heads:
            s_ref[h, :, q_lo:] = _dot_nt(k_ref[pl.ds(start, tk), h * HEAD_PAD:(h + 1) * HEAD_PAD],
                                         q_ref[q_lo:, h * HEAD_PAD:(h + 1) * HEAD_PAD])

    def softmax_pv(j, s_ref, mode):
        for h in heads:
            for c in halves:
                if mode[c] == "skip":
                    continue
                i = 2 * h + c
                s = s_ref[h, :, c * tk:(c + 1) * tk]
                if mode[c] == "diag":
                    s = s + bias_ref[...]
                m_old = m_ref[i]
                m_new = jnp.maximum(m_old, jnp.max(s, axis=0, keepdims=True))
                p = jnp.exp2(s - m_new).astype(BF16)
                pv = None
                for b in range(nvb):
                    vth = vt_ref[j * nvb + b, h * VT_ROWS:(h + 1) * VT_ROWS, :]
                    d = _dot(vth, p[b * VT_BLOCK:(b + 1) * VT_BLOCK, :])
                    pv = d if pv is None else pv + d
                acc_ref[i] = jnp.exp2(m_old - m_new) * acc_ref[i] + pv
                m_ref[i] = m_new

    full = ("full", "full")

    def pair(t):
        j = 2 * t
        scores(j + 1, sb_ref)
        softmax_pv(j, sa_ref, full)
        scores(j + 2, sa_ref)
        softmax_pv(j + 1, sb_ref, full)

    def two_pairs(u, carry):
        pair(2 * u)
        pair(2 * u + 1)
        return carry

    m_ref[...] = jnp.full(m_ref.shape, NEG_INF, F32)
    acc_ref[...] = jnp.zeros(acc_ref.shape, F32)
    scores(0, sa_ref)
    lax.fori_loop(0, qi // 2, two_pairs, 0)

    @pl.when(qi % 2 == 1)
    def _():
        pair(qi - 1)

    d0 = 2 * qi
    scores(d0 + 1, sb_ref, q_lo=tk)
    softmax_pv(d0, sa_ref, ("diag", "full"))
    softmax_pv(d0 + 1, sb_ref, ("skip", "diag"))
    for c in halves:
        rows = [acc_ref[2 * h + c] for h in heads]
        rows = [acc[:V_HEAD, :] / acc[V_HEAD:V_HEAD + 1, :] for acc in rows]
        o_ref[c * tk:(c + 1) * tk, :] = jnp.transpose(jnp.concatenate(rows, axis=0)).astype(BF16)


def _attn_prompt_call(q, k, vt, batch, seq):
    tq = _pick_tile(seq, (1024, 512))
    tk = tq // 2
    nq = seq // tq
    hp = N_HEADS // HEADS_PER_STEP
    qw = HEADS_PER_STEP * HEAD_PAD
    vw = HEADS_PER_STEP * V_HEAD
    vr = HEADS_PER_STEP * VT_ROWS
    nvb = seq // VT_BLOCK
    chunk = np.arange(tk) // CHUNK
    bias = np.where(chunk[:, None] <= chunk[None, :], 0.0, NEG_INF).astype(np.float32)
    return pl.pallas_call(
        functools.partial(_attn_prompt_kernel, tq),
        grid=(batch, hp, nq),
        in_specs=[pl.BlockSpec((tq, qw), lambda b, h, i: (b * nq + i, h)),
                  pl.BlockSpec((seq, qw), lambda b, h, i: (b, h)),
                  pl.BlockSpec((nvb, vr, VT_BLOCK), lambda b, h, i: (b, h, 0)),
                  pl.BlockSpec((tk, tk), lambda b, h, i: (0, 0))],
        out_specs=pl.BlockSpec((tq, vw), lambda b, h, i: (b * nq + i, h)),
        out_shape=jax.ShapeDtypeStruct((batch * seq, N_HEADS * V_HEAD), BF16),
        scratch_shapes=[pltpu.VMEM((HEADS_PER_STEP, tk, tq), F32),
                        pltpu.VMEM((HEADS_PER_STEP, tk, tq), F32),
                        pltpu.VMEM((2 * HEADS_PER_STEP, 1, tk), F32),
                        pltpu.VMEM((2 * HEADS_PER_STEP, VT_ROWS, tk), F32)],
        compiler_params=_cparams(("parallel", "parallel", "arbitrary")), name="attn_prompt",
    )(q, k, vt, jnp.asarray(bias))


def _attn_sample_kernel(mask_past, mask_new, past_len,
                        q_ref, kn_ref, vn_ref, kp_ref, vp_ref, o_ref):
    vp = vp_ref[...]
    vn = vn_ref[...]
    outs = []
    for h in range(HEADS_PER_STEP):
        sl = slice(h * HEAD_PAD, (h + 1) * HEAD_PAD)
        qh = q_ref[:, sl]
        sp = _dot_nt(qh, kp_ref[:, sl])
        sn = _dot_nt(qh, kn_ref[:, sl])
        if mask_past:
            qc = (lax.broadcasted_iota(jnp.int32, sp.shape, 0) + past_len) // CHUNK
            kc = lax.broadcasted_iota(jnp.int32, sp.shape, 1) // CHUNK
            sp = jnp.where(kc <= qc, sp, NEG_INF)
        if mask_new:
            qc = (lax.broadcasted_iota(jnp.int32, sn.shape, 0) + past_len) // CHUNK
            kc = (lax.broadcasted_iota(jnp.int32, sn.shape, 1) + past_len) // CHUNK
            sn = jnp.where(kc <= qc, sn, NEG_INF)
        m = jnp.maximum(jnp.max(sp, axis=1, keepdims=True), jnp.max(sn, axis=1, keepdims=True))
        pp = jnp.exp2(sp - m)
        pn = jnp.exp2(sn - m)
        l = jnp.sum(pp, axis=1, keepdims=True) + jnp.sum(pn, axis=1, keepdims=True)
        outs.append((_dot(pp.astype(BF16), vp) + _dot(pn.astype(BF16), vn)) / l)
    lane = lax.broadcasted_iota(jnp.int32, outs[0].shape, 1)
    o_ref[...] = jnp.where(lane < V_HEAD, outs[0], outs[1]).astype(BF16)


def _attn_sample_call(q, k, v, kpast, vpast, layer, row0, n_streams, t_new, past_len):
    hp = N_HEADS // HEADS_PER_STEP
    qw = HEADS_PER_STEP * HEAD_PAD
    vw = HEADS_PER_STEP * V_HEAD
    blk0 = row0 // t_new
    q_pos = past_len + np.arange(t_new)
    mask_past = not bool(np.all((np.arange(past_len) // CHUNK)[None, :] <= (q_pos // CHUNK)[:, None]))
    mask_new = not bool(np.all((q_pos // CHUNK)[None, :] <= (q_pos // CHUNK)[:, None]))
    new_spec = lambda w: pl.BlockSpec((t_new, w), lambda s, h: (blk0 + s, h))
    return pl.pallas_call(
        functools.partial(_attn_sample_kernel, mask_past, mask_new, past_len),
        grid=(n_streams, hp),
        in_specs=[new_spec(qw), new_spec(qw), new_spec(vw),
                  pl.BlockSpec((None, past_len, qw), lambda s, h: (layer * n_streams + s, 0, h)),
                  pl.BlockSpec((None, past_len, vw), lambda s, h: (layer * n_streams + s, 0, h))],
        out_specs=pl.BlockSpec((t_new, vw), lambda s, h: (s, h)),
        out_shape=jax.ShapeDtypeStruct((n_streams * t_new, N_HEADS * V_HEAD), BF16),
        compiler_params=_cparams(("parallel", "parallel")), name="attn_sample",
    )(q, k, v, kpast, vpast)


def _conv_kernel(tc, halo, conv_w, rc, u_ref, h_ref, w_ref, b_ref, g_ref, beta_ref, o_ref, sh_ref):
    sh_ref[0, 0:halo, :] = h_ref[...]
    sh_ref[0, halo:halo + tc, :] = u_ref[...]
    span = halo + tc - SUBLANES
    for r in range(1, SUBLANES):
        sh_ref[r, 0:span, :] = sh_ref[0, r:r + span, :]
    lead = halo - (conv_w - 1)
    for c in range(tc // rc):
        acc = jnp.broadcast_to(b_ref[...], (rc, u_ref.shape[1]))
        for j in range(conv_w):
            r = (lead + j) % SUBLANES
            r0 = c * rc + lead + j - r
            acc = acc + w_ref[j:j + 1, :] * sh_ref[r, r0:r0 + rc, :]
        y = _layernorm(acc, g_ref[...], beta_ref[...])
        o_ref[c * rc:(c + 1) * rc, :] = _silu(y).astype(BF16)


def _conv_call(u, row0, n_tiles, halo_rows, tc, w, b, g, beta):
    c = u.shape[1]
    blk0 = row0 // tc
    halo = halo_rows.shape[0] // n_tiles
    conv_w = w.shape[0]
    assert halo % SUBLANES == 0 and conv_w - 1 <= halo
    rc = min(tc, 64)
    const = lambda i: (0, 0)
    return pl.pallas_call(
        functools.partial(_conv_kernel, tc, halo, conv_w, rc),
        grid=(n_tiles,),
        in_specs=[pl.BlockSpec((tc, c), lambda i: (blk0 + i, 0)),
                  pl.BlockSpec((halo, c), lambda i: (i, 0)),
                  pl.BlockSpec((conv_w, c), const), pl.BlockSpec((1, c), const),
                  pl.BlockSpec((1, c), const), pl.BlockSpec((1, c), const)],
        out_specs=pl.BlockSpec((tc, c), lambda i: (i, 0)),
        out_shape=jax.ShapeDtypeStruct((n_tiles * tc, c), BF16),
        scratch_shapes=[pltpu.VMEM((SUBLANES, halo + tc, c), F32)],
        compiler_params=_cparams(("parallel",)), name="conv",
    )(u, halo_rows, w, b.reshape(1, c), g.reshape(1, c), beta.reshape(1, c))


def _outproj_kernel(alpha, d, p_tiles, x_ref, ap_ref, as_ref, cp_ref, cs_ref, gate_ref,
                    woa_ref, wco_ref, wout_ref, g_ref, b_ref, o_ref):
    is_prompt = pl.program_id(0) < p_tiles
    br_a = _dot(jnp.where(is_prompt, ap_ref[...], as_ref[...]), woa_ref[...])
    br_c = _dot(jnp.where(is_prompt, cp_ref[...], cs_ref[...]), wco_ref[...])
    merged = gate_ref[:, :d].astype(F32) * br_a + gate_ref[:, d:].astype(F32) * br_c
    mix = _dot(merged.astype(BF16), wout_ref[...])
    o_ref[...] = _layernorm(alpha * x_ref[...] + mix, g_ref[...], b_ref[...])


def _outproj_call(x, attn_p, attn_s, cact_p, cact_s, gates, ws, layer, alpha):
    n, d = x.shape
    n_p, n_s = attn_p.shape[0], attn_s.shape[0]
    tm = _pick_tile(math.gcd(n_p, n_s), (512, 256, 128, 64, 32))
    p_tiles = n_p // tm
    names = ("woa", "wco", "wout", "ln1g", "ln1b")
    row = lambda i: (i, 0)
    prow = lambda i: (jnp.minimum(i, p_tiles - 1), 0)
    srow = lambda i: (jnp.maximum(i - p_tiles, 0), 0)
    ha, hc = attn_p.shape[1], cact_p.shape[1]
    return pl.pallas_call(
        functools.partial(_outproj_kernel, alpha, d, p_tiles),
        grid=(n // tm,),
        in_specs=[pl.BlockSpec((tm, d), row), pl.BlockSpec((tm, ha), prow),
                  pl.BlockSpec((tm, ha), srow), pl.BlockSpec((tm, hc), prow),
                  pl.BlockSpec((tm, hc), srow), pl.BlockSpec((tm, 2 * d), row)]
                 + [_layer_spec(ws[k], layer) for k in names],
        out_specs=pl.BlockSpec((tm, d), row),
        out_shape=jax.ShapeDtypeStruct((n, d), F32),
        compiler_params=_cparams(("parallel",)), name="outproj",
    )(x, attn_p, attn_s, cact_p, cact_s, gates, *[ws[k] for k in names])


def _ffn_kernel(alpha, x_ref, wg_ref, wu_ref, wd_ref, g_ref, b_ref, o_ref, acc_ref):
    f = pl.program_id(1)

    @pl.when(f == 0)
    def _():
        acc_ref[...] = jnp.zeros(acc_ref.shape, F32)

    xb = x_ref[...].astype(BF16)
    h = _silu(_dot(xb, wg_ref[...])) * _dot(xb, wu_ref[...])
    acc_ref[...] += _dot(h.astype(BF16), wd_ref[...])

    @pl.when(f == pl.num_programs(1) - 1)
    def _():
        o_ref[...] = _layernorm(alpha * x_ref[...] + acc_ref[...], g_ref[...], b_ref[...])


def _ffn_call(x, wg, wu, wd, li, g, b, alpha):
    n, d = x.shape
    dff = wg.shape[2]
    tm = _pick_tile(n, (512, 256, 128, 64, 32))
    tf = _pick_tile(dff, (1408, 1024, 512, 256, 128))
    return pl.pallas_call(
        functools.partial(_ffn_kernel, alpha),
        grid=(n // tm, dff // tf),
        in_specs=[pl.BlockSpec((tm, d), lambda i, f: (i, 0)),
                  pl.BlockSpec((None, d, tf), lambda i, f: (li, 0, f)),
                  pl.BlockSpec((None, d, tf), lambda i, f: (li, 0, f)),
                  pl.BlockSpec((None, tf, d), lambda i, f: (li, f, 0)),
                  pl.BlockSpec((1, d), lambda i, f: (0, 0)),
                  pl.BlockSpec((1, d), lambda i, f: (0, 0))],
        out_specs=pl.BlockSpec((tm, d), lambda i, f: (i, 0)),
        out_shape=jax.ShapeDtypeStruct((n, d), F32),
        scratch_shapes=[pltpu.VMEM((tm, d), F32)],
        compiler_params=_cparams(("parallel", "arbitrary")), name="ffn",
    )(x, wg, wu, wd, g, b)


def _route_kernel(n_tok, n_exp, tm, x_ref, wr_ref, xb_ref, rank_ref, gate_ref, rankt_ref, cnt_ref):
    i = pl.program_id(0)
    x = x_ref[...]
    row = lax.broadcasted_iota(jnp.int32, (tm, 1), 0) + i * tm
    valid = row < n_tok
    xb_ref[...] = jnp.where(valid, x, 0.0).astype(BF16)
    logits = jnp.dot(x, wr_ref[...], preferred_element_type=F32, precision=lax.Precision.HIGHEST)
    lane = lax.broadcasted_iota(jnp.int32, (tm, LANES), 1)
    logits = jnp.where(lane < n_exp, logits, -jnp.inf)
    m1 = jnp.max(logits, axis=1, keepdims=True)
    i1 = jnp.min(jnp.where(logits == m1, lane, LANES), axis=1, keepdims=True)
    rest = jnp.where(lane == i1, -jnp.inf, logits)
    m2 = jnp.max(rest, axis=1, keepdims=True)
    i2 = jnp.min(jnp.where(rest == m2, lane, LANES), axis=1, keepdims=True)
    e2 = jnp.exp(m2 - m1)
    g1 = 1.0 / (1.0 + e2)
    g2 = e2 / (1.0 + e2)
    sel1 = (lane == i1) & valid
    sel2 = (lane == i2) & valid
    gate_ref[...] = jnp.where(sel1, g1, 0.0) + jnp.where(sel2, g2, 0.0)
    sel = jnp.where(sel1 | sel2, 1.0, 0.0)
    cs = min(tm, 256)
    r = lax.broadcasted_iota(jnp.int32, (cs, cs), 0)
    c = lax.broadcasted_iota(jnp.int32, (cs, cs), 1)
    lower = jnp.where(c < r, 1.0, 0.0).astype(BF16)
    offs = jnp.zeros((1, LANES), F32)
    for ch in range(tm // cs):
        sc = sel[ch * cs:(ch + 1) * cs, :]
        rk = _dot(lower, sc.astype(BF16)) + offs
        rank_ref[ch * cs:(ch + 1) * cs, :] = jnp.where(sc > 0.0, rk, -1.0)
        offs = offs + jnp.sum(sc, axis=0, keepdims=True)
    cnt_ref[...] = offs.astype(jnp.int32)
    rankt_ref[...] = jnp.transpose(rank_ref[...])[:rankt_ref.shape[0], :]


def _route_call(x, wr_pad, n_exp, tm):
    n, d = x.shape
    nt = -(-n // tm)
    return pl.pallas_call(
        functools.partial(_route_kernel, n, n_exp, tm),
        grid=(nt,),
        in_specs=[pl.BlockSpec((tm, d), lambda i: (i, 0)),
                  pl.BlockSpec((d, LANES), lambda i: (0, 0))],
        out_specs=[pl.BlockSpec((tm, d), lambda i: (i, 0)),
                   pl.BlockSpec((tm, LANES), lambda i: (i, 0)),
                   pl.BlockSpec((tm, LANES), lambda i: (i, 0)),
                   pl.BlockSpec((n_exp, tm), lambda i: (0, i)),
                   pl.BlockSpec((None, 1, LANES), lambda i: (i, 0, 0))],
        out_shape=[jax.ShapeDtypeStruct((nt * tm, d), BF16),
                   jax.ShapeDtypeStruct((nt * tm, LANES), F32),
                   jax.ShapeDtypeStruct((nt * tm, LANES), F32),
                   jax.ShapeDtypeStruct((n_exp, nt * tm), F32),
                   jax.ShapeDtypeStruct((nt, 1, LANES), jnp.int32)],
        compiler_params=_cparams(("parallel",)), name="route",
    )(x, wr_pad)


def _moe_kernel(alpha, tm, n256_ref, t128_ref, x_ref, xb_ref, rank_ref, gate_ref, rankt_ref,
                wg_ref, wu_ref, wd_ref, g_ref, b_ref, o_ref, xe_ref, oe_ref, rcol_ref, gcol_ref):
    i = pl.program_id(0)
    e = pl.program_id(1)
    f = pl.program_id(2)
    n_e = pl.num_programs(1)
    nb = n256_ref[i * n_e + e]
    has_half = t128_ref[i * n_e + e] == 1
    half = MOE_ROWS // 2
    half0 = pl.multiple_of(nb * MOE_ROWS, half)

    def for_blocks(fn):
        def body(j, carry):
            fn(pl.multiple_of(j * MOE_ROWS, MOE_ROWS), MOE_ROWS)
            return carry
        lax.fori_loop(0, nb, body, 0)

        @pl.when(has_half)
        def _():
            fn(half0, half)

    @pl.when((e == 0) & (f == 0))
    def _():
        o_ref[...] = jnp.zeros(o_ref.shape, F32)

    @pl.when(f == 0)
    def _():
        lane = lax.broadcasted_iota(jnp.int32, (tm, LANES), 1)
        rcol_ref[...] = jnp.max(jnp.where(lane == e, rank_ref[...], -1.0), axis=1, keepdims=True)
        gcol_ref[...] = jnp.sum(jnp.where(lane == e, gate_ref[...], 0.0), axis=1, keepdims=True)

        def gather(r0, rows):
            rrow = rankt_ref[pl.ds(e, 1), :]
            slot = (lax.broadcasted_iota(jnp.int32, (rows, tm), 0) + r0).astype(F32)
            onehot = jnp.where(rrow == slot, 1.0, 0.0).astype(BF16)
            xe_ref[pl.ds(r0, rows), :] = _dot(onehot, xb_ref[...]).astype(BF16)

        for_blocks(gather)

    def expert(r0, rows):
        xe = xe_ref[pl.ds(r0, rows), :]
        h = _silu(_dot(xe, wg_ref[...])) * _dot(xe, wu_ref[...])
        o = _dot(h.astype(BF16), wd_ref[...])

        @pl.when(f == 0)
        def _():
            oe_ref[pl.ds(r0, rows), :] = o

        @pl.when(f != 0)
        def _():
            oe_ref[pl.ds(r0, rows), :] += o

    for_blocks(expert)

    @pl.when(f == pl.num_programs(2) - 1)
    def _():
        def scatter(r0, rows):
            ob = oe_ref[pl.ds(r0, rows), :].astype(BF16)
            sb = min(tm, MOE_CHUNK)
            for t in range(tm // sb):
                ts = slice(t * sb, (t + 1) * sb)
                slot = (lax.broadcasted_iota(jnp.int32, (sb, rows), 1) + r0).astype(F32)
                onehot_t = jnp.where(rcol_ref[ts, :] == slot, 1.0, 0.0).astype(BF16)
                o_ref[ts, :] += gcol_ref[ts, :] * _dot(onehot_t, ob)

        for_blocks(scatter)

    @pl.when((e == n_e - 1) & (f == pl.num_programs(2) - 1))
    def _():
        rb = min(tm, 256)
        for t in range(tm // rb):
            ts = slice(t * rb, (t + 1) * rb)
            o_ref[ts, :] = _layernorm(alpha * x_ref[ts, :] + o_ref[ts, :], g_ref[...], b_ref[...])


def _moe_call(x, xb, rank, gate, rankt, n256, t128, wg, wu, wd, li, g, b, alpha, tm):
    n_tok, d = x.shape
    nt = xb.shape[0] // tm
    _, n_exp, _, dff = wg.shape
    tf = _pick_tile(dff, (896, 512, 256, 128))
    once = pl.Buffered(1)
    tile = lambda i, e, f, *_: (i, 0)
    const = lambda i, e, f, *_: (0, 0)
    grid_spec = pltpu.PrefetchScalarGridSpec(
        num_scalar_prefetch=2,
        grid=(nt, n_exp, dff // tf),
        in_specs=[pl.BlockSpec((tm, d), tile, pipeline_mode=once),
                  pl.BlockSpec((tm, d), tile, pipeline_mode=once),
                  pl.BlockSpec((tm, LANES), tile, pipeline_mode=once),
                  pl.BlockSpec((tm, LANES), tile, pipeline_mode=once),
                  pl.BlockSpec((n_exp, tm), lambda i, e, f, *_: (0, i), pipeline_mode=once),
                  pl.BlockSpec((None, None, d, tf), lambda i, e, f, *_: (li, e, 0, f)),
                  pl.BlockSpec((None, None, d, tf), lambda i, e, f, *_: (li, e, 0, f)),
                  pl.BlockSpec((None, None, tf, d), lambda i, e, f, *_: (li, e, f, 0)),
                  pl.BlockSpec((1, d), const), pl.BlockSpec((1, d), const)],
        out_specs=pl.BlockSpec((tm, d), tile, pipeline_mode=once),
        scratch_shapes=[pltpu.VMEM((tm, d), BF16), pltpu.VMEM((tm, d), F32),
                        pltpu.VMEM((tm, 1), F32), pltpu.VMEM((tm, 1), F32)],
    )
    return pl.pallas_call(
        functools.partial(_moe_kernel, alpha, tm),
        grid_spec=grid_spec,
        out_shape=jax.ShapeDtypeStruct((n_tok, d), F32),
        compiler_params=_cparams(("parallel", "arbitrary", "arbitrary")), name="moe",
    )(n256, t128, x, xb, rank, gate, rankt, wg, wu, wd, g, b)


def _moe_layer(x, w_router, wg, wu, wd, li, g, b, alpha):
    n, d = x.shape
    n_exp = w_router.shape[1]
    tm = 2048 if n >= 2048 else _pick_tile(n, (1024, 512, 256, 128))
    wr_pad = jnp.zeros((d, LANES), F32).at[:, :n_exp].set(w_router)
    xb, rank, gate, rankt, cnt = _route_call(x, wr_pad, n_exp, tm)
    cnt = cnt[:, 0, :n_exp].reshape(-1)
    rem = cnt % MOE_ROWS
    t128 = ((rem > 0) & (rem <= MOE_ROWS // 2)).astype(jnp.int32)
    n256 = cnt // MOE_ROWS + (rem > MOE_ROWS // 2).astype(jnp.int32)
    return _moe_call(x, xb, rank, gate, rankt, n256, t128, wg, wu, wd, li, g, b, alpha, tm)


def _swap_halves(w):
    half = w.shape[-1] // 2
    return jnp.concatenate([w[..., half:], w[..., :half]], axis=-1)


def _rope_tables(pos):
    inv = ROPE_THETA ** (-jnp.arange(QK_ROPE // 2, dtype=F32) * (2.0 / QK_ROPE))
    ang = pos.astype(F32)[:, None] * inv[None, :]
    cos, sin = jnp.cos(ang), jnp.sin(ang)
    n = pos.shape[0]
    pad = jnp.zeros((n, HEAD_PAD - QK_NOPE - QK_ROPE), F32)
    ct = jnp.concatenate([jnp.ones((n, QK_NOPE), F32), cos, cos, pad], axis=1)
    st = jnp.concatenate([jnp.zeros((n, QK_NOPE), F32), -sin, sin, pad], axis=1)
    return ct, st


def kernel(x_prompt, x_sample, cache_ckv, cache_kr, state_conv, ln_in_g, ln_in_b, w_in, b_in, g_qnorm, w_uq, g_kvnorm, w_uk, w_uv, w_o_attn, conv_w, conv_b, conv_ln_g, conv_ln_b, w_conv_out, w_out, ln1_g, ln1_b, ln2_g, ln2_b, w_ffn_gate, w_ffn_up, w_ffn_down, w_router, w_exp_gate, w_exp_up, w_exp_down):
    batch, seq, d = x_prompt.shape
    n_str, t_new, _ = x_sample.shape
    depth, _, past_len, kv_rank = cache_ckv.shape
    q_rank = g_qnorm.shape[1]
    c_conv = conv_w.shape[2]
    conv_width = conv_w.shape[1]
    halo = 32
    assert conv_width - 1 <= halo and t_new >= conv_width - 1 and t_new % 16 == 0
    assert w_uq.shape[2] == N_HEADS * (QK_NOPE + QK_ROPE) and cache_kr.shape[3] == QK_ROPE
    alpha = (2 * depth) ** 0.25
    n_p = batch * seq
    n_s = n_str * t_new
    n = n_p + n_s

    o1, o2, o3 = q_rank, q_rank + kv_rank, q_rank + kv_rank + QK_ROPE
    o4 = o3 + 2 * c_conv

    def place_kr(w):
        lead = jnp.zeros(w.shape[:-1] + (QK_NOPE,), F32)
        tail = jnp.zeros(w.shape[:-1] + (HEAD_PAD - QK_NOPE - QK_ROPE,), F32)
        return jnp.concatenate([lead, w, tail], axis=-1)

    def split_a(w):
        kr = w[..., o2:o3]
        return jnp.concatenate([w[..., :o2], place_kr(kr), place_kr(_swap_halves(kr))], axis=-1)

    wa = split_a(w_in).astype(BF16)
    ba = split_a(b_in)[:, None, :]
    wb = w_in[..., o3:].astype(BF16)
    bb = b_in[:, None, o3:]
    wq = w_uq.reshape(depth, q_rank, N_HEADS, QK_NOPE + QK_ROPE)
    wq_n, wq_r = wq[..., :QK_NOPE], wq[..., QK_NOPE:]
    z_tail = jnp.zeros(wq_r.shape[:-1] + (HEAD_PAD - QK_NOPE - QK_ROPE,), F32)
    wqa = jnp.concatenate([wq_n, wq_r, z_tail], -1).reshape(depth, q_rank, -1).astype(BF16)
    wqb = jnp.concatenate([jnp.zeros_like(wq_n), _swap_halves(wq_r), z_tail], -1)
    wqb = wqb.reshape(depth, q_rank, -1).astype(BF16)
    wuk = jnp.concatenate([w_uk, jnp.zeros(w_uk.shape[:-1] + (HEAD_PAD - QK_NOPE,), F32)], -1)
    wuk = wuk.reshape(depth, kv_rank, -1).astype(BF16)
    wuv = w_uv.reshape(depth, kv_rank, -1).astype(BF16)
    wuvt = jnp.transpose(w_uv, (0, 2, 3, 1))
    wuvt = jnp.concatenate([wuvt, jnp.zeros((depth, N_HEADS, VT_ROWS - V_HEAD, kv_rank), F32)], 2)
    wuvt = wuvt.reshape(depth, N_HEADS * VT_ROWS, kv_rank).astype(BF16)
    place = jnp.tile(place_kr(jnp.eye(QK_ROPE, dtype=F32)), (1, N_HEADS)).astype(BF16)
    woa = w_o_attn.astype(BF16)
    wco = w_conv_out.astype(BF16)
    wout = w_out.astype(BF16)
    wfg, wfu, wfd = w_ffn_gate.astype(BF16), w_ffn_up.astype(BF16), w_ffn_down.astype(BF16)
    weg, weu, wed = w_exp_gate.astype(BF16), w_exp_up.astype(BF16), w_exp_down.astype(BF16)

    ct_p, st_p = _rope_tables(jnp.arange(seq, dtype=jnp.int32))
    ct_s, st_s = _rope_tables(past_len + jnp.arange(t_new, dtype=jnp.int32))
    tabs = dict(ct=jnp.concatenate([jnp.tile(ct_p, (batch, 1)), jnp.tile(ct_s, (n_str, 1))]),
                st=jnp.concatenate([jnp.tile(st_p, (batch, 1)), jnp.tile(st_s, (n_str, 1))]))

    kpast, vpast = _kvpast_call(cache_ckv.reshape(depth, n_str * past_len, kv_rank),
                                cache_kr.reshape(depth, n_str * past_len, QK_ROPE), wuk, wuv, place)
    kpast = kpast.reshape(depth * n_str, past_len, -1)
    vpast = vpast.reshape(depth * n_str, past_len, -1)

    x = _ln_call(jnp.concatenate([x_prompt.reshape(n_p, d), x_sample.reshape(n_s, d)], axis=0),
                 ln_in_g, ln_in_b)
    tc = _pick_tile(seq, (512, 256, 128, 64))
    ws = dict(wa=wa, ba=ba, wb=wb, bb=bb, gq=g_qnorm[:, None, :], wqa=wqa, wqb=wqb,
              gkv=g_kvnorm[:, None, :], wuk=wuk, wuv=wuv, wuvt=wuvt, woa=woa, wco=wco, wout=wout,
              ln1g=ln1_g[:, None, :], ln1b=ln1_b[:, None, :])
    first_tile = (np.arange(n_p // tc) % (seq // tc) == 0)[:, None, None]
    keep = conv_width - 1
    ckv_l, kr_l, conv_p_l, conv_s_l = [], [], [], []
    for l in range(depth):
        q, k, v, vt, ckv, kr, u, gates = _inproj_call(x, ws, l, tabs, (q_rank, kv_rank, c_conv))
        attn_p = _attn_prompt_call(q, k, vt, batch, seq)
        attn_s = _attn_sample_call(q, k, v, kpast, vpast, l, n_p, n_str, t_new, past_len)
        tails = u[:n_p - tc].reshape(n_p // tc - 1, tc, c_conv)[:, tc - halo:, :]
        prev = jnp.concatenate([jnp.zeros((1, halo, c_conv), F32), tails], axis=0)
        halo_p = jnp.where(first_tile, 0.0, prev).reshape(-1, c_conv)
        halo_s = jnp.concatenate([jnp.zeros((n_str, halo - keep, c_conv), F32),
                                  state_conv[l]], axis=1).reshape(-1, c_conv)
        conv_args = (conv_w[l], conv_b[l], conv_ln_g[l], conv_ln_b[l])
        cact_p = _conv_call(u, 0, n_p // tc, halo_p, tc, *conv_args)
        cact_s = _conv_call(u, n_p, n_str, halo_s, t_new, *conv_args)
        x = _outproj_call(x, attn_p, attn_s, cact_p, cact_s, gates, ws, l, alpha)
        i = l // 2
        if l % 2 == 0:
            x = _ffn_call(x, wfg, wfu, wfd, i, ln2_g[l][None], ln2_b[l][None], alpha)
        else:
            x = _moe_layer(x, w_router[i], weg, weu, wed, i, ln2_g[l][None], ln2_b[l][None], alpha)
        ckv_l.append(ckv)
        kr_l.append(kr[:, QK_NOPE:QK_NOPE + QK_ROPE])
        conv_p_l.append(jnp.stack([u[(b + 1) * seq - keep:(b + 1) * seq] for b in range(batch)]))
        conv_s_l.append(u[n_p:].reshape(n_str, t_new, c_conv)[:, t_new - keep:, :])
    ckv_all = jnp.stack(ckv_l)
    kr_all = jnp.stack(kr_l)
    return (x[:n_p].reshape(batch, seq, d), x[n_p:].reshape(n_str, t_new, d),
            ckv_all[:, :n_p].reshape(depth, batch, seq, kv_rank),
            kr_all[:, :n_p].reshape(depth, batch, seq, QK_ROPE),
            jnp.stack(conv_p_l),
            ckv_all[:, n_p:].reshape(depth, n_str, t_new, kv_rank),
            kr_all[:, n_p:].reshape(depth, n_str, t_new, QK_ROPE),
            jnp.stack(conv_s_l))
```

```python
import functools
import math

import numpy as np
import jax
import jax.numpy as jnp
from jax import lax
from jax.experimental import pallas as pl
from jax.experimental.pallas import tpu as pltpu

CHUNK = 64
N_HEADS = 8
QK_NOPE = 64
QK_ROPE = 32
V_HEAD = 64
ROPE_THETA = 10000.0
TOP_K = 2
LN_EPS = 1e-5
NEG_INF = -1e30
ATTN_SCALE = (QK_NOPE + QK_ROPE) ** -0.5

LANES = 128
SUBLANES = 8
MXU_WIDTH = 256
HEAD_PAD = 128
HEADS_PER_STEP = 2
VT_ROWS = V_HEAD + 16
VT_BLOCK = 256
LOG2E = math.log2(math.e)
MOE_ROWS = 512
MOE_CHUNK = 512
VMEM_LIMIT = 56 * 1024 * 1024

BF16 = jnp.bfloat16
F32 = jnp.float32


def _cparams(sem):
    return pltpu.CompilerParams(dimension_semantics=sem, vmem_limit_bytes=VMEM_LIMIT)


def _pick_tile(n, candidates):
    for c in candidates:
        if n % c == 0:
            return c
    raise ValueError(f"no tile in {candidates} divides {n}")


def _dot(a, b):
    return jnp.dot(a, b, preferred_element_type=F32)


def _dot_nt(a, b):
    return lax.dot_general(a, b, (((1,), (1,)), ((), ())), preferred_element_type=F32)


def _layernorm(x, g, b):
    mu = jnp.mean(x, axis=-1, keepdims=True)
    xc = x - mu
    var = jnp.mean(xc * xc, axis=-1, keepdims=True)
    return xc * lax.rsqrt(var + LN_EPS) * g + b


def _rmsnorm(x, g):
    return x * lax.rsqrt(jnp.mean(x * x, axis=-1, keepdims=True) + LN_EPS) * g


def _silu(x):
    return x * jax.nn.sigmoid(x)


def _ln_kernel(x_ref, g_ref, b_ref, o_ref):
    o_ref[...] = _layernorm(x_ref[...], g_ref[...], b_ref[...])


def _ln_call(x, g, b):
    n, d = x.shape
    tm = _pick_tile(n, (1024, 512, 256, 128, 64, 32))
    return pl.pallas_call(
        _ln_kernel,
        grid=(n // tm,),
        in_specs=[pl.BlockSpec((tm, d), lambda i: (i, 0)),
                  pl.BlockSpec((1, d), lambda i: (0, 0)),
                  pl.BlockSpec((1, d), lambda i: (0, 0))],
        out_specs=pl.BlockSpec((tm, d), lambda i: (i, 0)),
        out_shape=jax.ShapeDtypeStruct((n, d), F32),
        compiler_params=_cparams(("parallel",)),
        name="ln_in",
    )(x, g.reshape(1, d), b.reshape(1, d))


def _inproj_kernel(q_rank, kv_rank, c_conv,
                   x_ref, wa_ref, ba_ref, wb_ref, bb_ref, gq_ref, wqa_ref, wqb_ref, gkv_ref,
                   wuk_ref, wuv_ref, wuvt_ref, ct_ref, st_ref,
                   q_ref, k_ref, v_ref, vt_ref, ckv_ref, kr_ref, u_ref, gate_ref):
    xb = x_ref[...].astype(BF16)
    pa = _dot(xb, wa_ref[...]) + ba_ref[...]
    o1, o2, o3 = q_rank, q_rank + kv_rank, q_rank + kv_rank + HEAD_PAD
    cqn = _rmsnorm(pa[:, :o1], gq_ref[...]).astype(BF16)
    ct = jnp.concatenate([ct_ref[...]] * N_HEADS, axis=1)
    st = jnp.concatenate([st_ref[...]] * N_HEADS, axis=1)
    q = (_dot(cqn, wqa_ref[...]) * ct + _dot(cqn, wqb_ref[...]) * st) * (ATTN_SCALE * LOG2E)
    q_ref[...] = q.astype(BF16)
    ckv = _rmsnorm(pa[:, o1:o2], gkv_ref[...])
    ckv_ref[...] = ckv
    kr = pa[:, o2:o3] * ct_ref[...] + pa[:, o3:] * st_ref[...]
    kr_ref[...] = kr
    ckvb = ckv.astype(BF16)
    k = _dot(ckvb, wuk_ref[...]) + jnp.concatenate([kr] * N_HEADS, axis=1)
    k_ref[...] = k.astype(BF16)
    v_ref[...] = _dot(ckvb, wuv_ref[...]).astype(BF16)
    vt = _dot_nt(wuvt_ref[...], ckvb)
    ones_row = lax.broadcasted_iota(jnp.int32, vt.shape, 0) % VT_ROWS >= V_HEAD
    vt_ref[...] = jnp.where(ones_row, 1.0, vt).astype(BF16)
    pu = _dot(xb, wb_ref[:, :2 * c_conv]) + bb_ref[:, :2 * c_conv]
    u_ref[...] = pu[:, :c_conv] * jax.nn.sigmoid(pu[:, c_conv:])
    pg = _dot(xb, wb_ref[:, 2 * c_conv:]) + bb_ref[:, 2 * c_conv:]
    gate_ref[...] = jax.nn.sigmoid(pg).astype(BF16)


def _layer_spec(w, layer):
    return pl.BlockSpec((None,) + w.shape[1:], lambda *_: (layer, 0, 0))


def _inproj_call(x, ws, layer, tabs, dims):
    n, d = x.shape
    q_rank, kv_rank, c_conv = dims
    tm = VT_BLOCK
    assert n % tm == 0
    hq = N_HEADS * HEAD_PAD
    hv = N_HEADS * V_HEAD
    hvt = N_HEADS * VT_ROWS
    row = lambda i: (i, 0)
    names = ("wa", "ba", "wb", "bb", "gq", "wqa", "wqb", "gkv", "wuk", "wuv", "wuvt")
    in_specs = ([pl.BlockSpec((tm, d), row)] + [_layer_spec(ws[k], layer) for k in names]
                + [pl.BlockSpec((tm, HEAD_PAD), row), pl.BlockSpec((tm, HEAD_PAD), row)])
    out_specs = [
        pl.BlockSpec((tm, hq), row), pl.BlockSpec((tm, hq), row), pl.BlockSpec((tm, hv), row),
        pl.BlockSpec((None, hvt, tm), lambda i: (i, 0, 0)),
        pl.BlockSpec((tm, kv_rank), row), pl.BlockSpec((tm, HEAD_PAD), row),
        pl.BlockSpec((tm, c_conv), row), pl.BlockSpec((tm, 2 * d), row),
    ]
    out_shape = [
        jax.ShapeDtypeStruct((n, hq), BF16), jax.ShapeDtypeStruct((n, hq), BF16),
        jax.ShapeDtypeStruct((n, hv), BF16), jax.ShapeDtypeStruct((n // tm, hvt, tm), BF16),
        jax.ShapeDtypeStruct((n, kv_rank), F32), jax.ShapeDtypeStruct((n, HEAD_PAD), F32),
        jax.ShapeDtypeStruct((n, c_conv), F32), jax.ShapeDtypeStruct((n, 2 * d), BF16),
    ]
    return pl.pallas_call(
        functools.partial(_inproj_kernel, q_rank, kv_rank, c_conv),
        grid=(n // tm,), in_specs=in_specs, out_specs=out_specs, out_shape=out_shape,
        compiler_params=_cparams(("parallel",)), name="inproj",
    )(x, *[ws[k] for k in names], tabs["ct"], tabs["st"])


def _kvpast_kernel(ckv_ref, kr_ref, wuk_ref, wuv_ref, place_ref, k_ref, v_ref):
    ckvb = ckv_ref[...].astype(BF16)
    k = _dot(ckvb, wuk_ref[...]) + _dot(kr_ref[...].astype(BF16), place_ref[...])
    k_ref[...] = k.astype(BF16)
    v_ref[...] = _dot(ckvb, wuv_ref[...]).astype(BF16)


def _kvpast_call(ckv, kr, wuk, wuv, place):
    depth, rows, kv_rank = ckv.shape
    rope = kr.shape[-1]
    tm = _pick_tile(rows, (1024, 512, 256, 128, 64, 32))
    tpl = rows // tm
    hq = N_HEADS * HEAD_PAD
    hv = N_HEADS * V_HEAD
    return pl.pallas_call(
        _kvpast_kernel,
        grid=(depth, tpl),
        in_specs=[pl.BlockSpec((None, tm, kv_rank), lambda l, i: (l, i, 0)),
                  pl.BlockSpec((None, tm, rope), lambda l, i: (l, i, 0)),
                  pl.BlockSpec((None, kv_rank, hq), lambda l, i: (l, 0, 0)),
                  pl.BlockSpec((None, kv_rank, hv), lambda l, i: (l, 0, 0)),
                  pl.BlockSpec((rope, hq), lambda l, i: (0, 0))],
        out_specs=[pl.BlockSpec((None, tm, hq), lambda l, i: (l, i, 0)),
                   pl.BlockSpec((None, tm, hv), lambda l, i: (l, i, 0))],
        out_shape=[jax.ShapeDtypeStruct((depth, rows, hq), BF16),
                   jax.ShapeDtypeStruct((depth, rows, hv), BF16)],
        compiler_params=_cparams(("parallel", "parallel")), name="kvpast",
    )(ckv, kr, wuk, wuv, place)


def _attn_prompt_kernel(tq, q_ref, k_ref, vt_ref, bias_ref, o_ref, sa_ref, sb_ref, m_ref, acc_ref):
    qi = pl.program_id(2)
    tk = tq // 2
    heads = range(HEADS_PER_STEP)
    halves = range(2)
    nvb = tk // VT_BLOCK

    def scores(j, s_ref, q_lo=0):
        start = pl.multiple_of(j * tk, tk)
        for h in heads:
            s_ref[h, :, q_lo:] = _dot_nt(k_ref[pl.ds(start, tk), h * HEAD_PAD:(h + 1) * HEAD_PAD],
                                         q_ref[q_lo:, h * HEAD_PAD:(h + 1) * HEAD_PAD])

    def softmax_pv(j, s_ref, mode):
        for h in heads:
            for c in halves:
                if mode[c] == "skip":
                    continue
                i = 2 * h + c
                s = s_ref[h, :, c * tk:(c + 1) * tk]
                if mode[c] == "diag":
                    s = s + bias_ref[...]
                m_old = m_ref[i]
                m_new = jnp.maximum(m_old, jnp.max(s, axis=0, keepdims=True))
                p = jnp.exp2(s - m_new).astype(BF16)
                pv = None
                for b in range(nvb):
                    vth = vt_ref[j * nvb + b, h * VT_ROWS:(h + 1) * VT_ROWS, :]
                    d = _dot(vth, p[b * VT_BLOCK:(b + 1) * VT_BLOCK, :])
                    pv = d if pv is None else pv + d
                acc_ref[i] = jnp.exp2(m_old - m_new) * acc_ref[i] + pv
                m_ref[i] = m_new

    full = ("full", "full")

    def pair(t):
        j = 2 * t
        scores(j + 1, sb_ref)
        softmax_pv(j, sa_ref, full)
        scores(j + 2, sa_ref)
        softmax_pv(j + 1, sb_ref, full)

    def two_pairs(u, carry):
        pair(2 * u)
        pair(2 * u + 1)
        return carry

    m_ref[...] = jnp.full(m_ref.shape, NEG_INF, F32)
    acc_ref[...] = jnp.zeros(acc_ref.shape, F32)
    scores(0, sa_ref)
    lax.fori_loop(0, qi // 2, two_pairs, 0)

    @pl.when(qi % 2 == 1)
    def _():
        pair(qi - 1)

    d0 = 2 * qi
    scores(d0 + 1, sb_ref, q_lo=tk)
    softmax_pv(d0, sa_ref, ("diag", "full"))
    softmax_pv(d0 + 1, sb_ref, ("skip", "diag"))
    for c in halves:
        rows = [acc_ref[2 * h + c] for h in heads]
        rows = [acc[:V_HEAD, :] / acc[V_HEAD:V_HEAD + 1, :] for acc in rows]
        o_ref[c * tk:(c + 1) * tk, :] = jnp.transpose(jnp.concatenate(rows, axis=0)).astype(BF16)


def _attn_prompt_call(q, k, vt, batch, seq):
    tq = _pick_tile(seq, (1024, 512))
    tk = tq // 2
    nq = seq // tq
    hp = N_HEADS // HEADS_PER_STEP
    qw = HEADS_PER_STEP * HEAD_PAD
    vw = HEADS_PER_STEP * V_HEAD
    vr = HEADS_PER_STEP * VT_ROWS
    nvb = seq // VT_BLOCK
    chunk = np.arange(tk) // CHUNK
    bias = np.where(chunk[:, None] <= chunk[None, :], 0.0, NEG_INF).astype(np.float32)
    return pl.pallas_call(
        functools.partial(_attn_prompt_kernel, tq),
        grid=(batch, hp, nq),
        in_specs=[pl.BlockSpec((tq, qw), lambda b, h, i: (b * nq + i, h)),
                  pl.BlockSpec((seq, qw), lambda b, h, i: (b, h)),
                  pl.BlockSpec((nvb, vr, VT_BLOCK), lambda b, h, i: (b, h, 0)),
                  pl.BlockSpec((tk, tk), lambda b, h, i: (0, 0))],
        out_specs=pl.BlockSpec((tq, vw), lambda b, h, i: (b * nq + i, h)),
        out_shape=jax.ShapeDtypeStruct((batch * seq, N_HEADS * V_HEAD), BF16),
        scratch_shapes=[pltpu.VMEM((HEADS_PER_STEP, tk, tq), F32),
                        pltpu.VMEM((HEADS_PER_STEP, tk, tq), F32),
                        pltpu.VMEM((2 * HEADS_PER_STEP, 1, tk), F32),
                        pltpu.VMEM((2 * HEADS_PER_STEP, VT_ROWS, tk), F32)],
        compiler_params=_cparams(("parallel", "parallel", "arbitrary")), name="attn_prompt",
    )(q, k, vt, jnp.asarray(bias))


def _attn_sample_kernel(mask_past, mask_new, past_len,
                        q_ref, kn_ref, vn_ref, kp_ref, vp_ref, o_ref):
    vp = vp_ref[...]
    vn = vn_ref[...]
    outs = []
    for h in range(HEADS_PER_STEP):
        sl = slice(h * HEAD_PAD, (h + 1) * HEAD_PAD)
        qh = q_ref[:, sl]
        sp = _dot_nt(qh, kp_ref[:, sl])
        sn = _dot_nt(qh, kn_ref[:, sl])
        if mask_past:
            qc = (lax.broadcasted_iota(jnp.int32, sp.shape, 0) + past_len) // CHUNK
            kc = lax.broadcasted_iota(jnp.int32, sp.shape, 1) // CHUNK
            sp = jnp.where(kc <= qc, sp, NEG_INF)
        if mask_new:
            qc = (lax.broadcasted_iota(jnp.int32, sn.shape, 0) + past_len) // CHUNK
            kc = (lax.broadcasted_iota(jnp.int32, sn.shape, 1) + past_len) // CHUNK
            sn = jnp.where(kc <= qc, sn, NEG_INF)
        m = jnp.maximum(jnp.max(sp, axis=1, keepdims=True), jnp.max(sn, axis=1, keepdims=True))
        pp = jnp.exp2(sp - m)
        pn = jnp.exp2(sn - m)
        l = jnp.sum(pp, axis=1, keepdims=True) + jnp.sum(pn, axis=1, keepdims=True)
        outs.append((_dot(pp.astype(BF16), vp) + _dot(pn.astype(BF16), vn)) / l)
    lane = lax.broadcasted_iota(jnp.int32, outs[0].shape, 1)
    o_ref[...] = jnp.where(lane < V_HEAD, outs[0], outs[1]).astype(BF16)


def _attn_sample_call(q, k, v, kpast, vpast, layer, row0, n_streams, t_new, past_len):
    hp = N_HEADS // HEADS_PER_STEP
    qw = HEADS_PER_STEP * HEAD_PAD
    vw = HEADS_PER_STEP * V_HEAD
    blk0 = row0 // t_new
    q_pos = past_len + np.arange(t_new)
    mask_past = not bool(np.all((np.arange(past_len) // CHUNK)[None, :] <= (q_pos // CHUNK)[:, None]))
    mask_new = not bool(np.all((q_pos // CHUNK)[None, :] <= (q_pos // CHUNK)[:, None]))
    new_spec = lambda w: pl.BlockSpec((t_new, w), lambda s, h: (blk0 + s, h))
    return pl.pallas_call(
        functools.partial(_attn_sample_kernel, mask_past, mask_new, past_len),
        grid=(n_streams, hp),
        in_specs=[new_spec(qw), new_spec(qw), new_spec(vw),
                  pl.BlockSpec((None, past_len, qw), lambda s, h: (layer * n_streams + s, 0, h)),
                  pl.BlockSpec((None, past_len, vw), lambda s, h: (layer * n_streams + s, 0, h))],
        out_specs=pl.BlockSpec((t_new, vw), lambda s, h: (s, h)),
        out_shape=jax.ShapeDtypeStruct((n_streams * t_new, N_HEADS * V_HEAD), BF16),
        compiler_params=_cparams(("parallel", "parallel")), name="attn_sample",
    )(q, k, v, kpast, vpast)


def _conv_kernel(tc, halo, conv_w, rc, u_ref, h_ref, w_ref, b_ref, g_ref, beta_ref, o_ref, sh_ref):
    sh_ref[0, 0:halo, :] = h_ref[...]
    sh_ref[0, halo:halo + tc, :] = u_ref[...]
    span = halo + tc - SUBLANES
    for r in range(1, SUBLANES):
        sh_ref[r, 0:span, :] = sh_ref[0, r:r + span, :]
    lead = halo - (conv_w - 1)
    for c in range(tc // rc):
        acc = jnp.broadcast_to(b_ref[...], (rc, u_ref.shape[1]))
        for j in range(conv_w):
            r = (lead + j) % SUBLANES
            r0 = c * rc + lead + j - r
            acc = acc + w_ref[j:j + 1, :] * sh_ref[r, r0:r0 + rc, :]
        y = _layernorm(acc, g_ref[...], beta_ref[...])
        o_ref[c * rc:(c + 1) * rc, :] = _silu(y).astype(BF16)


def _conv_call(u, row0, n_tiles, halo_rows, tc, w, b, g, beta):
    c = u.shape[1]
    blk0 = row0 // tc
    halo = halo_rows.shape[0] // n_tiles
    conv_w = w.shape[0]
    assert halo % SUBLANES == 0 and conv_w - 1 <= halo
    rc = min(tc, 64)
    const = lambda i: (0, 0)
    return pl.pallas_call(
        functools.partial(_conv_kernel, tc, halo, conv_w, rc),
        grid=(n_tiles,),
        in_specs=[pl.BlockSpec((tc, c), lambda i: (blk0 + i, 0)),
                  pl.BlockSpec((halo, c), lambda i: (i, 0)),
                  pl.BlockSpec((conv_w, c), const), pl.BlockSpec((1, c), const),
                  pl.BlockSpec((1, c), const), pl.BlockSpec((1, c), const)],
        out_specs=pl.BlockSpec((tc, c), lambda i: (i, 0)),
        out_shape=jax.ShapeDtypeStruct((n_tiles * tc, c), BF16),
        scratch_shapes=[pltpu.VMEM((SUBLANES, halo + tc, c), F32)],
        compiler_params=_cparams(("parallel",)), name="conv",
    )(u, halo_rows, w, b.reshape(1, c), g.reshape(1, c), beta.reshape(1, c))


def _outproj_kernel(alpha, d, p_tiles, x_ref, ap_ref, as_ref, cp_ref, cs_ref, gate_ref,
                    woa_ref, wco_ref, wout_ref, g_ref, b_ref, o_ref):
    is_prompt = pl.program_id(0) < p_tiles
    br_a = _dot(jnp.where(is_prompt, ap_ref[...], as_ref[...]), woa_ref[...])
    br_c = _dot(jnp.where(is_prompt, cp_ref[...], cs_ref[...]), wco_ref[...])
    merged = gate_ref[:, :d].astype(F32) * br_a + gate_ref[:, d:].astype(F32) * br_c
    mix = _dot(merged.astype(BF16), wout_ref[...])
    o_ref[...] = _layernorm(alpha * x_ref[...] + mix, g_ref[...], b_ref[...])


def _outproj_call(x, attn_p, attn_s, cact_p, cact_s, gates, ws, layer, alpha):
    n, d = x.shape
    n_p, n_s = attn_p.shape[0], attn_s.shape[0]
    tm = _pick_tile(math.gcd(n_p, n_s), (512, 256, 128, 64, 32))
    p_tiles = n_p // tm
    names = ("woa", "wco", "wout", "ln1g", "ln1b")
    row = lambda i: (i, 0)
    prow = lambda i: (jnp.minimum(i, p_tiles - 1), 0)
    srow = lambda i: (jnp.maximum(i - p_tiles, 0), 0)
    ha, hc = attn_p.shape[1], cact_p.shape[1]
    return pl.pallas_call(
        functools.partial(_outproj_kernel, alpha, d, p_tiles),
        grid=(n // tm,),
        in_specs=[pl.BlockSpec((tm, d), row), pl.BlockSpec((tm, ha), prow),
                  pl.BlockSpec((tm, ha), srow), pl.BlockSpec((tm, hc), prow),
                  pl.BlockSpec((tm, hc), srow), pl.BlockSpec((tm, 2 * d), row)]
                 + [_layer_spec(ws[k], layer) for k in names],
        out_specs=pl.BlockSpec((tm, d), row),
        out_shape=jax.ShapeDtypeStruct((n, d), F32),
        compiler_params=_cparams(("parallel",)), name="outproj",
    )(x, attn_p, attn_s, cact_p, cact_s, gates, *[ws[k] for k in names])


def _ffn_kernel(alpha, x_ref, wg_ref, wu_ref, wd_ref, g_ref, b_ref, o_ref, acc_ref):
    f = pl.program_id(1)

    @pl.when(f == 0)
    def _():
        acc_ref[...] = jnp.zeros(acc_ref.shape, F32)

    xb = x_ref[...].astype(BF16)
    h = _silu(_dot(xb, wg_ref[...])) * _dot(xb, wu_ref[...])
    acc_ref[...] += _dot(h.astype(BF16), wd_ref[...])

    @pl.when(f == pl.num_programs(1) - 1)
    def _():
        o_ref[...] = _layernorm(alpha * x_ref[...] + acc_ref[...], g_ref[...], b_ref[...])


def _ffn_call(x, wg, wu, wd, li, g, b, alpha):
    n, d = x.shape
    dff = wg.shape[2]
    tm = _pick_tile(n, (512, 256, 128, 64, 32))
    tf = dff
    once = pl.Buffered(1)
    return pl.pallas_call(
        functools.partial(_ffn_kernel, alpha),
        grid=(n // tm, dff // tf),
        in_specs=[pl.BlockSpec((tm, d), lambda i, f: (i, 0)),
                  pl.BlockSpec((None, d, tf), lambda i, f: (li, 0, f), pipeline_mode=once),
                  pl.BlockSpec((None, d, tf), lambda i, f: (li, 0, f), pipeline_mode=once),
                  pl.BlockSpec((None, tf, d), lambda i, f: (li, f, 0), pipeline_mode=once),
                  pl.BlockSpec((1, d), lambda i, f: (0, 0)),
                  pl.BlockSpec((1, d), lambda i, f: (0, 0))],
        out_specs=pl.BlockSpec((tm, d), lambda i, f: (i, 0)),
        out_shape=jax.ShapeDtypeStruct((n, d), F32),
        scratch_shapes=[pltpu.VMEM((tm, d), F32)],
        compiler_params=_cparams(("parallel", "arbitrary")), name="ffn",
    )(x, wg, wu, wd, g, b)


def _route_kernel(n_tok, n_exp, tm, x_ref, wr_ref, xb_ref, rank_ref, gate_ref, rankt_ref, cnt_ref):
    i = pl.program_id(0)
    x = x_ref[...]
    row = lax.broadcasted_iota(jnp.int32, (tm, 1), 0) + i * tm
    valid = row < n_tok
    xb_ref[...] = jnp.where(valid, x, 0.0).astype(BF16)
    logits = jnp.dot(x, wr_ref[...], preferred_element_type=F32, precision=lax.Precision.HIGHEST)
    lane = lax.broadcasted_iota(jnp.int32, (tm, LANES), 1)
    logits = jnp.where(lane < n_exp, logits, -jnp.inf)
    m1 = jnp.max(logits, axis=1, keepdims=True)
    i1 = jnp.min(jnp.where(logits == m1, lane, LANES), axis=1, keepdims=True)
    rest = jnp.where(lane == i1, -jnp.inf, logits)
    m2 = jnp.max(rest, axis=1, keepdims=True)
    i2 = jnp.min(jnp.where(rest == m2, lane, LANES), axis=1, keepdims=True)
    e2 = jnp.exp(m2 - m1)
    g1 = 1.0 / (1.0 + e2)
    g2 = e2 / (1.0 + e2)
    sel1 = (lane == i1) & valid
    sel2 = (lane == i2) & valid
    gate_ref[...] = jnp.where(sel1, g1, 0.0) + jnp.where(sel2, g2, 0.0)
    sel = jnp.where(sel1 | sel2, 1.0, 0.0)
    cs = min(tm, 256)
    r = lax.broadcasted_iota(jnp.int32, (cs, cs), 0)
    c = lax.broadcasted_iota(jnp.int32, (cs, cs), 1)
    lower = jnp.where(c < r, 1.0, 0.0).astype(BF16)
    offs = jnp.zeros((1, LANES), F32)
    for ch in range(tm // cs):
        sc = sel[ch * cs:(ch + 1) * cs, :]
        rk = _dot(lower, sc.astype(BF16)) + offs
        rank_ref[ch * cs:(ch + 1) * cs, :] = jnp.where(sc > 0.0, rk, -1.0)
        offs = offs + jnp.sum(sc, axis=0, keepdims=True)
    cnt_ref[...] = offs.astype(jnp.int32)
    rankt_ref[...] = jnp.transpose(rank_ref[...])[:rankt_ref.shape[0], :]


def _route_call(x, wr_pad, n_exp, tm):
    n, d = x.shape
    nt = -(-n // tm)
    return pl.pallas_call(
        functools.partial(_route_kernel, n, n_exp, tm),
        grid=(nt,),
        in_specs=[pl.BlockSpec((tm, d), lambda i: (i, 0)),
                  pl.BlockSpec((d, LANES), lambda i: (0, 0))],
        out_specs=[pl.BlockSpec((tm, d), lambda i: (i, 0)),
                   pl.BlockSpec((tm, LANES), lambda i: (i, 0)),
                   pl.BlockSpec((tm, LANES), lambda i: (i, 0)),
                   pl.BlockSpec((n_exp, tm), lambda i: (0, i)),
                   pl.BlockSpec((None, 1, LANES), lambda i: (i, 0, 0))],
        out_shape=[jax.ShapeDtypeStruct((nt * tm, d), BF16),
                   jax.ShapeDtypeStruct((nt * tm, LANES), F32),
                   jax.ShapeDtypeStruct((nt * tm, LANES), F32),
                   jax.ShapeDtypeStruct((n_exp, nt * tm), F32),
                   jax.ShapeDtypeStruct((nt, 1, LANES), jnp.int32)],
        compiler_params=_cparams(("parallel",)), name="route",
    )(x, wr_pad)


def _moe_kernel(tm, nfull_ref, half_ref, quarter_ref, xb_ref, rank_ref, gate_ref,
                rankt_ref, wg_ref, wu_ref, wd_ref, o_ref, xe_ref, oe_ref, rcol_ref, gcol_ref):
    i = pl.program_id(0)
    e = pl.program_id(1)
    f = pl.program_id(2)
    n_e = pl.num_programs(1)
    nb = nfull_ref[i * n_e + e]
    has_half = half_ref[i * n_e + e]
    has_quarter = quarter_ref[i * n_e + e]
    half, quarter = MOE_ROWS // 2, MOE_ROWS // 4
    half0 = pl.multiple_of(nb * MOE_ROWS, half)
    quarter0 = pl.multiple_of(nb * MOE_ROWS + has_half * half, quarter)

    def for_blocks(fn):
        def body(j, carry):
            fn(pl.multiple_of(j * MOE_ROWS, MOE_ROWS), MOE_ROWS)
            return carry
        lax.fori_loop(0, nb, body, 0)

        @pl.when(has_half == 1)
        def _():
            fn(half0, half)

        @pl.when(has_quarter == 1)
        def _():
            fn(quarter0, quarter)

    @pl.when((e == 0) & (f == 0))
    def _():
        o_ref[...] = jnp.zeros(o_ref.shape, F32)

    @pl.when(f == 0)
    def _():
        lane = lax.broadcasted_iota(jnp.int32, (tm, LANES), 1)
        rcol_ref[...] = jnp.max(jnp.where(lane == e, rank_ref[...], -1.0), axis=1, keepdims=True)
        gcol_ref[...] = jnp.sum(jnp.where(lane == e, gate_ref[...], 0.0), axis=1, keepdims=True)

        def gather(r0, rows):
            rrow = rankt_ref[pl.ds(e, 1), :]
            slot = (lax.broadcasted_iota(jnp.int32, (rows, tm), 0) + r0).astype(F32)
            onehot = jnp.where(rrow == slot, 1.0, 0.0).astype(BF16)
            xe_ref[pl.ds(r0, rows), :] = _dot(onehot, xb_ref[...]).astype(BF16)

        for_blocks(gather)

    def expert(r0, rows):
        xe = xe_ref[pl.ds(r0, rows), :]
        h = _silu(_dot(xe, wg_ref[...])) * _dot(xe, wu_ref[...])
        o = _dot(h.astype(BF16), wd_ref[...])

        @pl.when(f == 0)
        def _():
            oe_ref[pl.ds(r0, rows), :] = o

        @pl.when(f != 0)
        def _():
            oe_ref[pl.ds(r0, rows), :] += o

    for_blocks(expert)

    @pl.when(f == pl.num_programs(2) - 1)
    def _():
        def scatter(r0, rows):
            ob = oe_ref[pl.ds(r0, rows), :].astype(BF16)
            sb = min(tm, MOE_CHUNK)
            for t in range(tm // sb):
                ts = slice(t * sb, (t + 1) * sb)
                slot = (lax.broadcasted_iota(jnp.int32, (sb, rows), 1) + r0).astype(F32)
                onehot_t = jnp.where(rcol_ref[ts, :] == slot, 1.0, 0.0).astype(BF16)
                o_ref[ts, :] += gcol_ref[ts, :] * _dot(onehot_t, ob)

        for_blocks(scatter)


def _moe_call(n_tok, xb, rank, gate, rankt, blocks, wg, wu, wd, li, tm):
    d = xb.shape[1]
    nt = xb.shape[0] // tm
    _, n_exp, _, dff = wg.shape
    tf = _pick_tile(dff, (7 * MXU_WIDTH, 2 * MXU_WIDTH, MXU_WIDTH, LANES))
    once = pl.Buffered(1)
    tile = lambda i, e, f, *_: (i, 0)
    grid_spec = pltpu.PrefetchScalarGridSpec(
        num_scalar_prefetch=3,
        grid=(nt, n_exp, dff // tf),
        in_specs=[pl.BlockSpec((tm, d), tile, pipeline_mode=once),
                  pl.BlockSpec((tm, LANES), tile, pipeline_mode=once),
                  pl.BlockSpec((tm, LANES), tile, pipeline_mode=once),
                  pl.BlockSpec((n_exp, tm), lambda i, e, f, *_: (0, i), pipeline_mode=once),
                  pl.BlockSpec((None, None, d, tf), lambda i, e, f, *_: (li, e, 0, f)),
                  pl.BlockSpec((None, None, d, tf), lambda i, e, f, *_: (li, e, 0, f)),
                  pl.BlockSpec((None, None, tf, d), lambda i, e, f, *_: (li, e, f, 0))],
        out_specs=pl.BlockSpec((tm, d), tile, pipeline_mode=once),
        scratch_shapes=[pltpu.VMEM((tm, d), BF16), pltpu.VMEM((tm, d), F32),
                        pltpu.VMEM((tm, 1), F32), pltpu.VMEM((tm, 1), F32)],
    )
    return pl.pallas_call(
        functools.partial(_moe_kernel, tm),
        grid_spec=grid_spec,
        out_shape=jax.ShapeDtypeStruct((n_tok, d), F32),
        compiler_params=_cparams(("parallel", "arbitrary", "arbitrary")), name="moe",
    )(*blocks, xb, rank, gate, rankt, wg, wu, wd)


def _resid_ln_kernel(alpha, x_ref, y_ref, g_ref, b_ref, o_ref):
    o_ref[...] = _layernorm(alpha * x_ref[...] + y_ref[...], g_ref[...], b_ref[...])


def _resid_ln_call(x, y, g, b, alpha):
    n, d = x.shape
    tm = _pick_tile(n, (1024, 512, 256, 128, 64, 32))
    row = lambda i: (i, 0)
    const = lambda i: (0, 0)
    return pl.pallas_call(
        functools.partial(_resid_ln_kernel, alpha),
        grid=(n // tm,),
        in_specs=[pl.BlockSpec((tm, d), row), pl.BlockSpec((tm, d), row),
                  pl.BlockSpec((1, d), const), pl.BlockSpec((1, d), const)],
        out_specs=pl.BlockSpec((tm, d), row),
        out_shape=jax.ShapeDtypeStruct((n, d), F32),
        compiler_params=_cparams(("parallel",)), name="resid_ln",
    )(x, y, g, b)


def _moe_layer(x, w_router, wg, wu, wd, li, g, b, alpha):
    n, d = x.shape
    n_exp = w_router.shape[1]
    tm = 2048 if n >= 2048 else _pick_tile(n, (1024, 512, 256, 128))
    wr_pad = jnp.zeros((d, LANES), F32).at[:, :n_exp].set(w_router)
    xb, rank, gate, rankt, cnt = _route_call(x, wr_pad, n_exp, tm)
    cnt = cnt[:, 0, :n_exp].reshape(-1)
    half, quarter = MOE_ROWS // 2, MOE_ROWS // 4
    rem = cnt % MOE_ROWS
    n_full = cnt // MOE_ROWS + (rem > half + quarter).astype(jnp.int32)
    has_half = ((rem > quarter) & (rem <= half + quarter)).astype(jnp.int32)
    has_quarter = (((rem > 0) & (rem <= quarter)) | ((rem > half) & (rem <= half + quarter)))
    blocks = (n_full, has_half, has_quarter.astype(jnp.int32))
    y = _moe_call(n, xb, rank, gate, rankt, blocks, wg, wu, wd, li, tm)
    return _resid_ln_call(x, y, g, b, alpha)


def _swap_halves(w):
    half = w.shape[-1] // 2
    return jnp.concatenate([w[..., half:], w[..., :half]], axis=-1)


def _rope_tables(pos):
    inv = ROPE_THETA ** (-jnp.arange(QK_ROPE // 2, dtype=F32) * (2.0 / QK_ROPE))
    ang = pos.astype(F32)[:, None] * inv[None, :]
    cos, sin = jnp.cos(ang), jnp.sin(ang)
    n = pos.shape[0]
    pad = jnp.zeros((n, HEAD_PAD - QK_NOPE - QK_ROPE), F32)
    ct = jnp.concatenate([jnp.ones((n, QK_NOPE), F32), cos, cos, pad], axis=1)
    st = jnp.concatenate([jnp.zeros((n, QK_NOPE), F32), -sin, sin, pad], axis=1)
    return ct, st


def kernel(x_prompt, x_sample, cache_ckv, cache_kr, state_conv, ln_in_g, ln_in_b, w_in, b_in, g_qnorm, w_uq, g_kvnorm, w_uk, w_uv, w_o_attn, conv_w, conv_b, conv_ln_g, conv_ln_b, w_conv_out, w_out, ln1_g, ln1_b, ln2_g, ln2_b, w_ffn_gate, w_ffn_up, w_ffn_down, w_router, w_exp_gate, w_exp_up, w_exp_down):
    batch, seq, d = x_prompt.shape
    n_str, t_new, _ = x_sample.shape
    depth, _, past_len, kv_rank = cache_ckv.shape
    q_rank = g_qnorm.shape[1]
    c_conv = conv_w.shape[2]
    conv_width = conv_w.shape[1]
    halo = 32
    assert conv_width - 1 <= halo and t_new >= conv_width - 1 and t_new % 16 == 0
    assert w_uq.shape[2] == N_HEADS * (QK_NOPE + QK_ROPE) and cache_kr.shape[3] == QK_ROPE
    alpha = (2 * depth) ** 0.25
    n_p = batch * seq
    n_s = n_str * t_new
    n = n_p + n_s

    o1, o2, o3 = q_rank, q_rank + kv_rank, q_rank + kv_rank + QK_ROPE
    o4 = o3 + 2 * c_conv

    def place_kr(w):
        lead = jnp.zeros(w.shape[:-1] + (QK_NOPE,), F32)
        tail = jnp.zeros(w.shape[:-1] + (HEAD_PAD - QK_NOPE - QK_ROPE,), F32)
        return jnp.concatenate([lead, w, tail], axis=-1)

    def split_a(w):
        kr = w[..., o2:o3]
        return jnp.concatenate([w[..., :o2], place_kr(kr), place_kr(_swap_halves(kr))], axis=-1)

    wa = split_a(w_in).astype(BF16)
    ba = split_a(b_in)[:, None, :]
    wb = w_in[..., o3:].astype(BF16)
    bb = b_in[:, None, o3:]
    wq = w_uq.reshape(depth, q_rank, N_HEADS, QK_NOPE + QK_ROPE)
    wq_n, wq_r = wq[..., :QK_NOPE], wq[..., QK_NOPE:]
    z_tail = jnp.zeros(wq_r.shape[:-1] + (HEAD_PAD - QK_NOPE - QK_ROPE,), F32)
    wqa = jnp.concatenate([wq_n, wq_r, z_tail], -1).reshape(depth, q_rank, -1).astype(BF16)
    wqb = jnp.concatenate([jnp.zeros_like(wq_n), _swap_halves(wq_r), z_tail], -1)
    wqb = wqb.reshape(depth, q_rank, -1).astype(BF16)
    wuk = jnp.concatenate([w_uk, jnp.zeros(w_uk.shape[:-1] + (HEAD_PAD - QK_NOPE,), F32)], -1)
    wuk = wuk.reshape(depth, kv_rank, -1).astype(BF16)
    wuv = w_uv.reshape(depth, kv_rank, -1).astype(BF16)
    wuvt = jnp.transpose(w_uv, (0, 2, 3, 1))
    wuvt = jnp.concatenate([wuvt, jnp.zeros((depth, N_HEADS, VT_ROWS - V_HEAD, kv_rank), F32)], 2)
    wuvt = wuvt.reshape(depth, N_HEADS * VT_ROWS, kv_rank).astype(BF16)
    place = jnp.tile(place_kr(jnp.eye(QK_ROPE, dtype=F32)), (1, N_HEADS)).astype(BF16)
    woa = w_o_attn.astype(BF16)
    wco = w_conv_out.astype(BF16)
    wout = w_out.astype(BF16)
    wfg, wfu, wfd = w_ffn_gate.astype(BF16), w_ffn_up.astype(BF16), w_ffn_down.astype(BF16)
    weg, weu, wed = w_exp_gate.astype(BF16), w_exp_up.astype(BF16), w_exp_down.astype(BF16)

    ct_p, st_p = _rope_tables(jnp.arange(seq, dtype=jnp.int32))
    ct_s, st_s = _rope_tables(past_len + jnp.arange(t_new, dtype=jnp.int32))
    tabs = dict(ct=jnp.concatenate([jnp.tile(ct_p, (batch, 1)), jnp.tile(ct_s, (n_str, 1))]),
                st=jnp.concatenate([jnp.tile(st_p, (batch, 1)), jnp.tile(st_s, (n_str, 1))]))

    kpast, vpast = _kvpast_call(cache_ckv.reshape(depth, n_str * past_len, kv_rank),
                                cache_kr.reshape(depth, n_str * past_len, QK_ROPE), wuk, wuv, place)
    kpast = kpast.reshape(depth * n_str, past_len, -1)
    vpast = vpast.reshape(depth * n_str, past_len, -1)

    x = _ln_call(jnp.concatenate([x_prompt.reshape(n_p, d), x_sample.reshape(n_s, d)], axis=0),
                 ln_in_g, ln_in_b)
    tc = _pick_tile(seq, (512, 256, 128, 64))
    ws = dict(wa=wa, ba=ba, wb=wb, bb=bb, gq=g_qnorm[:, None, :], wqa=wqa, wqb=wqb,
              gkv=g_kvnorm[:, None, :], wuk=wuk, wuv=wuv, wuvt=wuvt, woa=woa, wco=wco, wout=wout,
              ln1g=ln1_g[:, None, :], ln1b=ln1_b[:, None, :])
    first_tile = (np.arange(n_p // tc) % (seq // tc) == 0)[:, None, None]
    keep = conv_width - 1
    ckv_l, kr_l, conv_p_l, conv_s_l = [], [], [], []
    for l in range(depth):
        q, k, v, vt, ckv, kr, u, gates = _inproj_call(x, ws, l, tabs, (q_rank, kv_rank, c_conv))
        attn_p = _attn_prompt_call(q, k, vt, batch, seq)
        attn_s = _attn_sample_call(q, k, v, kpast, vpast, l, n_p, n_str, t_new, past_len)
        tails = u[:n_p - tc].reshape(n_p // tc - 1, tc, c_conv)[:, tc - halo:, :]
        prev = jnp.concatenate([jnp.zeros((1, halo, c_conv), F32), tails], axis=0)
        halo_p = jnp.where(first_tile, 0.0, prev).reshape(-1, c_conv)
        halo_s = jnp.concatenate([jnp.zeros((n_str, halo - keep, c_conv), F32),
                                  state_conv[l]], axis=1).reshape(-1, c_conv)
        conv_args = (conv_w[l], conv_b[l], conv_ln_g[l], conv_ln_b[l])
        cact_p = _conv_call(u, 0, n_p // tc, halo_p, tc, *conv_args)
        cact_s = _conv_call(u, n_p, n_str, halo_s, t_new, *conv_args)
        x = _outproj_call(x, attn_p, attn_s, cact_p, cact_s, gates, ws, l, alpha)
        i = l // 2
        if l % 2 == 0:
            x = _ffn_call(x, wfg, wfu, wfd, i, ln2_g[l][None], ln2_b[l][None], alpha)
        else:
            x = _moe_layer(x, w_router[i], weg, weu, wed, i, ln2_g[l][None], ln2_b[l][None], alpha)
        ckv_l.append(ckv)
        kr_l.append(kr[:, QK_NOPE:QK_NOPE + QK_ROPE])
        conv_p_l.append(jnp.stack([u[(b + 1) * seq - keep:(b + 1) * seq] for b in range(batch)]))
        conv_s_l.append(u[n_p:].reshape(n_str, t_new, c_conv)[:, t_new - keep:, :])
    ckv_all = jnp.stack(ckv_l)
    kr_all = jnp.stack(kr_l)
    return (x[:n_p].reshape(batch, seq, d), x[n_p:].reshape(n_str, t_new, d),
            ckv_all[:, :n_p].reshape(depth, batch, seq, kv_rank),
            kr_all[:, :n_p].reshape(depth, batch, seq, QK_ROPE),
            jnp.stack(conv_p_l),
            ckv_all[:, n_p:].reshape(depth, n_str, t_new, kv_rank),
            kr_all[:, n_p:].reshape(depth, n_str, t_new, QK_ROPE),
            jnp.stack(conv_s_l))
```

```python
import functools
import math

import numpy as np
import jax
import jax.numpy as jnp
from jax import lax
from jax.experimental import pallas as pl
from jax.experimental.pallas import tpu as pltpu

CHUNK = 64
N_HEADS = 8
QK_NOPE = 64
QK_ROPE = 32
V_HEAD = 64
ROPE_THETA = 10000.0
TOP_K = 2
LN_EPS = 1e-5
NEG_INF = -1e30
ATTN_SCALE = (QK_NOPE + QK_ROPE) ** -0.5

LANES = 128
SUBLANES = 8
MXU_WIDTH = 256
HEAD_PAD = 128
HEADS_PER_STEP = 2
VT_ROWS = V_HEAD + 16
VT_BLOCK = 256
LOG2E = math.log2(math.e)
MOE_ROWS = 512
MOE_CHUNK = 512
VMEM_LIMIT = 56 * 1024 * 1024

BF16 = jnp.bfloat16
F32 = jnp.float32


def _cparams(sem):
    return pltpu.CompilerParams(dimension_semantics=sem, vmem_limit_bytes=VMEM_LIMIT)


def _pick_tile(n, candidates):
    for c in candidates:
        if n % c == 0:
            return c
    raise ValueError(f"no tile in {candidates} divides {n}")


def _dot(a, b):
    return jnp.dot(a, b, preferred_element_type=F32)


def _dot_nt(a, b):
    return lax.dot_general(a, b, (((1,), (1,)), ((), ())), preferred_element_type=F32)


def _layernorm(x, g, b):
    mu = jnp.mean(x, axis=-1, keepdims=True)
    xc = x - mu
    var = jnp.mean(xc * xc, axis=-1, keepdims=True)
    return xc * lax.rsqrt(var + LN_EPS) * g + b


def _rmsnorm(x, g):
    return x * lax.rsqrt(jnp.mean(x * x, axis=-1, keepdims=True) + LN_EPS) * g


def _silu(x):
    return x * jax.nn.sigmoid(x)


def _ln_kernel(p_tiles, xp_ref, xs_ref, g_ref, b_ref, o_ref):
    x = jnp.where(pl.program_id(0) < p_tiles, xp_ref[...], xs_ref[...])
    o_ref[...] = _layernorm(x, g_ref[...], b_ref[...])


def _ln_call(x_p, x_s, g, b):
    (n_p, d), n_s = x_p.shape, x_s.shape[0]
    tm = _pick_tile(math.gcd(n_p, n_s), (1024, 512, 256, 128, 64, 32))
    p_tiles = n_p // tm
    return pl.pallas_call(
        functools.partial(_ln_kernel, p_tiles),
        grid=((n_p + n_s) // tm,),
        in_specs=[pl.BlockSpec((tm, d), lambda i: (jnp.minimum(i, p_tiles - 1), 0)),
                  pl.BlockSpec((tm, d), lambda i: (jnp.maximum(i - p_tiles, 0), 0)),
                  pl.BlockSpec((1, d), lambda i: (0, 0)),
                  pl.BlockSpec((1, d), lambda i: (0, 0))],
        out_specs=pl.BlockSpec((tm, d), lambda i: (i, 0)),
        out_shape=jax.ShapeDtypeStruct((n_p + n_s, d), F32),
        compiler_params=_cparams(("parallel",)),
        name="ln_in",
    )(x_p, x_s, g.reshape(1, d), b.reshape(1, d))


def _inproj_kernel(q_rank, kv_rank, c_conv,
                   x_ref, wa_ref, ba_ref, wb_ref, bb_ref, gq_ref, wqa_ref, wqb_ref, gkv_ref,
                   wuk_ref, wuv_ref, wuvt_ref, ct_ref, st_ref,
                   q_ref, k_ref, v_ref, vt_ref, ckv_ref, kr_ref, u_ref, gate_ref):
    xb = x_ref[...].astype(BF16)
    pa = _dot(xb, wa_ref[...]) + ba_ref[...]
    o1, o2, o3 = q_rank, q_rank + kv_rank, q_rank + kv_rank + HEAD_PAD
    cqn = _rmsnorm(pa[:, :o1], gq_ref[...]).astype(BF16)
    ct = jnp.concatenate([ct_ref[...]] * N_HEADS, axis=1)
    st = jnp.concatenate([st_ref[...]] * N_HEADS, axis=1)
    q = (_dot(cqn, wqa_ref[...]) * ct + _dot(cqn, wqb_ref[...]) * st) * (ATTN_SCALE * LOG2E)
    q_ref[...] = q.astype(BF16)
    ckv = _rmsnorm(pa[:, o1:o2], gkv_ref[...])
    ckv_ref[...] = ckv
    kr = pa[:, o2:o3] * ct_ref[...] + pa[:, o3:] * st_ref[...]
    kr_ref[...] = kr
    ckvb = ckv.astype(BF16)
    k = _dot(ckvb, wuk_ref[...]) + jnp.concatenate([kr] * N_HEADS, axis=1)
    k_ref[...] = k.astype(BF16)
    v_ref[...] = _dot(ckvb, wuv_ref[...]).astype(BF16)
    vt = _dot_nt(wuvt_ref[...], ckvb)
    ones_row = lax.broadcasted_iota(jnp.int32, vt.shape, 0) % VT_ROWS >= V_HEAD
    vt_ref[...] = jnp.where(ones_row, 1.0, vt).astype(BF16)
    pu = _dot(xb, wb_ref[:, :2 * c_conv]) + bb_ref[:, :2 * c_conv]
    u_ref[...] = pu[:, :c_conv] * jax.nn.sigmoid(pu[:, c_conv:])
    pg = _dot(xb, wb_ref[:, 2 * c_conv:]) + bb_ref[:, 2 * c_conv:]
    gate_ref[...] = jax.nn.sigmoid(pg).astype(BF16)


def _layer_spec(w, layer):
    return pl.BlockSpec((None,) + w.shape[1:], lambda *_: (layer, 0, 0))


def _inproj_call(x, ws, layer, tabs, dims):
    n, d = x.shape
    q_rank, kv_rank, c_conv = dims
    tm = VT_BLOCK
    assert n % tm == 0
    hq = N_HEADS * HEAD_PAD
    hv = N_HEADS * V_HEAD
    hvt = N_HEADS * VT_ROWS
    row = lambda i: (i, 0)
    names = ("wa", "ba", "wb", "bb", "gq", "wqa", "wqb", "gkv", "wuk", "wuv", "wuvt")
    in_specs = ([pl.BlockSpec((tm, d), row)] + [_layer_spec(ws[k], layer) for k in names]
                + [pl.BlockSpec((tm, HEAD_PAD), row), pl.BlockSpec((tm, HEAD_PAD), row)])
    out_specs = [
        pl.BlockSpec((tm, hq), row), pl.BlockSpec((tm, hq), row), pl.BlockSpec((tm, hv), row),
        pl.BlockSpec((None, hvt, tm), lambda i: (i, 0, 0)),
        pl.BlockSpec((tm, kv_rank), row), pl.BlockSpec((tm, HEAD_PAD), row),
        pl.BlockSpec((tm, c_conv), row), pl.BlockSpec((tm, 2 * d), row),
    ]
    out_shape = [
        jax.ShapeDtypeStruct((n, hq), BF16), jax.ShapeDtypeStruct((n, hq), BF16),
        jax.ShapeDtypeStruct((n, hv), BF16), jax.ShapeDtypeStruct((n // tm, hvt, tm), BF16),
        jax.ShapeDtypeStruct((n, kv_rank), F32), jax.ShapeDtypeStruct((n, HEAD_PAD), F32),
        jax.ShapeDtypeStruct((n, c_conv), F32), jax.ShapeDtypeStruct((n, 2 * d), BF16),
    ]
    return pl.pallas_call(
        functools.partial(_inproj_kernel, q_rank, kv_rank, c_conv),
        grid=(n // tm,), in_specs=in_specs, out_specs=out_specs, out_shape=out_shape,
        compiler_params=_cparams(("parallel",)), name="inproj",
    )(x, *[ws[k] for k in names], tabs["ct"], tabs["st"])


def _kvpast_kernel(ckv_ref, kr_ref, wuk_ref, wuv_ref, place_ref, k_ref, v_ref):
    ckvb = ckv_ref[...].astype(BF16)
    k = _dot(ckvb, wuk_ref[...]) + _dot(kr_ref[...].astype(BF16), place_ref[...])
    k_ref[...] = k.astype(BF16)
    v_ref[...] = _dot(ckvb, wuv_ref[...]).astype(BF16)


def _kvpast_call(ckv, kr, wuk, wuv, place):
    depth, rows, kv_rank = ckv.shape
    rope = kr.shape[-1]
    tm = _pick_tile(rows, (1024, 512, 256, 128, 64, 32))
    tpl = rows // tm
    hq = N_HEADS * HEAD_PAD
    hv = N_HEADS * V_HEAD
    return pl.pallas_call(
        _kvpast_kernel,
        grid=(depth, tpl),
        in_specs=[pl.BlockSpec((None, tm, kv_rank), lambda l, i: (l, i, 0)),
                  pl.BlockSpec((None, tm, rope), lambda l, i: (l, i, 0)),
                  pl.BlockSpec((None, kv_rank, hq), lambda l, i: (l, 0, 0)),
                  pl.BlockSpec((None, kv_rank, hv), lambda l, i: (l, 0, 0)),
                  pl.BlockSpec((rope, hq), lambda l, i: (0, 0))],
        out_specs=[pl.BlockSpec((None, tm, hq), lambda l, i: (l, i, 0)),
                   pl.BlockSpec((None, tm, hv), lambda l, i: (l, i, 0))],
        out_shape=[jax.ShapeDtypeStruct((depth, rows, hq), BF16),
                   jax.ShapeDtypeStruct((depth, rows, hv), BF16)],
        compiler_params=_cparams(("parallel", "parallel")), name="kvpast",
    )(ckv, kr, wuk, wuv, place)


def _attn_prompt_kernel(tq, q_ref, k_ref, vt_ref, bias_ref, o_ref, sa_ref, sb_ref, m_ref, acc_ref):
    qi = pl.program_id(2)
    tk = tq // 2
    heads = range(HEADS_PER_STEP)
    halves = range(2)
    nvb = tk // VT_BLOCK

    def scores(j, s_ref, q_lo=0):
        start = pl.multiple_of(j * tk, tk)
        for h in heads:
            s_ref[h, :, q_lo:] = _dot_nt(k_ref[pl.ds(start, tk), h * HEAD_PAD:(h + 1) * HEAD_PAD],
                                         q_ref[q_lo:, h * HEAD_PAD:(h + 1) * HEAD_PAD])

    def softmax_pv(j, s_ref, mode):
        for h in heads:
            for c in halves:
                if mode[c] == "skip":
                    continue
                i = 2 * h + c
                s = s_ref[h, :, c * tk:(c + 1) * tk]
                if mode[c] == "diag":
                    s = s + bias_ref[...]
                m_old = m_ref[i]
                m_new = jnp.maximum(m_old, jnp.max(s, axis=0, keepdims=True))
                p = jnp.exp2(s - m_new).astype(BF16)
                pv = None
                for b in range(nvb):
                    vth = vt_ref[j * nvb + b, h * VT_ROWS:(h + 1) * VT_ROWS, :]
                    d = _dot(vth, p[b * VT_BLOCK:(b + 1) * VT_BLOCK, :])
                    pv = d if pv is None else pv + d
                acc_ref[i] = jnp.exp2(m_old - m_new) * acc_ref[i] + pv
                m_ref[i] = m_new

    full = ("full", "full")

    def pair(t):
        j = 2 * t
        scores(j + 1, sb_ref)
        softmax_pv(j, sa_ref, full)
        scores(j + 2, sa_ref)
        softmax_pv(j + 1, sb_ref, full)

    def two_pairs(u, carry):
        pair(2 * u)
        pair(2 * u + 1)
        return carry

    m_ref[...] = jnp.full(m_ref.shape, NEG_INF, F32)
    acc_ref[...] = jnp.zeros(acc_ref.shape, F32)
    scores(0, sa_ref)
    lax.fori_loop(0, qi // 2, two_pairs, 0)

    @pl.when(qi % 2 == 1)
    def _():
        pair(qi - 1)

    d0 = 2 * qi
    scores(d0 + 1, sb_ref, q_lo=tk)
    softmax_pv(d0, sa_ref, ("diag", "full"))
    softmax_pv(d0 + 1, sb_ref, ("skip", "diag"))
    for c in halves:
        rows = [acc_ref[2 * h + c] for h in heads]
        rows = [acc[:V_HEAD, :] / acc[V_HEAD:V_HEAD + 1, :] for acc in rows]
        o_ref[c * tk:(c + 1) * tk, :] = jnp.transpose(jnp.concatenate(rows, axis=0)).astype(BF16)


def _attn_prompt_call(q, k, vt, batch, seq):
    tq = _pick_tile(seq, (1024, 512))
    tk = tq // 2
    nq = seq // tq
    hp = N_HEADS // HEADS_PER_STEP
    qw = HEADS_PER_STEP * HEAD_PAD
    vw = HEADS_PER_STEP * V_HEAD
    vr = HEADS_PER_STEP * VT_ROWS
    nvb = seq // VT_BLOCK
    chunk = np.arange(tk) // CHUNK
    bias = np.where(chunk[:, None] <= chunk[None, :], 0.0, NEG_INF).astype(np.float32)
    return pl.pallas_call(
        functools.partial(_attn_prompt_kernel, tq),
        grid=(batch, hp, nq),
        in_specs=[pl.BlockSpec((tq, qw), lambda b, h, i: (b * nq + i, h)),
                  pl.BlockSpec((seq, qw), lambda b, h, i: (b, h)),
                  pl.BlockSpec((nvb, vr, VT_BLOCK), lambda b, h, i: (b, h, 0)),
                  pl.BlockSpec((tk, tk), lambda b, h, i: (0, 0))],
        out_specs=pl.BlockSpec((tq, vw), lambda b, h, i: (b * nq + i, h)),
        out_shape=jax.ShapeDtypeStruct((batch * seq, N_HEADS * V_HEAD), BF16),
        scratch_shapes=[pltpu.VMEM((HEADS_PER_STEP, tk, tq), F32),
                        pltpu.VMEM((HEADS_PER_STEP, tk, tq), F32),
                        pltpu.VMEM((2 * HEADS_PER_STEP, 1, tk), F32),
                        pltpu.VMEM((2 * HEADS_PER_STEP, VT_ROWS, tk), F32)],
        compiler_params=_cparams(("parallel", "parallel", "arbitrary")), name="attn_prompt",
    )(q, k, vt, jnp.asarray(bias))


def _attn_sample_kernel(mask_past, mask_new, past_len,
                        q_ref, kn_ref, vn_ref, kp_ref, vp_ref, o_ref):
    vp = vp_ref[...]
    vn = vn_ref[...]
    outs = []
    for h in range(HEADS_PER_STEP):
        sl = slice(h * HEAD_PAD, (h + 1) * HEAD_PAD)
        qh = q_ref[:, sl]
        sp = _dot_nt(qh, kp_ref[:, sl])
        sn = _dot_nt(qh, kn_ref[:, sl])
        if mask_past:
            qc = (lax.broadcasted_iota(jnp.int32, sp.shape, 0) + past_len) // CHUNK
            kc = lax.broadcasted_iota(jnp.int32, sp.shape, 1) // CHUNK
            sp = jnp.where(kc <= qc, sp, NEG_INF)
        if mask_new:
            qc = (lax.broadcasted_iota(jnp.int32, sn.shape, 0) + past_len) // CHUNK
            kc = (lax.broadcasted_iota(jnp.int32, sn.shape, 1) + past_len) // CHUNK
            sn = jnp.where(kc <= qc, sn, NEG_INF)
        m = jnp.maximum(jnp.max(sp, axis=1, keepdims=True), jnp.max(sn, axis=1, keepdims=True))
        pp = jnp.exp2(sp - m)
        pn = jnp.exp2(sn - m)
        l = jnp.sum(pp, axis=1, keepdims=True) + jnp.sum(pn, axis=1, keepdims=True)
        outs.append((_dot(pp.astype(BF16), vp) + _dot(pn.astype(BF16), vn)) / l)
    lane = lax.broadcasted_iota(jnp.int32, outs[0].shape, 1)
    o_ref[...] = jnp.where(lane < V_HEAD, outs[0], outs[1]).astype(BF16)


def _attn_sample_call(q, k, v, kpast, vpast, layer, row0, n_streams, t_new, past_len):
    hp = N_HEADS // HEADS_PER_STEP
    qw = HEADS_PER_STEP * HEAD_PAD
    vw = HEADS_PER_STEP * V_HEAD
    blk0 = row0 // t_new
    q_pos = past_len + np.arange(t_new)
    mask_past = not bool(np.all((np.arange(past_len) // CHUNK)[None, :] <= (q_pos // CHUNK)[:, None]))
    mask_new = not bool(np.all((q_pos // CHUNK)[None, :] <= (q_pos // CHUNK)[:, None]))
    new_spec = lambda w: pl.BlockSpec((t_new, w), lambda s, h: (blk0 + s, h))
    return pl.pallas_call(
        functools.partial(_attn_sample_kernel, mask_past, mask_new, past_len),
        grid=(n_streams, hp),
        in_specs=[new_spec(qw), new_spec(qw), new_spec(vw),
                  pl.BlockSpec((None, past_len, qw), lambda s, h: (layer * n_streams + s, 0, h)),
                  pl.BlockSpec((None, past_len, vw), lambda s, h: (layer * n_streams + s, 0, h))],
        out_specs=pl.BlockSpec((t_new, vw), lambda s, h: (s, h)),
        out_shape=jax.ShapeDtypeStruct((n_streams * t_new, N_HEADS * V_HEAD), BF16),
        compiler_params=_cparams(("parallel", "parallel")), name="attn_sample",
    )(q, k, v, kpast, vpast)


def _conv_kernel(tc, halo, conv_w, rc, u_ref, h_ref, w_ref, b_ref, g_ref, beta_ref, o_ref, sh_ref):
    sh_ref[0, 0:halo, :] = h_ref[...]
    sh_ref[0, halo:halo + tc, :] = u_ref[...]
    span = halo + tc - SUBLANES
    for r in range(1, SUBLANES):
        sh_ref[r, 0:span, :] = sh_ref[0, r:r + span, :]
    lead = halo - (conv_w - 1)
    for c in range(tc // rc):
        acc = jnp.broadcast_to(b_ref[...], (rc, u_ref.shape[1]))
        for j in range(conv_w):
            r = (lead + j) % SUBLANES
            r0 = c * rc + lead + j - r
            acc = acc + w_ref[j:j + 1, :] * sh_ref[r, r0:r0 + rc, :]
        y = _layernorm(acc, g_ref[...], beta_ref[...])
        o_ref[c * rc:(c + 1) * rc, :] = _silu(y).astype(BF16)


def _conv_call(u, row0, n_tiles, halo_rows, tc, w, b, g, beta):
    c = u.shape[1]
    blk0 = row0 // tc
    halo = halo_rows.shape[0] // n_tiles
    conv_w = w.shape[0]
    assert halo % SUBLANES == 0 and conv_w - 1 <= halo
    rc = min(tc, 64)
    const = lambda i: (0, 0)
    return pl.pallas_call(
        functools.partial(_conv_kernel, tc, halo, conv_w, rc),
        grid=(n_tiles,),
        in_specs=[pl.BlockSpec((tc, c), lambda i: (blk0 + i, 0)),
                  pl.BlockSpec((halo, c), lambda i: (i, 0)),
                  pl.BlockSpec((conv_w, c), const), pl.BlockSpec((1, c), const),
                  pl.BlockSpec((1, c), const), pl.BlockSpec((1, c), const)],
        out_specs=pl.BlockSpec((tc, c), lambda i: (i, 0)),
        out_shape=jax.ShapeDtypeStruct((n_tiles * tc, c), BF16),
        scratch_shapes=[pltpu.VMEM((SUBLANES, halo + tc, c), F32)],
        compiler_params=_cparams(("parallel",)), name="conv",
    )(u, halo_rows, w, b.reshape(1, c), g.reshape(1, c), beta.reshape(1, c))


def _outproj_kernel(alpha, d, p_tiles, x_ref, ap_ref, as_ref, cp_ref, cs_ref, gate_ref,
                    woa_ref, wco_ref, wout_ref, g_ref, b_ref, o_ref):
    is_prompt = pl.program_id(0) < p_tiles
    br_a = _dot(jnp.where(is_prompt, ap_ref[...], as_ref[...]), woa_ref[...])
    br_c = _dot(jnp.where(is_prompt, cp_ref[...], cs_ref[...]), wco_ref[...])
    merged = gate_ref[:, :d].astype(F32) * br_a + gate_ref[:, d:].astype(F32) * br_c
    mix = _dot(merged.astype(BF16), wout_ref[...])
    o_ref[...] = _layernorm(alpha * x_ref[...] + mix, g_ref[...], b_ref[...])


def _outproj_call(x, attn_p, attn_s, cact_p, cact_s, gates, ws, layer, alpha):
    n, d = x.shape
    n_p, n_s = attn_p.shape[0], attn_s.shape[0]
    tm = _pick_tile(math.gcd(n_p, n_s), (512, 256, 128, 64, 32))
    p_tiles = n_p // tm
    names = ("woa", "wco", "wout", "ln1g", "ln1b")
    row = lambda i: (i, 0)
    prow = lambda i: (jnp.minimum(i, p_tiles - 1), 0)
    srow = lambda i: (jnp.maximum(i - p_tiles, 0), 0)
    ha, hc = attn_p.shape[1], cact_p.shape[1]
    return pl.pallas_call(
        functools.partial(_outproj_kernel, alpha, d, p_tiles),
        grid=(n // tm,),
        in_specs=[pl.BlockSpec((tm, d), row), pl.BlockSpec((tm, ha), prow),
                  pl.BlockSpec((tm, ha), srow), pl.BlockSpec((tm, hc), prow),
                  pl.BlockSpec((tm, hc), srow), pl.BlockSpec((tm, 2 * d), row)]
                 + [_layer_spec(ws[k], layer) for k in names],
        out_specs=pl.BlockSpec((tm, d), row),
        out_shape=jax.ShapeDtypeStruct((n, d), F32),
        compiler_params=_cparams(("parallel",)), name="outproj",
    )(x, attn_p, attn_s, cact_p, cact_s, gates, *[ws[k] for k in names])


def _ffn_kernel(alpha, x_ref, wg_ref, wu_ref, wd_ref, g_ref, b_ref, o_ref, acc_ref):
    f = pl.program_id(1)

    @pl.when(f == 0)
    def _():
        acc_ref[...] = jnp.zeros(acc_ref.shape, F32)

    xb = x_ref[...].astype(BF16)
    h = _silu(_dot(xb, wg_ref[...])) * _dot(xb, wu_ref[...])
    acc_ref[...] += _dot(h.astype(BF16), wd_ref[...])

    @pl.when(f == pl.num_programs(1) - 1)
    def _():
        o_ref[...] = _layernorm(alpha * x_ref[...] + acc_ref[...], g_ref[...], b_ref[...])


def _ffn_call(x, wg, wu, wd, li, g, b, alpha):
    n, d = x.shape
    dff = wg.shape[2]
    tm = _pick_tile(n, (512, 256, 128, 64, 32))
    tf = dff
    once = pl.Buffered(1)
    return pl.pallas_call(
        functools.partial(_ffn_kernel, alpha),
        grid=(n // tm, dff // tf),
        in_specs=[pl.BlockSpec((tm, d), lambda i, f: (i, 0)),
                  pl.BlockSpec((None, d, tf), lambda i, f: (li, 0, f), pipeline_mode=once),
                  pl.BlockSpec((None, d, tf), lambda i, f: (li, 0, f), pipeline_mode=once),
                  pl.BlockSpec((None, tf, d), lambda i, f: (li, f, 0), pipeline_mode=once),
                  pl.BlockSpec((1, d), lambda i, f: (0, 0)),
                  pl.BlockSpec((1, d), lambda i, f: (0, 0))],
        out_specs=pl.BlockSpec((tm, d), lambda i, f: (i, 0)),
        out_shape=jax.ShapeDtypeStruct((n, d), F32),
        scratch_shapes=[pltpu.VMEM((tm, d), F32)],
        compiler_params=_cparams(("parallel", "arbitrary")), name="ffn",
    )(x, wg, wu, wd, g, b)


def _route_kernel(n_tok, n_exp, tm, x_ref, wr_ref, xb_ref, rank_ref, gate_ref, rankt_ref, cnt_ref):
    i = pl.program_id(0)
    x = x_ref[...]
    row = lax.broadcasted_iota(jnp.int32, (tm, 1), 0) + i * tm
    valid = row < n_tok
    xb_ref[...] = jnp.where(valid, x, 0.0).astype(BF16)
    logits = jnp.dot(x, wr_ref[...], preferred_element_type=F32, precision=lax.Precision.HIGHEST)
    lane = lax.broadcasted_iota(jnp.int32, (tm, LANES), 1)
    logits = jnp.where(lane < n_exp, logits, -jnp.inf)
    m1 = jnp.max(logits, axis=1, keepdims=True)
    i1 = jnp.min(jnp.where(logits == m1, lane, LANES), axis=1, keepdims=True)
    rest = jnp.where(lane == i1, -jnp.inf, logits)
    m2 = jnp.max(rest, axis=1, keepdims=True)
    i2 = jnp.min(jnp.where(rest == m2, lane, LANES), axis=1, keepdims=True)
    e2 = jnp.exp(m2 - m1)
    g1 = 1.0 / (1.0 + e2)
    g2 = e2 / (1.0 + e2)
    sel1 = (lane == i1) & valid
    sel2 = (lane == i2) & valid
    gate_ref[...] = jnp.where(sel1, g1, 0.0) + jnp.where(sel2, g2, 0.0)
    sel = jnp.where(sel1 | sel2, 1.0, 0.0)
    cs = min(tm, 256)
    r = lax.broadcasted_iota(jnp.int32, (cs, cs), 0)
    c = lax.broadcasted_iota(jnp.int32, (cs, cs), 1)
    lower = jnp.where(c < r, 1.0, 0.0).astype(BF16)
    offs = jnp.zeros((1, LANES), F32)
    for ch in range(tm // cs):
        sc = sel[ch * cs:(ch + 1) * cs, :]
        rk = _dot(lower, sc.astype(BF16)) + offs
        rank_ref[ch * cs:(ch + 1) * cs, :] = jnp.where(sc > 0.0, rk, -1.0)
        offs = offs + jnp.sum(sc, axis=0, keepdims=True)
    cnt_ref[...] = offs.astype(jnp.int32)
    rankt_ref[...] = jnp.transpose(rank_ref[...])[:rankt_ref.shape[0], :]


def _route_call(x, wr_pad, n_exp, tm):
    n, d = x.shape
    nt = -(-n // tm)
    return pl.pallas_call(
        functools.partial(_route_kernel, n, n_exp, tm),
        grid=(nt,),
        in_specs=[pl.BlockSpec((tm, d), lambda i: (i, 0)),
                  pl.BlockSpec((d, LANES), lambda i: (0, 0))],
        out_specs=[pl.BlockSpec((tm, d), lambda i: (i, 0)),
                   pl.BlockSpec((tm, LANES), lambda i: (i, 0)),
                   pl.BlockSpec((tm, LANES), lambda i: (i, 0)),
                   pl.BlockSpec((n_exp, tm), lambda i: (0, i)),
                   pl.BlockSpec((None, 1, LANES), lambda i: (i, 0, 0))],
        out_shape=[jax.ShapeDtypeStruct((nt * tm, d), BF16),
                   jax.ShapeDtypeStruct((nt * tm, LANES), F32),
                   jax.ShapeDtypeStruct((nt * tm, LANES), F32),
                   jax.ShapeDtypeStruct((n_exp, nt * tm), F32),
                   jax.ShapeDtypeStruct((nt, 1, LANES), jnp.int32)],
        compiler_params=_cparams(("parallel",)), name="route",
    )(x, wr_pad)


def _moe_kernel(tm, nfull_ref, half_ref, quarter_ref, xb_ref, rank_ref, gate_ref,
                rankt_ref, wg_ref, wu_ref, wd_ref, o_ref, xe_ref, oe_ref, rcol_ref, gcol_ref):
    i = pl.program_id(0)
    e = pl.program_id(1)
    f = pl.program_id(2)
    n_e = pl.num_programs(1)
    nb = nfull_ref[i * n_e + e]
    has_half = half_ref[i * n_e + e]
    has_quarter = quarter_ref[i * n_e + e]
    half, quarter = MOE_ROWS // 2, MOE_ROWS // 4
    half0 = pl.multiple_of(nb * MOE_ROWS, half)
    quarter0 = pl.multiple_of(nb * MOE_ROWS + has_half * half, quarter)

    def for_blocks(fn):
        def body(j, carry):
            fn(pl.multiple_of(j * MOE_ROWS, MOE_ROWS), MOE_ROWS)
            return carry
        lax.fori_loop(0, nb, body, 0)

        @pl.when(has_half == 1)
        def _():
            fn(half0, half)

        @pl.when(has_quarter == 1)
        def _():
            fn(quarter0, quarter)

    @pl.when((e == 0) & (f == 0))
    def _():
        o_ref[...] = jnp.zeros(o_ref.shape, F32)

    @pl.when(f == 0)
    def _():
        lane = lax.broadcasted_iota(jnp.int32, (tm, LANES), 1)
        rcol_ref[...] = jnp.max(jnp.where(lane == e, rank_ref[...], -1.0), axis=1, keepdims=True)
        gcol_ref[...] = jnp.sum(jnp.where(lane == e, gate_ref[...], 0.0), axis=1, keepdims=True)

        def gather(r0, rows):
            rrow = rankt_ref[pl.ds(e, 1), :]
            slot = (lax.broadcasted_iota(jnp.int32, (rows, tm), 0) + r0).astype(F32)
            onehot = jnp.where(rrow == slot, 1.0, 0.0).astype(BF16)
            xe_ref[pl.ds(r0, rows), :] = _dot(onehot, xb_ref[...]).astype(BF16)

        for_blocks(gather)

    def expert(r0, rows):
        xe = xe_ref[pl.ds(r0, rows), :]
        h = _silu(_dot(xe, wg_ref[...])) * _dot(xe, wu_ref[...])
        o = _dot(h.astype(BF16), wd_ref[...])

        @pl.when(f == 0)
        def _():
            oe_ref[pl.ds(r0, rows), :] = o

        @pl.when(f != 0)
        def _():
            oe_ref[pl.ds(r0, rows), :] += o

    for_blocks(expert)

    @pl.when(f == pl.num_programs(2) - 1)
    def _():
        def scatter(r0, rows):
            ob = oe_ref[pl.ds(r0, rows), :].astype(BF16)
            sb = min(tm, MOE_CHUNK)
            for t in range(tm // sb):
                ts = slice(t * sb, (t + 1) * sb)
                slot = (lax.broadcasted_iota(jnp.int32, (sb, rows), 1) + r0).astype(F32)
                onehot_t = jnp.where(rcol_ref[ts, :] == slot, 1.0, 0.0).astype(BF16)
                o_ref[ts, :] += gcol_ref[ts, :] * _dot(onehot_t, ob)

        for_blocks(scatter)


def _moe_call(n_tok, xb, rank, gate, rankt, blocks, wg, wu, wd, li, tm):
    d = xb.shape[1]
    nt = xb.shape[0] // tm
    _, n_exp, _, dff = wg.shape
    tf = _pick_tile(dff, (7 * MXU_WIDTH, 2 * MXU_WIDTH, MXU_WIDTH, LANES))
    once = pl.Buffered(1)
    tile = lambda i, e, f, *_: (i, 0)
    grid_spec = pltpu.PrefetchScalarGridSpec(
        num_scalar_prefetch=3,
        grid=(nt, n_exp, dff // tf),
        in_specs=[pl.BlockSpec((tm, d), tile, pipeline_mode=once),
                  pl.BlockSpec((tm, LANES), tile, pipeline_mode=once),
                  pl.BlockSpec((tm, LANES), tile, pipeline_mode=once),
                  pl.BlockSpec((n_exp, tm), lambda i, e, f, *_: (0, i), pipeline_mode=once),
                  pl.BlockSpec((None, None, d, tf), lambda i, e, f, *_: (li, e, 0, f)),
                  pl.BlockSpec((None, None, d, tf), lambda i, e, f, *_: (li, e, 0, f)),
                  pl.BlockSpec((None, None, tf, d), lambda i, e, f, *_: (li, e, f, 0))],
        out_specs=pl.BlockSpec((tm, d), tile, pipeline_mode=once),
        scratch_shapes=[pltpu.VMEM((tm, d), BF16), pltpu.VMEM((tm, d), F32),
                        pltpu.VMEM((tm, 1), F32), pltpu.VMEM((tm, 1), F32)],
    )
    return pl.pallas_call(
        functools.partial(_moe_kernel, tm),
        grid_spec=grid_spec,
        out_shape=jax.ShapeDtypeStruct((n_tok, d), F32),
        compiler_params=_cparams(("parallel", "arbitrary", "arbitrary")), name="moe",
    )(*blocks, xb, rank, gate, rankt, wg, wu, wd)


def _resid_ln_kernel(alpha, x_ref, y_ref, g_ref, b_ref, o_ref):
    o_ref[...] = _layernorm(alpha * x_ref[...] + y_ref[...], g_ref[...], b_ref[...])


def _resid_ln_call(x, y, g, b, alpha):
    n, d = x.shape
    tm = _pick_tile(n, (1024, 512, 256, 128, 64, 32))
    row = lambda i: (i, 0)
    const = lambda i: (0, 0)
    return pl.pallas_call(
        functools.partial(_resid_ln_kernel, alpha),
        grid=(n // tm,),
        in_specs=[pl.BlockSpec((tm, d), row), pl.BlockSpec((tm, d), row),
                  pl.BlockSpec((1, d), const), pl.BlockSpec((1, d), const)],
        out_specs=pl.BlockSpec((tm, d), row),
        out_shape=jax.ShapeDtypeStruct((n, d), F32),
        compiler_params=_cparams(("parallel",)), name="resid_ln",
    )(x, y, g, b)


def _moe_layer(x, w_router, wg, wu, wd, li, g, b, alpha):
    n, d = x.shape
    n_exp = w_router.shape[1]
    tm = 2048 if n >= 2048 else _pick_tile(n, (1024, 512, 256, 128))
    wr_pad = jnp.zeros((d, LANES), F32).at[:, :n_exp].set(w_router)
    xb, rank, gate, rankt, cnt = _route_call(x, wr_pad, n_exp, tm)
    cnt = cnt[:, 0, :n_exp].reshape(-1)
    half, quarter = MOE_ROWS // 2, MOE_ROWS // 4
    rem = cnt % MOE_ROWS
    n_full = cnt // MOE_ROWS + (rem > half + quarter).astype(jnp.int32)
    has_half = ((rem > quarter) & (rem <= half + quarter)).astype(jnp.int32)
    has_quarter = (((rem > 0) & (rem <= quarter)) | ((rem > half) & (rem <= half + quarter)))
    blocks = (n_full, has_half, has_quarter.astype(jnp.int32))
    y = _moe_call(n, xb, rank, gate, rankt, blocks, wg, wu, wd, li, tm)
    return _resid_ln_call(x, y, g, b, alpha)


def _swap_halves(w):
    half = w.shape[-1] // 2
    return jnp.concatenate([w[..., half:], w[..., :half]], axis=-1)


def _rope_tables(pos):
    inv = ROPE_THETA ** (-jnp.arange(QK_ROPE // 2, dtype=F32) * (2.0 / QK_ROPE))
    ang = pos.astype(F32)[:, None] * inv[None, :]
    cos, sin = jnp.cos(ang), jnp.sin(ang)
    n = pos.shape[0]
    pad = jnp.zeros((n, HEAD_PAD - QK_NOPE - QK_ROPE), F32)
    ct = jnp.concatenate([jnp.ones((n, QK_NOPE), F32), cos, cos, pad], axis=1)
    st = jnp.concatenate([jnp.zeros((n, QK_NOPE), F32), -sin, sin, pad], axis=1)
    return ct, st


def kernel(x_prompt, x_sample, cache_ckv, cache_kr, state_conv, ln_in_g, ln_in_b, w_in, b_in, g_qnorm, w_uq, g_kvnorm, w_uk, w_uv, w_o_attn, conv_w, conv_b, conv_ln_g, conv_ln_b, w_conv_out, w_out, ln1_g, ln1_b, ln2_g, ln2_b, w_ffn_gate, w_ffn_up, w_ffn_down, w_router, w_exp_gate, w_exp_up, w_exp_down):
    batch, seq, d = x_prompt.shape
    n_str, t_new, _ = x_sample.shape
    depth, _, past_len, kv_rank = cache_ckv.shape
    q_rank = g_qnorm.shape[1]
    c_conv = conv_w.shape[2]
    conv_width = conv_w.shape[1]
    halo = 32
    assert conv_width - 1 <= halo and t_new >= conv_width - 1 and t_new % 16 == 0
    assert w_uq.shape[2] == N_HEADS * (QK_NOPE + QK_ROPE) and cache_kr.shape[3] == QK_ROPE
    alpha = (2 * depth) ** 0.25
    n_p = batch * seq
    n_s = n_str * t_new
    n = n_p + n_s

    o1, o2, o3 = q_rank, q_rank + kv_rank, q_rank + kv_rank + QK_ROPE
    o4 = o3 + 2 * c_conv

    def place_kr(w):
        lead = jnp.zeros(w.shape[:-1] + (QK_NOPE,), F32)
        tail = jnp.zeros(w.shape[:-1] + (HEAD_PAD - QK_NOPE - QK_ROPE,), F32)
        return jnp.concatenate([lead, w, tail], axis=-1)

    def split_a(w):
        kr = w[..., o2:o3]
        return jnp.concatenate([w[..., :o2], place_kr(kr), place_kr(_swap_halves(kr))], axis=-1)

    wa = split_a(w_in).astype(BF16)
    ba = split_a(b_in)[:, None, :]
    wb = w_in[..., o3:].astype(BF16)
    bb = b_in[:, None, o3:]
    wq = w_uq.reshape(depth, q_rank, N_HEADS, QK_NOPE + QK_ROPE)
    wq_n, wq_r = wq[..., :QK_NOPE], wq[..., QK_NOPE:]
    z_tail = jnp.zeros(wq_r.shape[:-1] + (HEAD_PAD - QK_NOPE - QK_ROPE,), F32)
    wqa = jnp.concatenate([wq_n, wq_r, z_tail], -1).reshape(depth, q_rank, -1).astype(BF16)
    wqb = jnp.concatenate([jnp.zeros_like(wq_n), _swap_halves(wq_r), z_tail], -1)
    wqb = wqb.reshape(depth, q_rank, -1).astype(BF16)
    wuk = jnp.concatenate([w_uk, jnp.zeros(w_uk.shape[:-1] + (HEAD_PAD - QK_NOPE,), F32)], -1)
    wuk = wuk.reshape(depth, kv_rank, -1).astype(BF16)
    wuv = w_uv.reshape(depth, kv_rank, -1).astype(BF16)
    wuvt = jnp.transpose(w_uv, (0, 2, 3, 1))
    wuvt = jnp.concatenate([wuvt, jnp.zeros((depth, N_HEADS, VT_ROWS - V_HEAD, kv_rank), F32)], 2)
    wuvt = wuvt.reshape(depth, N_HEADS * VT_ROWS, kv_rank).astype(BF16)
    place = jnp.tile(place_kr(jnp.eye(QK_ROPE, dtype=F32)), (1, N_HEADS)).astype(BF16)
    woa = w_o_attn.astype(BF16)
    wco = w_conv_out.astype(BF16)
    wout = w_out.astype(BF16)
    wfg, wfu, wfd = w_ffn_gate.astype(BF16), w_ffn_up.astype(BF16), w_ffn_down.astype(BF16)
    weg, weu, wed = w_exp_gate.astype(BF16), w_exp_up.astype(BF16), w_exp_down.astype(BF16)

    ct_p, st_p = _rope_tables(jnp.arange(seq, dtype=jnp.int32))
    ct_s, st_s = _rope_tables(past_len + jnp.arange(t_new, dtype=jnp.int32))
    tabs = dict(ct=jnp.concatenate([jnp.tile(ct_p, (batch, 1)), jnp.tile(ct_s, (n_str, 1))]),
                st=jnp.concatenate([jnp.tile(st_p, (batch, 1)), jnp.tile(st_s, (n_str, 1))]))

    kpast, vpast = _kvpast_call(cache_ckv.reshape(depth, n_str * past_len, kv_rank),
                                cache_kr.reshape(depth, n_str * past_len, QK_ROPE), wuk, wuv, place)
    kpast = kpast.reshape(depth * n_str, past_len, -1)
    vpast = vpast.reshape(depth * n_str, past_len, -1)

    x = _ln_call(x_prompt.reshape(n_p, d), x_sample.reshape(n_s, d), ln_in_g, ln_in_b)
    tc = _pick_tile(seq, (512, 256, 128, 64))
    ws = dict(wa=wa, ba=ba, wb=wb, bb=bb, gq=g_qnorm[:, None, :], wqa=wqa, wqb=wqb,
              gkv=g_kvnorm[:, None, :], wuk=wuk, wuv=wuv, wuvt=wuvt, woa=woa, wco=wco, wout=wout,
              ln1g=ln1_g[:, None, :], ln1b=ln1_b[:, None, :])
    first_tile = (np.arange(n_p // tc) % (seq // tc) == 0)[:, None, None]
    keep = conv_width - 1
    ckv_l, kr_l, conv_p_l, conv_s_l = [], [], [], []
    for l in range(depth):
        q, k, v, vt, ckv, kr, u, gates = _inproj_call(x, ws, l, tabs, (q_rank, kv_rank, c_conv))
        attn_p = _attn_prompt_call(q, k, vt, batch, seq)
        attn_s = _attn_sample_call(q, k, v, kpast, vpast, l, n_p, n_str, t_new, past_len)
        tails = u[:n_p - tc].reshape(n_p // tc - 1, tc, c_conv)[:, tc - halo:, :]
        prev = jnp.concatenate([jnp.zeros((1, halo, c_conv), F32), tails], axis=0)
        halo_p = jnp.where(first_tile, 0.0, prev).reshape(-1, c_conv)
        halo_s = jnp.concatenate([jnp.zeros((n_str, halo - keep, c_conv), F32),
                                  state_conv[l]], axis=1).reshape(-1, c_conv)
        conv_args = (conv_w[l], conv_b[l], conv_ln_g[l], conv_ln_b[l])
        cact_p = _conv_call(u, 0, n_p // tc, halo_p, tc, *conv_args)
        cact_s = _conv_call(u, n_p, n_str, halo_s, t_new, *conv_args)
        x = _outproj_call(x, attn_p, attn_s, cact_p, cact_s, gates, ws, l, alpha)
        i = l // 2
        if l % 2 == 0:
            x = _ffn_call(x, wfg, wfu, wfd, i, ln2_g[l][None], ln2_b[l][None], alpha)
        else:
            x = _moe_layer(x, w_router[i], weg, weu, wed, i, ln2_g[l][None], ln2_b[l][None], alpha)
        ckv_l.append(ckv)
        kr_l.append(kr[:, QK_NOPE:QK_NOPE + QK_ROPE])
        conv_p_l.append(jnp.stack([u[(b + 1) * seq - keep:(b + 1) * seq] for b in range(batch)]))
        conv_s_l.append(u[n_p:].reshape(n_str, t_new, c_conv)[:, t_new - keep:, :])
    def stacked(parts, lo, hi, shape):
        return jnp.stack([p[lo:hi].reshape(shape) for p in parts])

    return (x[:n_p].reshape(batch, seq, d), x[n_p:].reshape(n_str, t_new, d),
            stacked(ckv_l, 0, n_p, (batch, seq, kv_rank)),
            stacked(kr_l, 0, n_p, (batch, seq, QK_ROPE)),
            jnp.stack(conv_p_l),
            stacked(ckv_l, n_p, n, (n_str, t_new, kv_rank)),
            stacked(kr_l, n_p, n, (n_str, t_new, QK_ROPE)),
            jnp.stack(conv_s_l))
```

```python
import functools
import math

import numpy as np
import jax
import jax.numpy as jnp
from jax import lax
from jax.experimental import pallas as pl
from jax.experimental.pallas import tpu as pltpu

CHUNK = 64
N_HEADS = 8
QK_NOPE = 64
QK_ROPE = 32
V_HEAD = 64
ROPE_THETA = 10000.0
TOP_K = 2
LN_EPS = 1e-5
NEG_INF = -1e30
ATTN_SCALE = (QK_NOPE + QK_ROPE) ** -0.5

LANES = 128
SUBLANES = 8
MXU_WIDTH = 256
HEAD_PAD = 128
HEADS_PER_STEP = 2
VT_ROWS = V_HEAD + 16
VT_BLOCK = 256
LOG2E = math.log2(math.e)
MOE_ROWS = 512
MOE_CHUNK = 512
VMEM_LIMIT = 56 * 1024 * 1024

BF16 = jnp.bfloat16
F32 = jnp.float32


def _cparams(sem):
    return pltpu.CompilerParams(dimension_semantics=sem, vmem_limit_bytes=VMEM_LIMIT)


def _pick_tile(n, candidates):
    for c in candidates:
        if n % c == 0:
            return c
    raise ValueError(f"no tile in {candidates} divides {n}")


def _dot(a, b):
    return jnp.dot(a, b, preferred_element_type=F32)


def _dot_nt(a, b):
    return lax.dot_general(a, b, (((1,), (1,)), ((), ())), preferred_element_type=F32)


def _layernorm(x, g, b):
    mu = jnp.mean(x, axis=-1, keepdims=True)
    xc = x - mu
    var = jnp.mean(xc * xc, axis=-1, keepdims=True)
    return xc * lax.rsqrt(var + LN_EPS) * g + b


def _rmsnorm(x, g):
    return x * lax.rsqrt(jnp.mean(x * x, axis=-1, keepdims=True) + LN_EPS) * g


def _silu(x):
    return x * jax.nn.sigmoid(x)


def _ln_kernel(p_tiles, xp_ref, xs_ref, g_ref, b_ref, o_ref):
    x = jnp.where(pl.program_id(0) < p_tiles, xp_ref[...], xs_ref[...])
    o_ref[...] = _layernorm(x, g_ref[...], b_ref[...])


def _ln_call(x_p, x_s, g, b):
    (n_p, d), n_s = x_p.shape, x_s.shape[0]
    tm = _pick_tile(math.gcd(n_p, n_s), (1024, 512, 256, 128, 64, 32))
    p_tiles = n_p // tm
    return pl.pallas_call(
        functools.partial(_ln_kernel, p_tiles),
        grid=((n_p + n_s) // tm,),
        in_specs=[pl.BlockSpec((tm, d), lambda i: (jnp.minimum(i, p_tiles - 1), 0)),
                  pl.BlockSpec((tm, d), lambda i: (jnp.maximum(i - p_tiles, 0), 0)),
                  pl.BlockSpec((1, d), lambda i: (0, 0)),
                  pl.BlockSpec((1, d), lambda i: (0, 0))],
        out_specs=pl.BlockSpec((tm, d), lambda i: (i, 0)),
        out_shape=jax.ShapeDtypeStruct((n_p + n_s, d), F32),
        compiler_params=_cparams(("parallel",)),
        name="ln_in",
    )(x_p, x_s, g.reshape(1, d), b.reshape(1, d))


def _inproj_kernel(q_rank, kv_rank, c_conv,
                   x_ref, wa_ref, ba_ref, wb_ref, bb_ref, gq_ref, wqa_ref, wqb_ref, gkv_ref,
                   wuk_ref, wuv_ref, wuvt_ref, ct_ref, st_ref,
                   q_ref, k_ref, v_ref, vt_ref, ckv_ref, kr_ref, u_ref, gate_ref):
    xb = x_ref[...].astype(BF16)
    pa = _dot(xb, wa_ref[...]) + ba_ref[...]
    o1, o2, o3 = q_rank, q_rank + kv_rank, q_rank + kv_rank + HEAD_PAD
    cqn = _rmsnorm(pa[:, :o1], gq_ref[...]).astype(BF16)
    ct = jnp.concatenate([ct_ref[...]] * N_HEADS, axis=1)
    st = jnp.concatenate([st_ref[...]] * N_HEADS, axis=1)
    q = (_dot(cqn, wqa_ref[...]) * ct + _dot(cqn, wqb_ref[...]) * st) * (ATTN_SCALE * LOG2E)
    q_ref[...] = q.astype(BF16)
    ckv = _rmsnorm(pa[:, o1:o2], gkv_ref[...])
    ckv_ref[...] = ckv
    kr = pa[:, o2:o3] * ct_ref[...] + pa[:, o3:] * st_ref[...]
    kr_ref[...] = kr
    ckvb = ckv.astype(BF16)
    k = _dot(ckvb, wuk_ref[...]) + jnp.concatenate([kr] * N_HEADS, axis=1)
    k_ref[...] = k.astype(BF16)
    v_ref[...] = _dot(ckvb, wuv_ref[...]).astype(BF16)
    vt = _dot_nt(wuvt_ref[...], ckvb)
    ones_row = lax.broadcasted_iota(jnp.int32, vt.shape, 0) % VT_ROWS >= V_HEAD
    vt_ref[...] = jnp.where(ones_row, 1.0, vt).astype(BF16)
    pu = _dot(xb, wb_ref[:, :2 * c_conv]) + bb_ref[:, :2 * c_conv]
    u_ref[...] = pu[:, :c_conv] * jax.nn.sigmoid(pu[:, c_conv:])
    pg = _dot(xb, wb_ref[:, 2 * c_conv:]) + bb_ref[:, 2 * c_conv:]
    gate_ref[...] = jax.nn.sigmoid(pg).astype(BF16)


def _layer_spec(w, layer):
    return pl.BlockSpec((None,) + w.shape[1:], lambda *_: (layer, 0, 0))


def _inproj_call(x, ws, layer, tabs, dims):
    n, d = x.shape
    q_rank, kv_rank, c_conv = dims
    tm = VT_BLOCK
    assert n % tm == 0
    hq = N_HEADS * HEAD_PAD
    hv = N_HEADS * V_HEAD
    hvt = N_HEADS * VT_ROWS
    row = lambda i: (i, 0)
    names = ("wa", "ba", "wb", "bb", "gq", "wqa", "wqb", "gkv", "wuk", "wuv", "wuvt")
    in_specs = ([pl.BlockSpec((tm, d), row)] + [_layer_spec(ws[k], layer) for k in names]
                + [pl.BlockSpec((tm, HEAD_PAD), row), pl.BlockSpec((tm, HEAD_PAD), row)])
    out_specs = [
        pl.BlockSpec((tm, hq), row), pl.BlockSpec((tm, hq), row), pl.BlockSpec((tm, hv), row),
        pl.BlockSpec((None, hvt, tm), lambda i: (i, 0, 0)),
        pl.BlockSpec((tm, kv_rank), row), pl.BlockSpec((tm, HEAD_PAD), row),
        pl.BlockSpec((tm, c_conv), row), pl.BlockSpec((tm, 2 * d), row),
    ]
    out_shape = [
        jax.ShapeDtypeStruct((n, hq), BF16), jax.ShapeDtypeStruct((n, hq), BF16),
        jax.ShapeDtypeStruct((n, hv), BF16), jax.ShapeDtypeStruct((n // tm, hvt, tm), BF16),
        jax.ShapeDtypeStruct((n, kv_rank), F32), jax.ShapeDtypeStruct((n, HEAD_PAD), F32),
        jax.ShapeDtypeStruct((n, c_conv), F32), jax.ShapeDtypeStruct((n, 2 * d), BF16),
    ]
    return pl.pallas_call(
        functools.partial(_inproj_kernel, q_rank, kv_rank, c_conv),
        grid=(n // tm,), in_specs=in_specs, out_specs=out_specs, out_shape=out_shape,
        compiler_params=_cparams(("parallel",)), name="inproj",
    )(x, *[ws[k] for k in names], tabs["ct"], tabs["st"])


def _kvpast_kernel(ckv_ref, kr_ref, wuk_ref, wuv_ref, place_ref, k_ref, v_ref):
    ckvb = ckv_ref[...].astype(BF16)
    k = _dot(ckvb, wuk_ref[...]) + _dot(kr_ref[...].astype(BF16), place_ref[...])
    k_ref[...] = k.astype(BF16)
    v_ref[...] = _dot(ckvb, wuv_ref[...]).astype(BF16)


def _kvpast_call(ckv, kr, wuk, wuv, place):
    depth, rows, kv_rank = ckv.shape
    rope = kr.shape[-1]
    tm = _pick_tile(rows, (1024, 512, 256, 128, 64, 32))
    tpl = rows // tm
    hq = N_HEADS * HEAD_PAD
    hv = N_HEADS * V_HEAD
    return pl.pallas_call(
        _kvpast_kernel,
        grid=(depth, tpl),
        in_specs=[pl.BlockSpec((None, tm, kv_rank), lambda l, i: (l, i, 0)),
                  pl.BlockSpec((None, tm, rope), lambda l, i: (l, i, 0)),
                  pl.BlockSpec((None, kv_rank, hq), lambda l, i: (l, 0, 0)),
                  pl.BlockSpec((None, kv_rank, hv), lambda l, i: (l, 0, 0)),
                  pl.BlockSpec((rope, hq), lambda l, i: (0, 0))],
        out_specs=[pl.BlockSpec((None, tm, hq), lambda l, i: (l, i, 0)),
                   pl.BlockSpec((None, tm, hv), lambda l, i: (l, i, 0))],
        out_shape=[jax.ShapeDtypeStruct((depth, rows, hq), BF16),
                   jax.ShapeDtypeStruct((depth, rows, hv), BF16)],
        compiler_params=_cparams(("parallel", "parallel")), name="kvpast",
    )(ckv, kr, wuk, wuv, place)


def _attn_prompt_kernel(tq, q_ref, k_ref, vt_ref, bias_ref, o_ref, sa_ref, sb_ref, m_ref, acc_ref):
    qi = pl.program_id(2)
    tk = tq // 2
    heads = range(HEADS_PER_STEP)
    halves = range(2)
    nvb = tk // VT_BLOCK

    def scores(j, s_ref, q_lo=0):
        start = pl.multiple_of(j * tk, tk)
        for h in heads:
            s_ref[h, :, q_lo:] = _dot_nt(k_ref[pl.ds(start, tk), h * HEAD_PAD:(h + 1) * HEAD_PAD],
                                         q_ref[q_lo:, h * HEAD_PAD:(h + 1) * HEAD_PAD])

    def softmax_pv(j, s_ref, mode):
        for h in heads:
            for c in halves:
                if mode[c] == "skip":
                    continue
                i = 2 * h + c
                s = s_ref[h, :, c * tk:(c + 1) * tk]
                if mode[c] == "diag":
                    s = s + bias_ref[...]
                m_old = m_ref[i]
                m_new = jnp.maximum(m_old, jnp.max(s, axis=0, keepdims=True))
                p = jnp.exp2(s - m_new).astype(BF16)
                pv = None
                for b in range(nvb):
                    vth = vt_ref[j * nvb + b, h * VT_ROWS:(h + 1) * VT_ROWS, :]
                    d = _dot(vth, p[b * VT_BLOCK:(b + 1) * VT_BLOCK, :])
                    pv = d if pv is None else pv + d
                acc_ref[i] = jnp.exp2(m_old - m_new) * acc_ref[i] + pv
                m_ref[i] = m_new

    full = ("full", "full")

    def pair(t):
        j = 2 * t
        scores(j + 1, sb_ref)
        softmax_pv(j, sa_ref, full)
        scores(j + 2, sa_ref)
        softmax_pv(j + 1, sb_ref, full)

    def two_pairs(u, carry):
        pair(2 * u)
        pair(2 * u + 1)
        return carry

    m_ref[...] = jnp.full(m_ref.shape, NEG_INF, F32)
    acc_ref[...] = jnp.zeros(acc_ref.shape, F32)
    scores(0, sa_ref)
    lax.fori_loop(0, qi // 2, two_pairs, 0)

    @pl.when(qi % 2 == 1)
    def _():
        pair(qi - 1)

    d0 = 2 * qi
    scores(d0 + 1, sb_ref, q_lo=tk)
    softmax_pv(d0, sa_ref, ("diag", "full"))
    softmax_pv(d0 + 1, sb_ref, ("skip", "diag"))
    for c in halves:
        rows = [acc_ref[2 * h + c] for h in heads]
        rows = [acc[:V_HEAD, :] / acc[V_HEAD:V_HEAD + 1, :] for acc in rows]
        o_ref[c * tk:(c + 1) * tk, :] = jnp.transpose(jnp.concatenate(rows, axis=0)).astype(BF16)


def _attn_prompt_call(q, k, vt, batch, seq):
    tq = _pick_tile(seq, (1024, 512))
    tk = tq // 2
    nq = seq // tq
    hp = N_HEADS // HEADS_PER_STEP
    qw = HEADS_PER_STEP * HEAD_PAD
    vw = HEADS_PER_STEP * V_HEAD
    vr = HEADS_PER_STEP * VT_ROWS
    nvb = seq // VT_BLOCK
    chunk = np.arange(tk) // CHUNK
    bias = np.where(chunk[:, None] <= chunk[None, :], 0.0, NEG_INF).astype(np.float32)
    return pl.pallas_call(
        functools.partial(_attn_prompt_kernel, tq),
        grid=(batch, hp, nq),
        in_specs=[pl.BlockSpec((tq, qw), lambda b, h, i: (b * nq + i, h)),
                  pl.BlockSpec((seq, qw), lambda b, h, i: (b, h)),
                  pl.BlockSpec((nvb, vr, VT_BLOCK), lambda b, h, i: (b, h, 0)),
                  pl.BlockSpec((tk, tk), lambda b, h, i: (0, 0))],
        out_specs=pl.BlockSpec((tq, vw), lambda b, h, i: (b * nq + i, h)),
        out_shape=jax.ShapeDtypeStruct((batch * seq, N_HEADS * V_HEAD), BF16),
        scratch_shapes=[pltpu.VMEM((HEADS_PER_STEP, tk, tq), F32),
                        pltpu.VMEM((HEADS_PER_STEP, tk, tq), F32),
                        pltpu.VMEM((2 * HEADS_PER_STEP, 1, tk), F32),
                        pltpu.VMEM((2 * HEADS_PER_STEP, VT_ROWS, tk), F32)],
        compiler_params=_cparams(("parallel", "parallel", "arbitrary")), name="attn_prompt",
    )(q, k, vt, jnp.asarray(bias))


def _attn_sample_kernel(mask_past, mask_new, past_len,
                        q_ref, kn_ref, vn_ref, kp_ref, vp_ref, o_ref):
    vw = HEADS_PER_STEP * V_HEAD
    for hp in range(N_HEADS // HEADS_PER_STEP):
        _attn_sample_pair(mask_past, mask_new, past_len, hp * HEADS_PER_STEP * HEAD_PAD,
                          q_ref, kn_ref, kp_ref, vn_ref[:, hp * vw:(hp + 1) * vw],
                          vp_ref[:, hp * vw:(hp + 1) * vw], o_ref.at[:, hp * vw:(hp + 1) * vw])


def _attn_sample_pair(mask_past, mask_new, past_len, col0, q_ref, kn_ref, kp_ref, vn, vp, o_ref):
    outs = []
    for h in range(HEADS_PER_STEP):
        sl = slice(col0 + h * HEAD_PAD, col0 + (h + 1) * HEAD_PAD)
        qh = q_ref[:, sl]
        sp = _dot_nt(qh, kp_ref[:, sl])
        sn = _dot_nt(qh, kn_ref[:, sl])
        if mask_past:
            qc = (lax.broadcasted_iota(jnp.int32, sp.shape, 0) + past_len) // CHUNK
            kc = lax.broadcasted_iota(jnp.int32, sp.shape, 1) // CHUNK
            sp = jnp.where(kc <= qc, sp, NEG_INF)
        if mask_new:
            qc = (lax.broadcasted_iota(jnp.int32, sn.shape, 0) + past_len) // CHUNK
            kc = (lax.broadcasted_iota(jnp.int32, sn.shape, 1) + past_len) // CHUNK
            sn = jnp.where(kc <= qc, sn, NEG_INF)
        m = jnp.maximum(jnp.max(sp, axis=1, keepdims=True), jnp.max(sn, axis=1, keepdims=True))
        pp = jnp.exp2(sp - m)
        pn = jnp.exp2(sn - m)
        l = jnp.sum(pp, axis=1, keepdims=True) + jnp.sum(pn, axis=1, keepdims=True)
        outs.append((_dot(pp.astype(BF16), vp) + _dot(pn.astype(BF16), vn)) / l)
    lane = lax.broadcasted_iota(jnp.int32, outs[0].shape, 1)
    o_ref[...] = jnp.where(lane < V_HEAD, outs[0], outs[1]).astype(BF16)


def _attn_sample_call(q, k, v, kpast, vpast, layer, row0, n_streams, t_new, past_len):
    qw = N_HEADS * HEAD_PAD
    vw = N_HEADS * V_HEAD
    blk0 = row0 // t_new
    q_pos = past_len + np.arange(t_new)
    mask_past = not bool(np.all((np.arange(past_len) // CHUNK)[None, :] <= (q_pos // CHUNK)[:, None]))
    mask_new = not bool(np.all((q_pos // CHUNK)[None, :] <= (q_pos // CHUNK)[:, None]))
    new_spec = lambda w: pl.BlockSpec((t_new, w), lambda s: (blk0 + s, 0))
    return pl.pallas_call(
        functools.partial(_attn_sample_kernel, mask_past, mask_new, past_len),
        grid=(n_streams,),
        in_specs=[new_spec(qw), new_spec(qw), new_spec(vw),
                  pl.BlockSpec((None, past_len, qw), lambda s: (layer * n_streams + s, 0, 0)),
                  pl.BlockSpec((None, past_len, vw), lambda s: (layer * n_streams + s, 0, 0))],
        out_specs=pl.BlockSpec((t_new, vw), lambda s: (s, 0)),
        out_shape=jax.ShapeDtypeStruct((n_streams * t_new, vw), BF16),
        compiler_params=_cparams(("parallel",)), name="attn_sample",
    )(q, k, v, kpast, vpast)


def _conv_kernel(tc, halo, conv_w, rc, u_ref, h_ref, w_ref, b_ref, g_ref, beta_ref, o_ref, sh_ref):
    sh_ref[0, 0:halo, :] = h_ref[...]
    sh_ref[0, halo:halo + tc, :] = u_ref[...]
    span = halo + tc - SUBLANES
    for r in range(1, SUBLANES):
        sh_ref[r, 0:span, :] = sh_ref[0, r:r + span, :]
    lead = halo - (conv_w - 1)
    for c in range(tc // rc):
        acc = jnp.broadcast_to(b_ref[...], (rc, u_ref.shape[1]))
        for j in range(conv_w):
            r = (lead + j) % SUBLANES
            r0 = c * rc + lead + j - r
            acc = acc + w_ref[j:j + 1, :] * sh_ref[r, r0:r0 + rc, :]
        y = _layernorm(acc, g_ref[...], beta_ref[...])
        o_ref[c * rc:(c + 1) * rc, :] = _silu(y).astype(BF16)


def _conv_call(u, row0, n_tiles, halo_rows, tc, w, b, g, beta):
    c = u.shape[1]
    blk0 = row0 // tc
    halo = halo_rows.shape[0] // n_tiles
    conv_w = w.shape[0]
    assert halo % SUBLANES == 0 and conv_w - 1 <= halo
    rc = min(tc, 64)
    const = lambda i: (0, 0)
    return pl.pallas_call(
        functools.partial(_conv_kernel, tc, halo, conv_w, rc),
        grid=(n_tiles,),
        in_specs=[pl.BlockSpec((tc, c), lambda i: (blk0 + i, 0)),
                  pl.BlockSpec((halo, c), lambda i: (i, 0)),
                  pl.BlockSpec((conv_w, c), const), pl.BlockSpec((1, c), const),
                  pl.BlockSpec((1, c), const), pl.BlockSpec((1, c), const)],
        out_specs=pl.BlockSpec((tc, c), lambda i: (i, 0)),
        out_shape=jax.ShapeDtypeStruct((n_tiles * tc, c), BF16),
        scratch_shapes=[pltpu.VMEM((SUBLANES, halo + tc, c), F32)],
        compiler_params=_cparams(("parallel",)), name="conv",
    )(u, halo_rows, w, b.reshape(1, c), g.reshape(1, c), beta.reshape(1, c))


def _outproj_kernel(alpha, d, p_tiles, x_ref, ap_ref, as_ref, cp_ref, cs_ref, gate_ref,
                    woa_ref, wco_ref, wout_ref, g_ref, b_ref, o_ref):
    is_prompt = pl.program_id(0) < p_tiles
    br_a = _dot(jnp.where(is_prompt, ap_ref[...], as_ref[...]), woa_ref[...])
    br_c = _dot(jnp.where(is_prompt, cp_ref[...], cs_ref[...]), wco_ref[...])
    merged = gate_ref[:, :d].astype(F32) * br_a + gate_ref[:, d:].astype(F32) * br_c
    mix = _dot(merged.astype(BF16), wout_ref[...])
    o_ref[...] = _layernorm(alpha * x_ref[...] + mix, g_ref[...], b_ref[...])


def _outproj_call(x, attn_p, attn_s, cact_p, cact_s, gates, ws, layer, alpha):
    n, d = x.shape
    n_p, n_s = attn_p.shape[0], attn_s.shape[0]
    tm = _pick_tile(math.gcd(n_p, n_s), (512, 256, 128, 64, 32))
    p_tiles = n_p // tm
    names = ("woa", "wco", "wout", "ln1g", "ln1b")
    row = lambda i: (i, 0)
    prow = lambda i: (jnp.minimum(i, p_tiles - 1), 0)
    srow = lambda i: (jnp.maximum(i - p_tiles, 0), 0)
    ha, hc = attn_p.shape[1], cact_p.shape[1]
    return pl.pallas_call(
        functools.partial(_outproj_kernel, alpha, d, p_tiles),
        grid=(n // tm,),
        in_specs=[pl.BlockSpec((tm, d), row), pl.BlockSpec((tm, ha), prow),
                  pl.BlockSpec((tm, ha), srow), pl.BlockSpec((tm, hc), prow),
                  pl.BlockSpec((tm, hc), srow), pl.BlockSpec((tm, 2 * d), row)]
                 + [_layer_spec(ws[k], layer) for k in names],
        out_specs=pl.BlockSpec((tm, d), row),
        out_shape=jax.ShapeDtypeStruct((n, d), F32),
        compiler_params=_cparams(("parallel",)), name="outproj",
    )(x, attn_p, attn_s, cact_p, cact_s, gates, *[ws[k] for k in names])


def _ffn_kernel(alpha, x_ref, wg_ref, wu_ref, wd_ref, g_ref, b_ref, o_ref, acc_ref):
    f = pl.program_id(1)

    @pl.when(f == 0)
    def _():
        acc_ref[...] = jnp.zeros(acc_ref.shape, F32)

    xb = x_ref[...].astype(BF16)
    h = _silu(_dot(xb, wg_ref[...])) * _dot(xb, wu_ref[...])
    acc_ref[...] += _dot(h.astype(BF16), wd_ref[...])

    @pl.when(f == pl.num_programs(1) - 1)
    def _():
        o_ref[...] = _layernorm(alpha * x_ref[...] + acc_ref[...], g_ref[...], b_ref[...])


def _ffn_call(x, wg, wu, wd, li, g, b, alpha):
    n, d = x.shape
    dff = wg.shape[2]
    tm = _pick_tile(n, (512, 256, 128, 64, 32))
    tf = dff
    once = pl.Buffered(1)
    return pl.pallas_call(
        functools.partial(_ffn_kernel, alpha),
        grid=(n // tm, dff // tf),
        in_specs=[pl.BlockSpec((tm, d), lambda i, f: (i, 0)),
                  pl.BlockSpec((None, d, tf), lambda i, f: (li, 0, f), pipeline_mode=once),
                  pl.BlockSpec((None, d, tf), lambda i, f: (li, 0, f), pipeline_mode=once),
                  pl.BlockSpec((None, tf, d), lambda i, f: (li, f, 0), pipeline_mode=once),
                  pl.BlockSpec((1, d), lambda i, f: (0, 0)),
                  pl.BlockSpec((1, d), lambda i, f: (0, 0))],
        out_specs=pl.BlockSpec((tm, d), lambda i, f: (i, 0)),
        out_shape=jax.ShapeDtypeStruct((n, d), F32),
        scratch_shapes=[pltpu.VMEM((tm, d), F32)],
        compiler_params=_cparams(("parallel", "arbitrary")), name="ffn",
    )(x, wg, wu, wd, g, b)


def _route_kernel(n_tok, n_exp, tm, x_ref, wr_ref, xb_ref, rank_ref, gate_ref, rankt_ref, cnt_ref):
    i = pl.program_id(0)
    x = x_ref[...]
    row = lax.broadcasted_iota(jnp.int32, (tm, 1), 0) + i * tm
    valid = row < n_tok
    xb_ref[...] = jnp.where(valid, x, 0.0).astype(BF16)
    logits = jnp.dot(x, wr_ref[...], preferred_element_type=F32, precision=lax.Precision.HIGHEST)
    lane = lax.broadcasted_iota(jnp.int32, (tm, LANES), 1)
    logits = jnp.where(lane < n_exp, logits, -jnp.inf)
    m1 = jnp.max(logits, axis=1, keepdims=True)
    i1 = jnp.min(jnp.where(logits == m1, lane, LANES), axis=1, keepdims=True)
    rest = jnp.where(lane == i1, -jnp.inf, logits)
    m2 = jnp.max(rest, axis=1, keepdims=True)
    i2 = jnp.min(jnp.where(rest == m2, lane, LANES), axis=1, keepdims=True)
    e2 = jnp.exp(m2 - m1)
    g1 = 1.0 / (1.0 + e2)
    g2 = e2 / (1.0 + e2)
    sel1 = (lane == i1) & valid
    sel2 = (lane == i2) & valid
    gate_ref[...] = jnp.where(sel1, g1, 0.0) + jnp.where(sel2, g2, 0.0)
    sel = jnp.where(sel1 | sel2, 1.0, 0.0)
    cs = min(tm, 256)
    r = lax.broadcasted_iota(jnp.int32, (cs, cs), 0)
    c = lax.broadcasted_iota(jnp.int32, (cs, cs), 1)
    lower = jnp.where(c < r, 1.0, 0.0).astype(BF16)
    offs = jnp.zeros((1, LANES), F32)
    for ch in range(tm // cs):
        sc = sel[ch * cs:(ch + 1) * cs, :]
        rk = _dot(lower, sc.astype(BF16)) + offs
        rank_ref[ch * cs:(ch + 1) * cs, :] = jnp.where(sc > 0.0, rk, -1.0)
        offs = offs + jnp.sum(sc, axis=0, keepdims=True)
    cnt_ref[...] = offs.astype(jnp.int32)
    rankt_ref[...] = jnp.transpose(rank_ref[...])[:rankt_ref.shape[0], :]


def _route_call(x, wr_pad, n_exp, tm):
    n, d = x.shape
    nt = -(-n // tm)
    return pl.pallas_call(
        functools.partial(_route_kernel, n, n_exp, tm),
        grid=(nt,),
        in_specs=[pl.BlockSpec((tm, d), lambda i: (i, 0)),
                  pl.BlockSpec((d, LANES), lambda i: (0, 0))],
        out_specs=[pl.BlockSpec((tm, d), lambda i: (i, 0)),
                   pl.BlockSpec((tm, LANES), lambda i: (i, 0)),
                   pl.BlockSpec((tm, LANES), lambda i: (i, 0)),
                   pl.BlockSpec((n_exp, tm), lambda i: (0, i)),
                   pl.BlockSpec((None, 1, LANES), lambda i: (i, 0, 0))],
        out_shape=[jax.ShapeDtypeStruct((nt * tm, d), BF16),
                   jax.ShapeDtypeStruct((nt * tm, LANES), F32),
                   jax.ShapeDtypeStruct((nt * tm, LANES), F32),
                   jax.ShapeDtypeStruct((n_exp, nt * tm), F32),
                   jax.ShapeDtypeStruct((nt, 1, LANES), jnp.int32)],
        compiler_params=_cparams(("parallel",)), name="route",
    )(x, wr_pad)


def _moe_kernel(tm, nfull_ref, half_ref, quarter_ref, xb_ref, rank_ref, gate_ref,
                rankt_ref, wg_ref, wu_ref, wd_ref, o_ref, xe_ref, oe_ref, rcol_ref, gcol_ref):
    i = pl.program_id(0)
    e = pl.program_id(1)
    f = pl.program_id(2)
    n_e = pl.num_programs(1)
    nb = nfull_ref[i * n_e + e]
    has_half = half_ref[i * n_e + e]
    has_quarter = quarter_ref[i * n_e + e]
    half, quarter = MOE_ROWS // 2, MOE_ROWS // 4
    half0 = pl.multiple_of(nb * MOE_ROWS, half)
    quarter0 = pl.multiple_of(nb * MOE_ROWS + has_half * half, quarter)

    def for_blocks(fn):
        def body(j, carry):
            fn(pl.multiple_of(j * MOE_ROWS, MOE_ROWS), MOE_ROWS)
            return carry
        lax.fori_loop(0, nb, body, 0)

        @pl.when(has_half == 1)
        def _():
            fn(half0, half)

        @pl.when(has_quarter == 1)
        def _():
            fn(quarter0, quarter)

    @pl.when((e == 0) & (f == 0))
    def _():
        o_ref[...] = jnp.zeros(o_ref.shape, F32)

    @pl.when(f == 0)
    def _():
        lane = lax.broadcasted_iota(jnp.int32, (tm, LANES), 1)
        rcol_ref[...] = jnp.max(jnp.where(lane == e, rank_ref[...], -1.0), axis=1, keepdims=True)
        gcol_ref[...] = jnp.sum(jnp.where(lane == e, gate_ref[...], 0.0), axis=1, keepdims=True)

        def gather(r0, rows):
            rrow = rankt_ref[pl.ds(e, 1), :]
            slot = (lax.broadcasted_iota(jnp.int32, (rows, tm), 0) + r0).astype(F32)
            onehot = jnp.where(rrow == slot, 1.0, 0.0).astype(BF16)
            xe_ref[pl.ds(r0, rows), :] = _dot(onehot, xb_ref[...]).astype(BF16)

        for_blocks(gather)

    def expert(r0, rows):
        xe = xe_ref[pl.ds(r0, rows), :]
        h = _silu(_dot(xe, wg_ref[...])) * _dot(xe, wu_ref[...])
        o = _dot(h.astype(BF16), wd_ref[...])

        @pl.when(f == 0)
        def _():
            oe_ref[pl.ds(r0, rows), :] = o

        @pl.when(f != 0)
        def _():
            oe_ref[pl.ds(r0, rows), :] += o

    for_blocks(expert)

    @pl.when(f == pl.num_programs(2) - 1)
    def _():
        def scatter(r0, rows):
            ob = oe_ref[pl.ds(r0, rows), :].astype(BF16)
            sb = min(tm, MOE_CHUNK)
            for t in range(tm // sb):
                ts = slice(t * sb, (t + 1) * sb)
                slot = (lax.broadcasted_iota(jnp.int32, (sb, rows), 1) + r0).astype(F32)
                onehot_t = jnp.where(rcol_ref[ts, :] == slot, 1.0, 0.0).astype(BF16)
                o_ref[ts, :] += gcol_ref[ts, :] * _dot(onehot_t, ob)

        for_blocks(scatter)


def _moe_call(n_tok, xb, rank, gate, rankt, blocks, wg, wu, wd, li, tm):
    d = xb.shape[1]
    nt = xb.shape[0] // tm
    _, n_exp, _, dff = wg.shape
    tf = _pick_tile(dff, (7 * MXU_WIDTH, 2 * MXU_WIDTH, MXU_WIDTH, LANES))
    once = pl.Buffered(1)
    tile = lambda i, e, f, *_: (i, 0)
    grid_spec = pltpu.PrefetchScalarGridSpec(
        num_scalar_prefetch=3,
        grid=(nt, n_exp, dff // tf),
        in_specs=[pl.BlockSpec((tm, d), tile, pipeline_mode=once),
                  pl.BlockSpec((tm, LANES), tile, pipeline_mode=once),
                  pl.BlockSpec((tm, LANES), tile, pipeline_mode=once),
                  pl.BlockSpec((n_exp, tm), lambda i, e, f, *_: (0, i), pipeline_mode=once),
                  pl.BlockSpec((None, None, d, tf), lambda i, e, f, *_: (li, e, 0, f)),
                  pl.BlockSpec((None, None, d, tf), lambda i, e, f, *_: (li, e, 0, f)),
                  pl.BlockSpec((None, None, tf, d), lambda i, e, f, *_: (li, e, f, 0))],
        out_specs=pl.BlockSpec((tm, d), tile, pipeline_mode=once),
        scratch_shapes=[pltpu.VMEM((tm, d), BF16), pltpu.VMEM((tm, d), F32),
                        pltpu.VMEM((tm, 1), F32), pltpu.VMEM((tm, 1), F32)],
    )
    return pl.pallas_call(
        functools.partial(_moe_kernel, tm),
        grid_spec=grid_spec,
        out_shape=jax.ShapeDtypeStruct((n_tok, d), F32),
        compiler_params=_cparams(("parallel", "arbitrary", "arbitrary")), name="moe",
    )(*blocks, xb, rank, gate, rankt, wg, wu, wd)


def _resid_ln_kernel(alpha, x_ref, y_ref, g_ref, b_ref, o_ref):
    o_ref[...] = _layernorm(alpha * x_ref[...] + y_ref[...], g_ref[...], b_ref[...])


def _resid_ln_call(x, y, g, b, alpha):
    n, d = x.shape
    tm = _pick_tile(n, (1024, 512, 256, 128, 64, 32))
    row = lambda i: (i, 0)
    const = lambda i: (0, 0)
    return pl.pallas_call(
        functools.partial(_resid_ln_kernel, alpha),
        grid=(n // tm,),
        in_specs=[pl.BlockSpec((tm, d), row), pl.BlockSpec((tm, d), row),
                  pl.BlockSpec((1, d), const), pl.BlockSpec((1, d), const)],
        out_specs=pl.BlockSpec((tm, d), row),
        out_shape=jax.ShapeDtypeStruct((n, d), F32),
        compiler_params=_cparams(("parallel",)), name="resid_ln",
    )(x, y, g, b)


def _moe_layer(x, w_router, wg, wu, wd, li, g, b, alpha):
    n, d = x.shape
    n_exp = w_router.shape[1]
    tm = 2048 if n >= 2048 else _pick_tile(n, (1024, 512, 256, 128))
    wr_pad = jnp.zeros((d, LANES), F32).at[:, :n_exp].set(w_router)
    xb, rank, gate, rankt, cnt = _route_call(x, wr_pad, n_exp, tm)
    cnt = cnt[:, 0, :n_exp].reshape(-1)
    half, quarter = MOE_ROWS // 2, MOE_ROWS // 4
    rem = cnt % MOE_ROWS
    n_full = cnt // MOE_ROWS + (rem > half + quarter).astype(jnp.int32)
    has_half = ((rem > quarter) & (rem <= half + quarter)).astype(jnp.int32)
    has_quarter = (((rem > 0) & (rem <= quarter)) | ((rem > half) & (rem <= half + quarter)))
    blocks = (n_full, has_half, has_quarter.astype(jnp.int32))
    y = _moe_call(n, xb, rank, gate, rankt, blocks, wg, wu, wd, li, tm)
    return _resid_ln_call(x, y, g, b, alpha)


def _swap_halves(w):
    half = w.shape[-1] // 2
    return jnp.concatenate([w[..., half:], w[..., :half]], axis=-1)


def _rope_tables(pos):
    inv = ROPE_THETA ** (-jnp.arange(QK_ROPE // 2, dtype=F32) * (2.0 / QK_ROPE))
    ang = pos.astype(F32)[:, None] * inv[None, :]
    cos, sin = jnp.cos(ang), jnp.sin(ang)
    n = pos.shape[0]
    pad = jnp.zeros((n, HEAD_PAD - QK_NOPE - QK_ROPE), F32)
    ct = jnp.concatenate([jnp.ones((n, QK_NOPE), F32), cos, cos, pad], axis=1)
    st = jnp.concatenate([jnp.zeros((n, QK_NOPE), F32), -sin, sin, pad], axis=1)
    return ct, st


def kernel(x_prompt, x_sample, cache_ckv, cache_kr, state_conv, ln_in_g, ln_in_b, w_in, b_in, g_qnorm, w_uq, g_kvnorm, w_uk, w_uv, w_o_attn, conv_w, conv_b, conv_ln_g, conv_ln_b, w_conv_out, w_out, ln1_g, ln1_b, ln2_g, ln2_b, w_ffn_gate, w_ffn_up, w_ffn_down, w_router, w_exp_gate, w_exp_up, w_exp_down):
    batch, seq, d = x_prompt.shape
    n_str, t_new, _ = x_sample.shape
    depth, _, past_len, kv_rank = cache_ckv.shape
    q_rank = g_qnorm.shape[1]
    c_conv = conv_w.shape[2]
    conv_width = conv_w.shape[1]
    halo = 32
    assert conv_width - 1 <= halo and t_new >= conv_width - 1 and t_new % 16 == 0
    assert w_uq.shape[2] == N_HEADS * (QK_NOPE + QK_ROPE) and cache_kr.shape[3] == QK_ROPE
    alpha = (2 * depth) ** 0.25
    n_p = batch * seq
    n_s = n_str * t_new
    n = n_p + n_s

    o1, o2, o3 = q_rank, q_rank + kv_rank, q_rank + kv_rank + QK_ROPE
    o4 = o3 + 2 * c_conv

    def place_kr(w):
        lead = jnp.zeros(w.shape[:-1] + (QK_NOPE,), F32)
        tail = jnp.zeros(w.shape[:-1] + (HEAD_PAD - QK_NOPE - QK_ROPE,), F32)
        return jnp.concatenate([lead, w, tail], axis=-1)

    def split_a(w):
        kr = w[..., o2:o3]
        return jnp.concatenate([w[..., :o2], place_kr(kr), place_kr(_swap_halves(kr))], axis=-1)

    wa = split_a(w_in).astype(BF16)
    ba = split_a(b_in)[:, None, :]
    wb = w_in[..., o3:].astype(BF16)
    bb = b_in[:, None, o3:]
    wq = w_uq.reshape(depth, q_rank, N_HEADS, QK_NOPE + QK_ROPE)
    wq_n, wq_r = wq[..., :QK_NOPE], wq[..., QK_NOPE:]
    z_tail = jnp.zeros(wq_r.shape[:-1] + (HEAD_PAD - QK_NOPE - QK_ROPE,), F32)
    wqa = jnp.concatenate([wq_n, wq_r, z_tail], -1).reshape(depth, q_rank, -1).astype(BF16)
    wqb = jnp.concatenate([jnp.zeros_like(wq_n), _swap_halves(wq_r), z_tail], -1)
    wqb = wqb.reshape(depth, q_rank, -1).astype(BF16)
    wuk = jnp.concatenate([w_uk, jnp.zeros(w_uk.shape[:-1] + (HEAD_PAD - QK_NOPE,), F32)], -1)
    wuk = wuk.reshape(depth, kv_rank, -1).astype(BF16)
    wuv = w_uv.reshape(depth, kv_rank, -1).astype(BF16)
    wuvt = jnp.transpose(w_uv, (0, 2, 3, 1))
    wuvt = jnp.concatenate([wuvt, jnp.zeros((depth, N_HEADS, VT_ROWS - V_HEAD, kv_rank), F32)], 2)
    wuvt = wuvt.reshape(depth, N_HEADS * VT_ROWS, kv_rank).astype(BF16)
    place = jnp.tile(place_kr(jnp.eye(QK_ROPE, dtype=F32)), (1, N_HEADS)).astype(BF16)
    woa = w_o_attn.astype(BF16)
    wco = w_conv_out.astype(BF16)
    wout = w_out.astype(BF16)
    wfg, wfu, wfd = w_ffn_gate.astype(BF16), w_ffn_up.astype(BF16), w_ffn_down.astype(BF16)
    weg, weu, wed = w_exp_gate.astype(BF16), w_exp_up.astype(BF16), w_exp_down.astype(BF16)

    ct_p, st_p = _rope_tables(jnp.arange(seq, dtype=jnp.int32))
    ct_s, st_s = _rope_tables(past_len + jnp.arange(t_new, dtype=jnp.int32))
    tabs = dict(ct=jnp.concatenate([jnp.tile(ct_p, (batch, 1)), jnp.tile(ct_s, (n_str, 1))]),
                st=jnp.concatenate([jnp.tile(st_p, (batch, 1)), jnp.tile(st_s, (n_str, 1))]))

    kpast, vpast = _kvpast_call(cache_ckv.reshape(depth, n_str * past_len, kv_rank),
                                cache_kr.reshape(depth, n_str * past_len, QK_ROPE), wuk, wuv, place)
    kpast = kpast.reshape(depth * n_str, past_len, -1)
    vpast = vpast.reshape(depth * n_str, past_len, -1)

    x = _ln_call(x_prompt.reshape(n_p, d), x_sample.reshape(n_s, d), ln_in_g, ln_in_b)
    tc = _pick_tile(seq, (512, 256, 128, 64))
    ws = dict(wa=wa, ba=ba, wb=wb, bb=bb, gq=g_qnorm[:, None, :], wqa=wqa, wqb=wqb,
              gkv=g_kvnorm[:, None, :], wuk=wuk, wuv=wuv, wuvt=wuvt, woa=woa, wco=wco, wout=wout,
              ln1g=ln1_g[:, None, :], ln1b=ln1_b[:, None, :])
    first_tile = (np.arange(n_p // tc) % (seq // tc) == 0)[:, None, None]
    keep = conv_width - 1
    ckv_l, kr_l, conv_p_l, conv_s_l = [], [], [], []
    for l in range(depth):
        q, k, v, vt, ckv, kr, u, gates = _inproj_call(x, ws, l, tabs, (q_rank, kv_rank, c_conv))
        attn_p = _attn_prompt_call(q, k, vt, batch, seq)
        attn_s = _attn_sample_call(q, k, v, kpast, vpast, l, n_p, n_str, t_new, past_len)
        tails = u[:n_p - tc].reshape(n_p // tc - 1, tc, c_conv)[:, tc - halo:, :]
        prev = jnp.concatenate([jnp.zeros((1, halo, c_conv), F32), tails], axis=0)
        halo_p = jnp.where(first_tile, 0.0, prev).reshape(-1, c_conv)
        halo_s = jnp.concatenate([jnp.zeros((n_str, halo - keep, c_conv), F32),
                                  state_conv[l]], axis=1).reshape(-1, c_conv)
        conv_args = (conv_w[l], conv_b[l], conv_ln_g[l], conv_ln_b[l])
        cact_p = _conv_call(u, 0, n_p // tc, halo_p, tc, *conv_args)
        cact_s = _conv_call(u, n_p, n_str, halo_s, t_new, *conv_args)
        x = _outproj_call(x, attn_p, attn_s, cact_p, cact_s, gates, ws, l, alpha)
        i = l // 2
        if l % 2 == 0:
            x = _ffn_call(x, wfg, wfu, wfd, i, ln2_g[l][None], ln2_b[l][None], alpha)
        else:
            x = _moe_layer(x, w_router[i], weg, weu, wed, i, ln2_g[l][None], ln2_b[l][None], alpha)
        ckv_l.append(ckv)
        kr_l.append(kr[:, QK_NOPE:QK_NOPE + QK_ROPE])
        conv_p_l.append(jnp.stack([u[(b + 1) * seq - keep:(b + 1) * seq] for b in range(batch)]))
        conv_s_l.append(u[n_p:].reshape(n_str, t_new, c_conv)[:, t_new - keep:, :])
    def stacked(parts, lo, hi, shape):
        return jnp.stack([p[lo:hi].reshape(shape) for p in parts])

    return (x[:n_p].reshape(batch, seq, d), x[n_p:].reshape(n_str, t_new, d),
            stacked(ckv_l, 0, n_p, (batch, seq, kv_rank)),
            stacked(kr_l, 0, n_p, (batch, seq, QK_ROPE)),
            jnp.stack(conv_p_l),
            stacked(ckv_l, n_p, n, (n_str, t_new, kv_rank)),
            stacked(kr_l, n_p, n, (n_str, t_new, QK_ROPE)),
            jnp.stack(conv_s_l))
```

```python
import functools
import math

import numpy as np
import jax
import jax.numpy as jnp
from jax import lax
from jax.experimental import pallas as pl
from jax.experimental.pallas import tpu as pltpu

CHUNK = 64
N_HEADS = 8
QK_NOPE = 64
QK_ROPE = 32
V_HEAD = 64
ROPE_THETA = 10000.0
TOP_K = 2
LN_EPS = 1e-5
NEG_INF = -1e30
ATTN_SCALE = (QK_NOPE + QK_ROPE) ** -0.5

LANES = 128
SUBLANES = 8
MXU_WIDTH = 256
HEAD_PAD = 128
HEADS_PER_STEP = 2
VT_ROWS = V_HEAD + 16
VT_BLOCK = 256
LOG2E = math.log2(math.e)
MOE_ROWS = 512
MOE_CHUNK = 512
VMEM_LIMIT = 56 * 1024 * 1024

BF16 = jnp.bfloat16
F32 = jnp.float32


def _cparams(sem):
    return pltpu.CompilerParams(dimension_semantics=sem, vmem_limit_bytes=VMEM_LIMIT)


def _pick_tile(n, candidates):
    for c in candidates:
        if n % c == 0:
            return c
    raise ValueError(f"no tile in {candidates} divides {n}")


def _dot(a, b):
    return jnp.dot(a, b, preferred_element_type=F32)


def _dot_nt(a, b):
    return lax.dot_general(a, b, (((1,), (1,)), ((), ())), preferred_element_type=F32)


def _layernorm(x, g, b):
    mu = jnp.mean(x, axis=-1, keepdims=True)
    xc = x - mu
    var = jnp.mean(xc * xc, axis=-1, keepdims=True)
    return xc * lax.rsqrt(var + LN_EPS) * g + b


def _rmsnorm(x, g):
    return x * lax.rsqrt(jnp.mean(x * x, axis=-1, keepdims=True) + LN_EPS) * g


def _silu(x):
    return x * jax.nn.sigmoid(x)


def _ln_kernel(p_tiles, xp_ref, xs_ref, g_ref, b_ref, o_ref):
    x = jnp.where(pl.program_id(0) < p_tiles, xp_ref[...], xs_ref[...])
    o_ref[...] = _layernorm(x, g_ref[...], b_ref[...])


def _ln_call(x_p, x_s, g, b):
    (n_p, d), n_s = x_p.shape, x_s.shape[0]
    tm = _pick_tile(math.gcd(n_p, n_s), (1024, 512, 256, 128, 64, 32))
    p_tiles = n_p // tm
    return pl.pallas_call(
        functools.partial(_ln_kernel, p_tiles),
        grid=((n_p + n_s) // tm,),
        in_specs=[pl.BlockSpec((tm, d), lambda i: (jnp.minimum(i, p_tiles - 1), 0)),
                  pl.BlockSpec((tm, d), lambda i: (jnp.maximum(i - p_tiles, 0), 0)),
                  pl.BlockSpec((1, d), lambda i: (0, 0)),
                  pl.BlockSpec((1, d), lambda i: (0, 0))],
        out_specs=pl.BlockSpec((tm, d), lambda i: (i, 0)),
        out_shape=jax.ShapeDtypeStruct((n_p + n_s, d), F32),
        compiler_params=_cparams(("parallel",)),
        name="ln_in",
    )(x_p, x_s, g.reshape(1, d), b.reshape(1, d))


def _inproj_kernel(q_rank, kv_rank, c_conv,
                   x_ref, wa_ref, ba_ref, wb_ref, bb_ref, gq_ref, wqa_ref, wqb_ref, gkv_ref,
                   wuk_ref, wuv_ref, wuvt_ref, ct_ref, st_ref,
                   q_ref, k_ref, v_ref, vt_ref, ckv_ref, kr_ref, u_ref, gate_ref):
    xb = x_ref[...].astype(BF16)
    pa = _dot(xb, wa_ref[...]) + ba_ref[...]
    o1, o2, o3 = q_rank, q_rank + kv_rank, q_rank + kv_rank + HEAD_PAD
    cqn = _rmsnorm(pa[:, :o1], gq_ref[...]).astype(BF16)
    ct = jnp.concatenate([ct_ref[...]] * N_HEADS, axis=1)
    st = jnp.concatenate([st_ref[...]] * N_HEADS, axis=1)
    q = (_dot(cqn, wqa_ref[...]) * ct + _dot(cqn, wqb_ref[...]) * st) * (ATTN_SCALE * LOG2E)
    q_ref[...] = q.astype(BF16)
    ckv = _rmsnorm(pa[:, o1:o2], gkv_ref[...])
    ckv_ref[...] = ckv
    kr = pa[:, o2:o3] * ct_ref[...] + pa[:, o3:] * st_ref[...]
    kr_ref[...] = kr
    ckvb = ckv.astype(BF16)
    k = _dot(ckvb, wuk_ref[...]) + jnp.concatenate([kr] * N_HEADS, axis=1)
    k_ref[...] = k.astype(BF16)
    v_ref[...] = _dot(ckvb, wuv_ref[...]).astype(BF16)
    vt = _dot_nt(wuvt_ref[...], ckvb)
    ones_row = lax.broadcasted_iota(jnp.int32, vt.shape, 0) % VT_ROWS >= V_HEAD
    vt_ref[...] = jnp.where(ones_row, 1.0, vt).astype(BF16)
    pu = _dot(xb, wb_ref[:, :2 * c_conv]) + bb_ref[:, :2 * c_conv]
    u_ref[...] = pu[:, :c_conv] * jax.nn.sigmoid(pu[:, c_conv:])
    pg = _dot(xb, wb_ref[:, 2 * c_conv:]) + bb_ref[:, 2 * c_conv:]
    gate_ref[...] = jax.nn.sigmoid(pg).astype(BF16)


def _layer_spec(w, layer):
    return pl.BlockSpec((None,) + w.shape[1:], lambda *_: (layer, 0, 0))


def _inproj_call(x, ws, layer, tabs, dims):
    n, d = x.shape
    q_rank, kv_rank, c_conv = dims
    tm = VT_BLOCK
    assert n % tm == 0
    hq = N_HEADS * HEAD_PAD
    hv = N_HEADS * V_HEAD
    hvt = N_HEADS * VT_ROWS
    row = lambda i: (i, 0)
    names = ("wa", "ba", "wb", "bb", "gq", "wqa", "wqb", "gkv", "wuk", "wuv", "wuvt")
    in_specs = ([pl.BlockSpec((tm, d), row)] + [_layer_spec(ws[k], layer) for k in names]
                + [pl.BlockSpec((tm, HEAD_PAD), row), pl.BlockSpec((tm, HEAD_PAD), row)])
    out_specs = [
        pl.BlockSpec((tm, hq), row), pl.BlockSpec((tm, hq), row), pl.BlockSpec((tm, hv), row),
        pl.BlockSpec((None, hvt, tm), lambda i: (i, 0, 0)),
        pl.BlockSpec((tm, kv_rank), row), pl.BlockSpec((tm, HEAD_PAD), row),
        pl.BlockSpec((tm, c_conv), row), pl.BlockSpec((tm, 2 * d), row),
    ]
    out_shape = [
        jax.ShapeDtypeStruct((n, hq), BF16), jax.ShapeDtypeStruct((n, hq), BF16),
        jax.ShapeDtypeStruct((n, hv), BF16), jax.ShapeDtypeStruct((n // tm, hvt, tm), BF16),
        jax.ShapeDtypeStruct((n, kv_rank), F32), jax.ShapeDtypeStruct((n, HEAD_PAD), F32),
        jax.ShapeDtypeStruct((n, c_conv), F32), jax.ShapeDtypeStruct((n, 2 * d), BF16),
    ]
    return pl.pallas_call(
        functools.partial(_inproj_kernel, q_rank, kv_rank, c_conv),
        grid=(n // tm,), in_specs=in_specs, out_specs=out_specs, out_shape=out_shape,
        compiler_params=_cparams(("parallel",)), name="inproj",
    )(x, *[ws[k] for k in names], tabs["ct"], tabs["st"])


def _kvpast_kernel(ckv_ref, kr_ref, wuk_ref, wuv_ref, place_ref, k_ref, v_ref):
    ckvb = ckv_ref[...].astype(BF16)
    k = _dot(ckvb, wuk_ref[...]) + _dot(kr_ref[...].astype(BF16), place_ref[...])
    k_ref[...] = k.astype(BF16)
    v_ref[...] = _dot(ckvb, wuv_ref[...]).astype(BF16)


def _kvpast_call(ckv, kr, wuk, wuv, place):
    depth, rows, kv_rank = ckv.shape
    rope = kr.shape[-1]
    tm = _pick_tile(rows, (2048, 1024, 512, 256, 128, 64, 32))
    tpl = rows // tm
    hq = N_HEADS * HEAD_PAD
    hv = N_HEADS * V_HEAD
    return pl.pallas_call(
        _kvpast_kernel,
        grid=(depth, tpl),
        in_specs=[pl.BlockSpec((None, tm, kv_rank), lambda l, i: (l, i, 0)),
                  pl.BlockSpec((None, tm, rope), lambda l, i: (l, i, 0)),
                  pl.BlockSpec((None, kv_rank, hq), lambda l, i: (l, 0, 0)),
                  pl.BlockSpec((None, kv_rank, hv), lambda l, i: (l, 0, 0)),
                  pl.BlockSpec((rope, hq), lambda l, i: (0, 0))],
        out_specs=[pl.BlockSpec((None, tm, hq), lambda l, i: (l, i, 0)),
                   pl.BlockSpec((None, tm, hv), lambda l, i: (l, i, 0))],
        out_shape=[jax.ShapeDtypeStruct((depth, rows, hq), BF16),
                   jax.ShapeDtypeStruct((depth, rows, hv), BF16)],
        compiler_params=_cparams(("parallel", "parallel")), name="kvpast",
    )(ckv, kr, wuk, wuv, place)


def _attn_prompt_kernel(tq, q_ref, k_ref, vt_ref, bias_ref, o_ref, sa_ref, sb_ref, m_ref, acc_ref):
    qi = pl.program_id(2)
    tk = tq // 2
    heads = range(HEADS_PER_STEP)
    halves = range(2)
    nvb = tk // VT_BLOCK

    def scores(j, s_ref, q_lo=0):
        start = pl.multiple_of(j * tk, tk)
        for h in heads:
            s_ref[h, :, q_lo:] = _dot_nt(k_ref[pl.ds(start, tk), h * HEAD_PAD:(h + 1) * HEAD_PAD],
                                         q_ref[q_lo:, h * HEAD_PAD:(h + 1) * HEAD_PAD])

    def softmax_pv(j, s_ref, mode):
        for h in heads:
            for c in halves:
                if mode[c] == "skip":
                    continue
                i = 2 * h + c
                s = s_ref[h, :, c * tk:(c + 1) * tk]
                if mode[c] == "diag":
                    s = s + bias_ref[...]
                m_old = m_ref[i]
                m_new = jnp.maximum(m_old, jnp.max(s, axis=0, keepdims=True))
                p = jnp.exp2(s - m_new).astype(BF16)
                pv = None
                for b in range(nvb):
                    vth = vt_ref[j * nvb + b, h * VT_ROWS:(h + 1) * VT_ROWS, :]
                    d = _dot(vth, p[b * VT_BLOCK:(b + 1) * VT_BLOCK, :])
                    pv = d if pv is None else pv + d
                acc_ref[i] = jnp.exp2(m_old - m_new) * acc_ref[i] + pv
                m_ref[i] = m_new

    full = ("full", "full")

    def pair(t):
        j = 2 * t
        scores(j + 1, sb_ref)
        softmax_pv(j, sa_ref, full)
        scores(j + 2, sa_ref)
        softmax_pv(j + 1, sb_ref, full)

    def two_pairs(u, carry):
        pair(2 * u)
        pair(2 * u + 1)
        return carry

    m_ref[...] = jnp.full(m_ref.shape, NEG_INF, F32)
    acc_ref[...] = jnp.zeros(acc_ref.shape, F32)
    scores(0, sa_ref)
    lax.fori_loop(0, qi // 2, two_pairs, 0)

    @pl.when(qi % 2 == 1)
    def _():
        pair(qi - 1)

    d0 = 2 * qi
    scores(d0 + 1, sb_ref, q_lo=tk)
    softmax_pv(d0, sa_ref, ("diag", "full"))
    softmax_pv(d0 + 1, sb_ref, ("skip", "diag"))
    for c in halves:
        rows = [acc_ref[2 * h + c] for h in heads]
        rows = [acc[:V_HEAD, :] / acc[V_HEAD:V_HEAD + 1, :] for acc in rows]
        o_ref[c * tk:(c + 1) * tk, :] = jnp.transpose(jnp.concatenate(rows, axis=0)).astype(BF16)


def _attn_prompt_call(q, k, vt, batch, seq):
    tq = _pick_tile(seq, (1024, 512))
    tk = tq // 2
    nq = seq // tq
    hp = N_HEADS // HEADS_PER_STEP
    qw = HEADS_PER_STEP * HEAD_PAD
    vw = HEADS_PER_STEP * V_HEAD
    vr = HEADS_PER_STEP * VT_ROWS
    nvb = seq // VT_BLOCK
    chunk = np.arange(tk) // CHUNK
    bias = np.where(chunk[:, None] <= chunk[None, :], 0.0, NEG_INF).astype(np.float32)
    return pl.pallas_call(
        functools.partial(_attn_prompt_kernel, tq),
        grid=(batch, hp, nq),
        in_specs=[pl.BlockSpec((tq, qw), lambda b, h, i: (b * nq + i, h)),
                  pl.BlockSpec((seq, qw), lambda b, h, i: (b, h)),
                  pl.BlockSpec((nvb, vr, VT_BLOCK), lambda b, h, i: (b, h, 0)),
                  pl.BlockSpec((tk, tk), lambda b, h, i: (0, 0))],
        out_specs=pl.BlockSpec((tq, vw), lambda b, h, i: (b * nq + i, h)),
        out_shape=jax.ShapeDtypeStruct((batch * seq, N_HEADS * V_HEAD), BF16),
        scratch_shapes=[pltpu.VMEM((HEADS_PER_STEP, tk, tq), F32),
                        pltpu.VMEM((HEADS_PER_STEP, tk, tq), F32),
                        pltpu.VMEM((2 * HEADS_PER_STEP, 1, tk), F32),
                        pltpu.VMEM((2 * HEADS_PER_STEP, VT_ROWS, tk), F32)],
        compiler_params=_cparams(("parallel", "parallel", "arbitrary")), name="attn_prompt",
    )(q, k, vt, jnp.asarray(bias))


def _attn_sample_kernel(mask_past, mask_new, past_len,
                        q_ref, kn_ref, vn_ref, kp_ref, vp_ref, o_ref):
    vw = HEADS_PER_STEP * V_HEAD
    for hp in range(N_HEADS // HEADS_PER_STEP):
        _attn_sample_pair(mask_past, mask_new, past_len, hp * HEADS_PER_STEP * HEAD_PAD,
                          q_ref, kn_ref, kp_ref, vn_ref[:, hp * vw:(hp + 1) * vw],
                          vp_ref[:, hp * vw:(hp + 1) * vw], o_ref.at[:, hp * vw:(hp + 1) * vw])


def _attn_sample_pair(mask_past, mask_new, past_len, col0, q_ref, kn_ref, kp_ref, vn, vp, o_ref):
    outs = []
    for h in range(HEADS_PER_STEP):
        sl = slice(col0 + h * HEAD_PAD, col0 + (h + 1) * HEAD_PAD)
        qh = q_ref[:, sl]
        sp = _dot_nt(qh, kp_ref[:, sl])
        sn = _dot_nt(qh, kn_ref[:, sl])
        if mask_past:
            qc = (lax.broadcasted_iota(jnp.int32, sp.shape, 0) + past_len) // CHUNK
            kc = lax.broadcasted_iota(jnp.int32, sp.shape, 1) // CHUNK
            sp = jnp.where(kc <= qc, sp, NEG_INF)
        if mask_new:
            qc = (lax.broadcasted_iota(jnp.int32, sn.shape, 0) + past_len) // CHUNK
            kc = (lax.broadcasted_iota(jnp.int32, sn.shape, 1) + past_len) // CHUNK
            sn = jnp.where(kc <= qc, sn, NEG_INF)
        m = jnp.maximum(jnp.max(sp, axis=1, keepdims=True), jnp.max(sn, axis=1, keepdims=True))
        pp = jnp.exp2(sp - m)
        pn = jnp.exp2(sn - m)
        l = jnp.sum(pp, axis=1, keepdims=True) + jnp.sum(pn, axis=1, keepdims=True)
        outs.append((_dot(pp.astype(BF16), vp) + _dot(pn.astype(BF16), vn)) / l)
    lane = lax.broadcasted_iota(jnp.int32, outs[0].shape, 1)
    o_ref[...] = jnp.where(lane < V_HEAD, outs[0], outs[1]).astype(BF16)


def _attn_sample_call(q, k, v, kpast, vpast, layer, row0, n_streams, t_new, past_len):
    qw = N_HEADS * HEAD_PAD
    vw = N_HEADS * V_HEAD
    blk0 = row0 // t_new
    q_pos = past_len + np.arange(t_new)
    mask_past = not bool(np.all((np.arange(past_len) // CHUNK)[None, :] <= (q_pos // CHUNK)[:, None]))
    mask_new = not bool(np.all((q_pos // CHUNK)[None, :] <= (q_pos // CHUNK)[:, None]))
    new_spec = lambda w: pl.BlockSpec((t_new, w), lambda s: (blk0 + s, 0))
    return pl.pallas_call(
        functools.partial(_attn_sample_kernel, mask_past, mask_new, past_len),
        grid=(n_streams,),
        in_specs=[new_spec(qw), new_spec(qw), new_spec(vw),
                  pl.BlockSpec((None, past_len, qw), lambda s: (layer * n_streams + s, 0, 0)),
                  pl.BlockSpec((None, past_len, vw), lambda s: (layer * n_streams + s, 0, 0))],
        out_specs=pl.BlockSpec((t_new, vw), lambda s: (s, 0)),
        out_shape=jax.ShapeDtypeStruct((n_streams * t_new, vw), BF16),
        compiler_params=_cparams(("parallel",)), name="attn_sample",
    )(q, k, v, kpast, vpast)


def _conv_kernel(tc, halo, conv_w, rc, u_ref, h_ref, w_ref, b_ref, g_ref, beta_ref, o_ref, sh_ref):
    sh_ref[0, 0:halo, :] = h_ref[...]
    sh_ref[0, halo:halo + tc, :] = u_ref[...]
    span = halo + tc - SUBLANES
    for r in range(1, SUBLANES):
        sh_ref[r, 0:span, :] = sh_ref[0, r:r + span, :]
    lead = halo - (conv_w - 1)
    for c in range(tc // rc):
        acc = jnp.broadcast_to(b_ref[...], (rc, u_ref.shape[1]))
        for j in range(conv_w):
            r = (lead + j) % SUBLANES
            r0 = c * rc + lead + j - r
            acc = acc + w_ref[j:j + 1, :] * sh_ref[r, r0:r0 + rc, :]
        y = _layernorm(acc, g_ref[...], beta_ref[...])
        o_ref[c * rc:(c + 1) * rc, :] = _silu(y).astype(BF16)


def _conv_call(u, row0, n_tiles, halo_rows, tc, w, b, g, beta):
    c = u.shape[1]
    blk0 = row0 // tc
    halo = halo_rows.shape[0] // n_tiles
    conv_w = w.shape[0]
    assert halo % SUBLANES == 0 and conv_w - 1 <= halo
    rc = min(tc, 64)
    const = lambda i: (0, 0)
    return pl.pallas_call(
        functools.partial(_conv_kernel, tc, halo, conv_w, rc),
        grid=(n_tiles,),
        in_specs=[pl.BlockSpec((tc, c), lambda i: (blk0 + i, 0)),
                  pl.BlockSpec((halo, c), lambda i: (i, 0)),
                  pl.BlockSpec((conv_w, c), const), pl.BlockSpec((1, c), const),
                  pl.BlockSpec((1, c), const), pl.BlockSpec((1, c), const)],
        out_specs=pl.BlockSpec((tc, c), lambda i: (i, 0)),
        out_shape=jax.ShapeDtypeStruct((n_tiles * tc, c), BF16),
        scratch_shapes=[pltpu.VMEM((SUBLANES, halo + tc, c), F32)],
        compiler_params=_cparams(("parallel",)), name="conv",
    )(u, halo_rows, w, b.reshape(1, c), g.reshape(1, c), beta.reshape(1, c))


def _outproj_kernel(alpha, d, p_tiles, x_ref, ap_ref, as_ref, cp_ref, cs_ref, gate_ref,
                    woa_ref, wco_ref, wout_ref, g_ref, b_ref, o_ref):
    is_prompt = pl.program_id(0) < p_tiles
    br_a = _dot(jnp.where(is_prompt, ap_ref[...], as_ref[...]), woa_ref[...])
    br_c = _dot(jnp.where(is_prompt, cp_ref[...], cs_ref[...]), wco_ref[...])
    merged = gate_ref[:, :d].astype(F32) * br_a + gate_ref[:, d:].astype(F32) * br_c
    mix = _dot(merged.astype(BF16), wout_ref[...])
    o_ref[...] = _layernorm(alpha * x_ref[...] + mix, g_ref[...], b_ref[...])


def _outproj_call(x, attn_p, attn_s, cact_p, cact_s, gates, ws, layer, alpha):
    n, d = x.shape
    n_p, n_s = attn_p.shape[0], attn_s.shape[0]
    tm = _pick_tile(math.gcd(n_p, n_s), (512, 256, 128, 64, 32))
    p_tiles = n_p // tm
    names = ("woa", "wco", "wout", "ln1g", "ln1b")
    row = lambda i: (i, 0)
    prow = lambda i: (jnp.minimum(i, p_tiles - 1), 0)
    srow = lambda i: (jnp.maximum(i - p_tiles, 0), 0)
    ha, hc = attn_p.shape[1], cact_p.shape[1]
    return pl.pallas_call(
        functools.partial(_outproj_kernel, alpha, d, p_tiles),
        grid=(n // tm,),
        in_specs=[pl.BlockSpec((tm, d), row), pl.BlockSpec((tm, ha), prow),
                  pl.BlockSpec((tm, ha), srow), pl.BlockSpec((tm, hc), prow),
                  pl.BlockSpec((tm, hc), srow), pl.BlockSpec((tm, 2 * d), row)]
                 + [_layer_spec(ws[k], layer) for k in names],
        out_specs=pl.BlockSpec((tm, d), row),
        out_shape=jax.ShapeDtypeStruct((n, d), F32),
        compiler_params=_cparams(("parallel",)), name="outproj",
    )(x, attn_p, attn_s, cact_p, cact_s, gates, *[ws[k] for k in names])


def _ffn_kernel(alpha, x_ref, wg_ref, wu_ref, wd_ref, g_ref, b_ref, o_ref, acc_ref):
    f = pl.program_id(1)

    @pl.when(f == 0)
    def _():
        acc_ref[...] = jnp.zeros(acc_ref.shape, F32)

    xb = x_ref[...].astype(BF16)
    h = _silu(_dot(xb, wg_ref[...])) * _dot(xb, wu_ref[...])
    acc_ref[...] += _dot(h.astype(BF16), wd_ref[...])

    @pl.when(f == pl.num_programs(1) - 1)
    def _():
        o_ref[...] = _layernorm(alpha * x_ref[...] + acc_ref[...], g_ref[...], b_ref[...])


def _ffn_call(x, wg, wu, wd, li, g, b, alpha):
    n, d = x.shape
    dff = wg.shape[2]
    tm = _pick_tile(n, (512, 256, 128, 64, 32))
    tf = dff
    once = pl.Buffered(1)
    return pl.pallas_call(
        functools.partial(_ffn_kernel, alpha),
        grid=(n // tm, dff // tf),
        in_specs=[pl.BlockSpec((tm, d), lambda i, f: (i, 0)),
                  pl.BlockSpec((None, d, tf), lambda i, f: (li, 0, f), pipeline_mode=once),
                  pl.BlockSpec((None, d, tf), lambda i, f: (li, 0, f), pipeline_mode=once),
                  pl.BlockSpec((None, tf, d), lambda i, f: (li, f, 0), pipeline_mode=once),
                  pl.BlockSpec((1, d), lambda i, f: (0, 0)),
                  pl.BlockSpec((1, d), lambda i, f: (0, 0))],
        out_specs=pl.BlockSpec((tm, d), lambda i, f: (i, 0)),
        out_shape=jax.ShapeDtypeStruct((n, d), F32),
        scratch_shapes=[pltpu.VMEM((tm, d), F32)],
        compiler_params=_cparams(("parallel", "arbitrary")), name="ffn",
    )(x, wg, wu, wd, g, b)


def _route_kernel(n_tok, n_exp, tm, x_ref, wr_ref, xb_ref, rank_ref, gate_ref, rankt_ref, cnt_ref):
    i = pl.program_id(0)
    x = x_ref[...]
    row = lax.broadcasted_iota(jnp.int32, (tm, 1), 0) + i * tm
    valid = row < n_tok
    xb_ref[...] = jnp.where(valid, x, 0.0).astype(BF16)
    logits = jnp.dot(x, wr_ref[...], preferred_element_type=F32, precision=lax.Precision.HIGHEST)
    lane = lax.broadcasted_iota(jnp.int32, (tm, LANES), 1)
    logits = jnp.where(lane < n_exp, logits, -jnp.inf)
    m1 = jnp.max(logits, axis=1, keepdims=True)
    i1 = jnp.min(jnp.where(logits == m1, lane, LANES), axis=1, keepdims=True)
    rest = jnp.where(lane == i1, -jnp.inf, logits)
    m2 = jnp.max(rest, axis=1, keepdims=True)
    i2 = jnp.min(jnp.where(rest == m2, lane, LANES), axis=1, keepdims=True)
    e2 = jnp.exp(m2 - m1)
    g1 = 1.0 / (1.0 + e2)
    g2 = e2 / (1.0 + e2)
    sel1 = (lane == i1) & valid
    sel2 = (lane == i2) & valid
    gate_ref[...] = jnp.where(sel1, g1, 0.0) + jnp.where(sel2, g2, 0.0)
    sel = jnp.where(sel1 | sel2, 1.0, 0.0)
    cs = min(tm, 256)
    r = lax.broadcasted_iota(jnp.int32, (cs, cs), 0)
    c = lax.broadcasted_iota(jnp.int32, (cs, cs), 1)
    lower = jnp.where(c < r, 1.0, 0.0).astype(BF16)
    offs = jnp.zeros((1, LANES), F32)
    for ch in range(tm // cs):
        sc = sel[ch * cs:(ch + 1) * cs, :]
        rk = _dot(lower, sc.astype(BF16)) + offs
        rank_ref[ch * cs:(ch + 1) * cs, :] = jnp.where(sc > 0.0, rk, -1.0)
        offs = offs + jnp.sum(sc, axis=0, keepdims=True)
    cnt_ref[...] = offs.astype(jnp.int32)
    rankt_ref[...] = jnp.transpose(rank_ref[...])[:rankt_ref.shape[0], :]


def _route_call(x, wr_pad, n_exp, tm):
    n, d = x.shape
    nt = -(-n // tm)
    return pl.pallas_call(
        functools.partial(_route_kernel, n, n_exp, tm),
        grid=(nt,),
        in_specs=[pl.BlockSpec((tm, d), lambda i: (i, 0)),
                  pl.BlockSpec((d, LANES), lambda i: (0, 0))],
        out_specs=[pl.BlockSpec((tm, d), lambda i: (i, 0)),
                   pl.BlockSpec((tm, LANES), lambda i: (i, 0)),
                   pl.BlockSpec((tm, LANES), lambda i: (i, 0)),
                   pl.BlockSpec((n_exp, tm), lambda i: (0, i)),
                   pl.BlockSpec((None, 1, LANES), lambda i: (i, 0, 0))],
        out_shape=[jax.ShapeDtypeStruct((nt * tm, d), BF16),
                   jax.ShapeDtypeStruct((nt * tm, LANES), F32),
                   jax.ShapeDtypeStruct((nt * tm, LANES), F32),
                   jax.ShapeDtypeStruct((n_exp, nt * tm), F32),
                   jax.ShapeDtypeStruct((nt, 1, LANES), jnp.int32)],
        compiler_params=_cparams(("parallel",)), name="route",
    )(x, wr_pad)


def _moe_kernel(tm, nfull_ref, half_ref, quarter_ref, xb_ref, rank_ref, gate_ref,
                rankt_ref, wg_ref, wu_ref, wd_ref, o_ref, xe_ref, oe_ref, rcol_ref, gcol_ref):
    i = pl.program_id(0)
    e = pl.program_id(1)
    f = pl.program_id(2)
    n_e = pl.num_programs(1)
    nb = nfull_ref[i * n_e + e]
    has_half = half_ref[i * n_e + e]
    has_quarter = quarter_ref[i * n_e + e]
    half, quarter = MOE_ROWS // 2, MOE_ROWS // 4
    half0 = pl.multiple_of(nb * MOE_ROWS, half)
    quarter0 = pl.multiple_of(nb * MOE_ROWS + has_half * half, quarter)

    def for_blocks(fn):
        def body(j, carry):
            fn(pl.multiple_of(j * MOE_ROWS, MOE_ROWS), MOE_ROWS)
            return carry
        lax.fori_loop(0, nb, body, 0)

        @pl.when(has_half == 1)
        def _():
            fn(half0, half)

        @pl.when(has_quarter == 1)
        def _():
            fn(quarter0, quarter)

    @pl.when((e == 0) & (f == 0))
    def _():
        o_ref[...] = jnp.zeros(o_ref.shape, F32)

    @pl.when(f == 0)
    def _():
        lane = lax.broadcasted_iota(jnp.int32, (tm, LANES), 1)
        rcol_ref[...] = jnp.max(jnp.where(lane == e, rank_ref[...], -1.0), axis=1, keepdims=True)
        gcol_ref[...] = jnp.sum(jnp.where(lane == e, gate_ref[...], 0.0), axis=1, keepdims=True)

        def gather(r0, rows):
            rrow = rankt_ref[pl.ds(e, 1), :]
            slot = (lax.broadcasted_iota(jnp.int32, (rows, tm), 0) + r0).astype(F32)
            onehot = jnp.where(rrow == slot, 1.0, 0.0).astype(BF16)
            xe_ref[pl.ds(r0, rows), :] = _dot(onehot, xb_ref[...]).astype(BF16)

        for_blocks(gather)

    def expert(r0, rows):
        xe = xe_ref[pl.ds(r0, rows), :]
        h = _silu(_dot(xe, wg_ref[...])) * _dot(xe, wu_ref[...])
        o = _dot(h.astype(BF16), wd_ref[...])

        @pl.when(f == 0)
        def _():
            oe_ref[pl.ds(r0, rows), :] = o

        @pl.when(f != 0)
        def _():
            oe_ref[pl.ds(r0, rows), :] += o

    for_blocks(expert)

    @pl.when(f == pl.num_programs(2) - 1)
    def _():
        def scatter(r0, rows):
            ob = oe_ref[pl.ds(r0, rows), :].astype(BF16)
            sb = min(tm, MOE_CHUNK)
            for t in range(tm // sb):
                ts = slice(t * sb, (t + 1) * sb)
                slot = (lax.broadcasted_iota(jnp.int32, (sb, rows), 1) + r0).astype(F32)
                onehot_t = jnp.where(rcol_ref[ts, :] == slot, 1.0, 0.0).astype(BF16)
                o_ref[ts, :] += gcol_ref[ts, :] * _dot(onehot_t, ob)

        for_blocks(scatter)


def _moe_call(n_tok, xb, rank, gate, rankt, blocks, wg, wu, wd, li, tm):
    d = xb.shape[1]
    nt = xb.shape[0] // tm
    _, n_exp, _, dff = wg.shape
    tf = _pick_tile(dff, (7 * MXU_WIDTH, 2 * MXU_WIDTH, MXU_WIDTH, LANES))
    once = pl.Buffered(1)
    tile = lambda i, e, f, *_: (i, 0)
    grid_spec = pltpu.PrefetchScalarGridSpec(
        num_scalar_prefetch=3,
        grid=(nt, n_exp, dff // tf),
        in_specs=[pl.BlockSpec((tm, d), tile, pipeline_mode=once),
                  pl.BlockSpec((tm, LANES), tile, pipeline_mode=once),
                  pl.BlockSpec((tm, LANES), tile, pipeline_mode=once),
                  pl.BlockSpec((n_exp, tm), lambda i, e, f, *_: (0, i), pipeline_mode=once),
                  pl.BlockSpec((None, None, d, tf), lambda i, e, f, *_: (li, e, 0, f)),
                  pl.BlockSpec((None, None, d, tf), lambda i, e, f, *_: (li, e, 0, f)),
                  pl.BlockSpec((None, None, tf, d), lambda i, e, f, *_: (li, e, f, 0))],
        out_specs=pl.BlockSpec((tm, d), tile, pipeline_mode=once),
        scratch_shapes=[pltpu.VMEM((tm, d), BF16), pltpu.VMEM((tm, d), F32),
                        pltpu.VMEM((tm, 1), F32), pltpu.VMEM((tm, 1), F32)],
    )
    return pl.pallas_call(
        functools.partial(_moe_kernel, tm),
        grid_spec=grid_spec,
        out_shape=jax.ShapeDtypeStruct((n_tok, d), F32),
        compiler_params=_cparams(("parallel", "arbitrary", "arbitrary")), name="moe",
    )(*blocks, xb, rank, gate, rankt, wg, wu, wd)


def _resid_ln_kernel(alpha, x_ref, y_ref, g_ref, b_ref, o_ref):
    o_ref[...] = _layernorm(alpha * x_ref[...] + y_ref[...], g_ref[...], b_ref[...])


def _resid_ln_call(x, y, g, b, alpha):
    n, d = x.shape
    tm = _pick_tile(n, (1024, 512, 256, 128, 64, 32))
    row = lambda i: (i, 0)
    const = lambda i: (0, 0)
    return pl.pallas_call(
        functools.partial(_resid_ln_kernel, alpha),
        grid=(n // tm,),
        in_specs=[pl.BlockSpec((tm, d), row), pl.BlockSpec((tm, d), row),
                  pl.BlockSpec((1, d), const), pl.BlockSpec((1, d), const)],
        out_specs=pl.BlockSpec((tm, d), row),
        out_shape=jax.ShapeDtypeStruct((n, d), F32),
        compiler_params=_cparams(("parallel",)), name="resid_ln",
    )(x, y, g, b)


def _moe_layer(x, w_router, wg, wu, wd, li, g, b, alpha):
    n, d = x.shape
    n_exp = w_router.shape[1]
    tm = 2048 if n >= 2048 else _pick_tile(n, (1024, 512, 256, 128))
    wr_pad = jnp.zeros((d, LANES), F32).at[:, :n_exp].set(w_router)
    xb, rank, gate, rankt, cnt = _route_call(x, wr_pad, n_exp, tm)
    cnt = cnt[:, 0, :n_exp].reshape(-1)
    half, quarter = MOE_ROWS // 2, MOE_ROWS // 4
    rem = cnt % MOE_ROWS
    n_full = cnt // MOE_ROWS + (rem > half + quarter).astype(jnp.int32)
    has_half = ((rem > quarter) & (rem <= half + quarter)).astype(jnp.int32)
    has_quarter = (((rem > 0) & (rem <= quarter)) | ((rem > half) & (rem <= half + quarter)))
    blocks = (n_full, has_half, has_quarter.astype(jnp.int32))
    y = _moe_call(n, xb, rank, gate, rankt, blocks, wg, wu, wd, li, tm)
    return _resid_ln_call(x, y, g, b, alpha)


def _swap_halves(w):
    half = w.shape[-1] // 2
    return jnp.concatenate([w[..., half:], w[..., :half]], axis=-1)


def _rope_tables(pos):
    inv = ROPE_THETA ** (-jnp.arange(QK_ROPE // 2, dtype=F32) * (2.0 / QK_ROPE))
    ang = pos.astype(F32)[:, None] * inv[None, :]
    cos, sin = jnp.cos(ang), jnp.sin(ang)
    n = pos.shape[0]
    pad = jnp.zeros((n, HEAD_PAD - QK_NOPE - QK_ROPE), F32)
    ct = jnp.concatenate([jnp.ones((n, QK_NOPE), F32), cos, cos, pad], axis=1)
    st = jnp.concatenate([jnp.zeros((n, QK_NOPE), F32), -sin, sin, pad], axis=1)
    return ct, st


def kernel(x_prompt, x_sample, cache_ckv, cache_kr, state_conv, ln_in_g, ln_in_b, w_in, b_in, g_qnorm, w_uq, g_kvnorm, w_uk, w_uv, w_o_attn, conv_w, conv_b, conv_ln_g, conv_ln_b, w_conv_out, w_out, ln1_g, ln1_b, ln2_g, ln2_b, w_ffn_gate, w_ffn_up, w_ffn_down, w_router, w_exp_gate, w_exp_up, w_exp_down):
    batch, seq, d = x_prompt.shape
    n_str, t_new, _ = x_sample.shape
    depth, _, past_len, kv_rank = cache_ckv.shape
    q_rank = g_qnorm.shape[1]
    c_conv = conv_w.shape[2]
    conv_width = conv_w.shape[1]
    halo = 32
    assert conv_width - 1 <= halo and t_new >= conv_width - 1 and t_new % 16 == 0
    assert w_uq.shape[2] == N_HEADS * (QK_NOPE + QK_ROPE) and cache_kr.shape[3] == QK_ROPE
    alpha = (2 * depth) ** 0.25
    n_p = batch * seq
    n_s = n_str * t_new
    n = n_p + n_s

    o1, o2, o3 = q_rank, q_rank + kv_rank, q_rank + kv_rank + QK_ROPE
    o4 = o3 + 2 * c_conv

    def place_kr(w):
        lead = jnp.zeros(w.shape[:-1] + (QK_NOPE,), F32)
        tail = jnp.zeros(w.shape[:-1] + (HEAD_PAD - QK_NOPE - QK_ROPE,), F32)
        return jnp.concatenate([lead, w, tail], axis=-1)

    def split_a(w):
        kr = w[..., o2:o3]
        return jnp.concatenate([w[..., :o2], place_kr(kr), place_kr(_swap_halves(kr))], axis=-1)

    wa = split_a(w_in).astype(BF16)
    ba = split_a(b_in)[:, None, :]
    wb = w_in[..., o3:].astype(BF16)
    bb = b_in[:, None, o3:]
    wq = w_uq.reshape(depth, q_rank, N_HEADS, QK_NOPE + QK_ROPE)
    wq_n, wq_r = wq[..., :QK_NOPE], wq[..., QK_NOPE:]
    z_tail = jnp.zeros(wq_r.shape[:-1] + (HEAD_PAD - QK_NOPE - QK_ROPE,), F32)
    wqa = jnp.concatenate([wq_n, wq_r, z_tail], -1).reshape(depth, q_rank, -1).astype(BF16)
    wqb = jnp.concatenate([jnp.zeros_like(wq_n), _swap_halves(wq_r), z_tail], -1)
    wqb = wqb.reshape(depth, q_rank, -1).astype(BF16)
    wuk = jnp.concatenate([w_uk, jnp.zeros(w_uk.shape[:-1] + (HEAD_PAD - QK_NOPE,), F32)], -1)
    wuk = wuk.reshape(depth, kv_rank, -1).astype(BF16)
    wuv = w_uv.reshape(depth, kv_rank, -1).astype(BF16)
    wuvt = jnp.transpose(w_uv, (0, 2, 3, 1))
    wuvt = jnp.concatenate([wuvt, jnp.zeros((depth, N_HEADS, VT_ROWS - V_HEAD, kv_rank), F32)], 2)
    wuvt = wuvt.reshape(depth, N_HEADS * VT_ROWS, kv_rank).astype(BF16)
    place = jnp.tile(place_kr(jnp.eye(QK_ROPE, dtype=F32)), (1, N_HEADS)).astype(BF16)
    woa = w_o_attn.astype(BF16)
    wco = w_conv_out.astype(BF16)
    wout = w_out.astype(BF16)
    wfg, wfu, wfd = w_ffn_gate.astype(BF16), w_ffn_up.astype(BF16), w_ffn_down.astype(BF16)
    weg, weu, wed = w_exp_gate.astype(BF16), w_exp_up.astype(BF16), w_exp_down.astype(BF16)

    ct_p, st_p = _rope_tables(jnp.arange(seq, dtype=jnp.int32))
    ct_s, st_s = _rope_tables(past_len + jnp.arange(t_new, dtype=jnp.int32))
    tabs = dict(ct=jnp.concatenate([jnp.tile(ct_p, (batch, 1)), jnp.tile(ct_s, (n_str, 1))]),
                st=jnp.concatenate([jnp.tile(st_p, (batch, 1)), jnp.tile(st_s, (n_str, 1))]))

    kpast, vpast = _kvpast_call(cache_ckv.reshape(depth, n_str * past_len, kv_rank),
                                cache_kr.reshape(depth, n_str * past_len, QK_ROPE), wuk, wuv, place)
    kpast = kpast.reshape(depth * n_str, past_len, -1)
    vpast = vpast.reshape(depth * n_str, past_len, -1)

    x = _ln_call(x_prompt.reshape(n_p, d), x_sample.reshape(n_s, d), ln_in_g, ln_in_b)
    tc = _pick_tile(seq, (512, 256, 128, 64))
    ws = dict(wa=wa, ba=ba, wb=wb, bb=bb, gq=g_qnorm[:, None, :], wqa=wqa, wqb=wqb,
              gkv=g_kvnorm[:, None, :], wuk=wuk, wuv=wuv, wuvt=wuvt, woa=woa, wco=wco, wout=wout,
              ln1g=ln1_g[:, None, :], ln1b=ln1_b[:, None, :])
    first_tile = (np.arange(n_p // tc) % (seq // tc) == 0)[:, None, None]
    keep = conv_width - 1
    ckv_l, kr_l, conv_p_l, conv_s_l = [], [], [], []
    for l in range(depth):
        q, k, v, vt, ckv, kr, u, gates = _inproj_call(x, ws, l, tabs, (q_rank, kv_rank, c_conv))
        attn_p = _attn_prompt_call(q, k, vt, batch, seq)
        attn_s = _attn_sample_call(q, k, v, kpast, vpast, l, n_p, n_str, t_new, past_len)
        tails = u[:n_p - tc].reshape(n_p // tc - 1, tc, c_conv)[:, tc - halo:, :]
        prev = jnp.concatenate([jnp.zeros((1, halo, c_conv), F32), tails], axis=0)
        halo_p = jnp.where(first_tile, 0.0, prev).reshape(-1, c_conv)
        halo_s = jnp.concatenate([jnp.zeros((n_str, halo - keep, c_conv), F32),
                                  state_conv[l]], axis=1).reshape(-1, c_conv)
        conv_args = (conv_w[l], conv_b[l], conv_ln_g[l], conv_ln_b[l])
        cact_p = _conv_call(u, 0, n_p // tc, halo_p, tc, *conv_args)
        cact_s = _conv_call(u, n_p, n_str, halo_s, t_new, *conv_args)
        x = _outproj_call(x, attn_p, attn_s, cact_p, cact_s, gates, ws, l, alpha)
        i = l // 2
        if l % 2 == 0:
            x = _ffn_call(x, wfg, wfu, wfd, i, ln2_g[l][None], ln2_b[l][None], alpha)
        else:
            x = _moe_layer(x, w_router[i], weg, weu, wed, i, ln2_g[l][None], ln2_b[l][None], alpha)
        ckv_l.append(ckv)
        kr_l.append(kr[:, QK_NOPE:QK_NOPE + QK_ROPE])
        conv_p_l.append(jnp.stack([u[(b + 1) * seq - keep:(b + 1) * seq] for b in range(batch)]))
        conv_s_l.append(u[n_p:].reshape(n_str, t_new, c_conv)[:, t_new - keep:, :])
    def stacked(parts, lo, hi, shape):
        return jnp.stack([p[lo:hi].reshape(shape) for p in parts])

    return (x[:n_p].reshape(batch, seq, d), x[n_p:].reshape(n_str, t_new, d),
            stacked(ckv_l, 0, n_p, (batch, seq, kv_rank)),
            stacked(kr_l, 0, n_p, (batch, seq, QK_ROPE)),
            jnp.stack(conv_p_l),
            stacked(ckv_l, n_p, n, (n_str, t_new, kv_rank)),
            stacked(kr_l, n_p, n, (n_str, t_new, QK_ROPE)),
            jnp.stack(conv_s_l))
```
